```python
import math
import jax
import jax.numpy as jnp
from jax import lax
import numpy as np

D_MODEL = 2048
BATCH = 8
SEQ = 4096
DEPTH = 4

SSM_GROUPS = 32
SSM_CH = 16
SSM_STATE = 64
SSM_WIDTH = SSM_GROUPS * SSM_CH
SSM_DT_MIN = 1e-3
SSM_DT_MAX = 1e-1
DN_HEADS = 6
DN_HEAD_DIM = 128
DN_WIDTH = DN_HEADS * DN_HEAD_DIM
DN_CONV = 4
DN_CHUNK = 64
ATTN_HEADS = 6
ATTN_HEAD_DIM = 128
ATTN_WIDTH = ATTN_HEADS * ATTN_HEAD_DIM
DILATED_PAIRS = ((128, 1), (512, 4), (2048, 16))
ATTN_BLOCK = 128
N_BUCKETS = 32
REL_MAX_DIST = 2048
D_MIX = SSM_WIDTH + DN_WIDTH + ATTN_WIDTH
IN_SPLITS = (SSM_WIDTH, ATTN_WIDTH, ATTN_WIDTH, ATTN_WIDTH, 3 * DN_WIDTH, DN_WIDTH, DN_HEADS, DN_HEADS)
N_IN_COLS = SSM_WIDTH + 3 * ATTN_WIDTH + 4 * DN_WIDTH + 2 * DN_HEADS
D_FF = 5632
NORM_EPS = 1e-6
NEG_INF = -1e30

kernel_name = 'hybrid_s5_gdn_dilated_macaron'


def rms_norm(x, gain):
    xf = x.astype(jnp.float32)
    y = xf * lax.rsqrt(jnp.mean(xf * xf, axis=-1, keepdims=True) + NORM_EPS)
    return (y * gain.astype(jnp.float32)).astype(x.dtype)


def l2_normalize(x):
    return x * lax.rsqrt(jnp.sum(x * x, axis=-1, keepdims=True) + NORM_EPS)


def swiglu(h, w_gate, w_up, w_down):
    return (jax.nn.silu(h @ w_gate) * (h @ w_up)) @ w_down


def causal_depthwise_conv(x, w):
    k_width, channels = w.shape
    return lax.conv_general_dilated(x, w[:, None, :], window_strides=(1,), padding=((k_width - 1, 0),),
                                    dimension_numbers=('NWC', 'WIO', 'NWC'), feature_group_count=channels)


def s5_layer(u, lam_re, lam_im, b_re, b_im, c_re, c_im, d_skip, log_dt, glu_w, glu_b):
    bsz, seq, _ = u.shape
    f32 = jnp.float32
    uf = u.astype(f32).reshape(bsz, seq, SSM_GROUPS, SSM_CH)
    lam = lax.complex(lam_re.astype(f32), lam_im.astype(f32))
    dt = jnp.exp(log_dt.astype(f32))[:, None]
    lam_bar = jnp.exp(lam * dt)
    b = lax.complex(b_re.astype(f32), b_im.astype(f32))
    b_bar = ((lam_bar - 1.0) / lam)[..., None] * b
    c = lax.complex(c_re.astype(f32), c_im.astype(f32))
    bu = jnp.einsum('gpc,bsgc->bsgp', b_bar, uf.astype(jnp.complex64))
    a = jnp.broadcast_to(lam_bar, bu.shape)

    def combine(e1, e2):
        a1, b1 = e1
        a2, b2 = e2
        return a1 * a2, a2 * b1 + b2

    _, states = lax.associative_scan(combine, (a, bu), axis=1)
    y = jnp.einsum('gcp,bsgp->bsgc', c, states).real + d_skip.astype(f32).reshape(SSM_GROUPS, SSM_CH) * uf
    y = jax.nn.gelu(y.reshape(bsz, seq, SSM_WIDTH))
    return y * jax.nn.sigmoid(y @ glu_w.astype(f32) + glu_b.astype(f32))


def to_chunks(t, chunk):
    bsz, seq, heads = t.shape[:3]
    t = t.reshape(bsz, seq // chunk, chunk, heads, *t.shape[3:])
    return jnp.moveaxis(t, 3, 1)


def gated_delta_rule(q, k, v, g, beta):
    bsz, seq, heads, dk = q.shape
    dv = v.shape[-1]
    q = to_chunks(q * dk ** -0.5, DN_CHUNK)
    k, v = to_chunks(k, DN_CHUNK), to_chunks(v, DN_CHUNK)
    g, beta = to_chunks(g, DN_CHUNK), to_chunks(beta, DN_CHUNK)
    gc = jnp.cumsum(g, axis=-1)
    idx = jnp.arange(DN_CHUNK)
    causal = idx[:, None] >= idx[None, :]
    strict = idx[:, None] > idx[None, :]
    decay = jnp.exp(jnp.where(causal, gc[..., :, None] - gc[..., None, :], NEG_INF))
    k_beta = k * beta[..., None]
    a_mat = jnp.where(strict, jnp.einsum('bhnck,bhnek->bhnce', k_beta, k) * decay, 0.0)
    eye = jnp.eye(DN_CHUNK, dtype=jnp.float32)
    t_inv = lax.linalg.triangular_solve(eye + a_mat, jnp.broadcast_to(eye, a_mat.shape),
                                        left_side=True, lower=True, unit_diagonal=True)
    u = jnp.einsum('bhnce,bhnev->bhncv', t_inv, v * beta[..., None])
    w = jnp.einsum('bhnce,bhnek->bhnck', t_inv, k_beta * jnp.exp(gc)[..., None])
    attn = jnp.einsum('bhnck,bhnek->bhnce', q, k) * decay
    q_dec = q * jnp.exp(gc)[..., None]
    k_tail = k * jnp.exp(gc[..., -1:] - gc)[..., None]
    chunk_decay = jnp.exp(gc[..., -1])
    xs = tuple(jnp.moveaxis(t, 2, 0) for t in (q_dec, k_tail, u, w, attn, chunk_decay))

    def step(state, inp):
        qd, kt, un, wn, an, dec = inp
        v_new = un - jnp.einsum('bhck,bhkv->bhcv', wn, state)
        o = jnp.einsum('bhck,bhkv->bhcv', qd, state) + jnp.einsum('bhce,bhev->bhcv', an, v_new)
        state = state * dec[..., None, None] + jnp.einsum('bhck,bhcv->bhkv', kt, v_new)
        return state, o

    state0 = jnp.zeros((bsz, heads, dk, dv), jnp.float32)
    _, o = lax.scan(step, state0, xs)
    o = jnp.moveaxis(o, 0, 2).reshape(bsz, heads, seq, dv)
    return jnp.swapaxes(o, 1, 2)


def t5_bucket(dist):
    max_exact = N_BUCKETS // 2
    d = jnp.maximum(dist, 1).astype(jnp.float32)
    large = max_exact + jnp.log(d / max_exact) / math.log(REL_MAX_DIST / max_exact) * (N_BUCKETS - max_exact)
    large = jnp.minimum(large.astype(jnp.int32), N_BUCKETS - 1)
    return jnp.where(dist < max_exact, dist, large)


def dilated_branch(q, k, v, rel_bias, window, dil):
    bsz, seq, heads, dh = q.shape
    blk = ATTN_BLOCK
    sub_len = seq // dil
    n_blocks = -(-sub_len // blk)
    sub_pad = n_blocks * blk

    def to_sub(t):
        t = t.astype(jnp.float32).reshape(bsz, sub_len, dil, heads, dh).transpose(0, 2, 1, 3, 4)
        return jnp.pad(t, ((0, 0), (0, 0), (0, sub_pad - sub_len), (0, 0), (0, 0)))

    def band(t):
        tp = jnp.pad(t, ((0, 0), (0, 0), (blk, 0), (0, 0), (0, 0)))
        prev = tp[:, :, :sub_pad].reshape(bsz, dil, n_blocks, blk, heads, dh)
        cur = tp[:, :, blk:].reshape(bsz, dil, n_blocks, blk, heads, dh)
        return jnp.concatenate([prev, cur], axis=3)

    qb = to_sub(q).reshape(bsz, dil, n_blocks, blk, heads, dh) * dh ** -0.5
    kb = band(to_sub(k))
    vb = band(to_sub(v))
    rel = blk + jnp.arange(blk)[:, None] - jnp.arange(2 * blk)[None, :]
    key_idx = jnp.arange(n_blocks)[:, None] * blk + jnp.arange(2 * blk)[None, :] - blk
    valid = ((rel >= 0) & (rel <= window // dil))[None] & (key_idx >= 0)[:, None, :]
    bias = jnp.moveaxis(rel_bias.astype(jnp.float32)[t5_bucket(jnp.maximum(rel, 0) * dil)], -1, 0)
    logits = jnp.einsum('bgnqhd,bgnkhd->bhgnqk', qb, kb) + bias[None, :, None, None]
    logits = jnp.where(valid[None, None, None], logits, NEG_INF)
    m = jnp.max(logits, axis=-1, keepdims=True)
    p = jnp.exp(logits - m)
    s = jnp.sum(p, axis=-1, keepdims=True)
    o = jnp.einsum('bhgnqk,bgnkhd->bgnqhd', p / s, vb)
    lse = (m + jnp.log(s))[..., 0].reshape(bsz, heads, dil, sub_pad)[..., :sub_len]
    o = o.reshape(bsz, dil, sub_pad, heads, dh)[:, :, :sub_len].transpose(0, 2, 1, 3, 4).reshape(bsz, seq, heads, dh)
    lse = lse.transpose(0, 3, 2, 1).reshape(bsz, seq, heads)
    return o, lse


def dilated_attention(q, k, v, rel_bias):
    outs, lses = [], []
    for window, dil in DILATED_PAIRS:
        o, lse = dilated_branch(q, k, v, rel_bias, window, dil)
        outs.append(o)
        lses.append(lse)
    weights = jax.nn.softmax(jnp.stack(lses, axis=0), axis=0)
    return jnp.einsum('gbsh,gbshd->bshd', weights, jnp.stack(outs, axis=0))


def hybrid_mixer(h, w_in, w_out, ssm_lambda_re, ssm_lambda_im, ssm_b_re, ssm_b_im, ssm_c_re, ssm_c_im,
                 ssm_d, ssm_log_dt, ssm_glu_w, ssm_glu_b, ssm_out_gain, dn_conv_w, dn_a_log, dn_dt_bias,
                 dn_norm_gain, attn_out_gain, rel_bias):
    bsz, seq, _ = h.shape
    f32 = jnp.float32
    proj = h @ w_in
    cuts = [sum(IN_SPLITS[:i + 1]) for i in range(len(IN_SPLITS) - 1)]
    u_ssm, a_q, a_k, a_v, dn_qkv, dn_z, dn_a, dn_b = jnp.split(proj, cuts, axis=-1)

    y_ssm = rms_norm(s5_layer(u_ssm, ssm_lambda_re, ssm_lambda_im, ssm_b_re, ssm_b_im, ssm_c_re, ssm_c_im,
                              ssm_d, ssm_log_dt, ssm_glu_w, ssm_glu_b), ssm_out_gain)

    qkv = jax.nn.silu(causal_depthwise_conv(dn_qkv, dn_conv_w)).astype(f32)
    d_q, d_k, d_v = [t.reshape(bsz, seq, DN_HEADS, DN_HEAD_DIM) for t in jnp.split(qkv, 3, axis=-1)]
    d_q, d_k = l2_normalize(d_q), l2_normalize(d_k)
    beta = jax.nn.sigmoid(dn_b.astype(f32))
    g = -jnp.exp(dn_a_log.astype(f32)) * jax.nn.softplus(dn_a.astype(f32) + dn_dt_bias.astype(f32))
    o_dn = gated_delta_rule(d_q, d_k, d_v, g, beta)
    o_dn = rms_norm(o_dn, dn_norm_gain) * jax.nn.silu(dn_z.astype(f32).reshape(bsz, seq, DN_HEADS, DN_HEAD_DIM))
    y_dn = o_dn.reshape(bsz, seq, DN_WIDTH)

    at_q, at_k, at_v = [t.reshape(bsz, seq, ATTN_HEADS, ATTN_HEAD_DIM) for t in (a_q, a_k, a_v)]
    o_at = dilated_attention(at_q, at_k, at_v, rel_bias).reshape(bsz, seq, ATTN_WIDTH)
    y_at = rms_norm(o_at, attn_out_gain)

    mix = jnp.concatenate([y_ssm.astype(h.dtype), y_dn.astype(h.dtype), y_at.astype(h.dtype)], axis=-1)
    return mix @ w_out


def _fwd_setup_inputs(seed: int = 0) -> dict:
    key = jax.random.key(seed)
    ks = jax.random.split(key, 24)
    f32 = jnp.float32

    def nrm(k, shape, scale):
        return scale * jax.random.normal(k, shape, f32)

    x = jax.random.normal(ks[0], (BATCH, SEQ, D_MODEL), f32)
    norm_gains = 1.0 + nrm(ks[1], (DEPTH, 6, D_MODEL), 0.05)
    ffn_w_gate = nrm(ks[2], (DEPTH, 2, D_MODEL, D_FF), D_MODEL ** -0.5)
    ffn_w_up = nrm(ks[3], (DEPTH, 2, D_MODEL, D_FF), D_MODEL ** -0.5)
    ffn_w_down = nrm(ks[4], (DEPTH, 2, D_FF, D_MODEL), D_FF ** -0.5)
    w_in = nrm(ks[5], (DEPTH, D_MODEL, N_IN_COLS), D_MODEL ** -0.5)
    w_out = nrm(ks[6], (DEPTH, D_MIX, D_MODEL), D_MIX ** -0.5)
    n_idx = jnp.arange(SSM_STATE, dtype=f32)
    ssm_lambda_re = -0.5 + nrm(ks[7], (DEPTH, SSM_GROUPS, SSM_STATE), 0.01)
    ssm_lambda_im = math.pi * n_idx + nrm(ks[8], (DEPTH, SSM_GROUPS, SSM_STATE), 0.01)
    ssm_b_re = nrm(ks[9], (DEPTH, SSM_GROUPS, SSM_STATE, SSM_CH), (2 * SSM_CH) ** -0.5)
    ssm_b_im = nrm(ks[10], (DEPTH, SSM_GROUPS, SSM_STATE, SSM_CH), (2 * SSM_CH) ** -0.5)
    ssm_c_re = nrm(ks[11], (DEPTH, SSM_GROUPS, SSM_CH, SSM_STATE), (2 * SSM_STATE) ** -0.5)
    ssm_c_im = nrm(ks[12], (DEPTH, SSM_GROUPS, SSM_CH, SSM_STATE), (2 * SSM_STATE) ** -0.5)
    ssm_d = nrm(ks[13], (DEPTH, SSM_WIDTH), 1.0)
    ssm_log_dt = jax.random.uniform(ks[14], (DEPTH, SSM_GROUPS), f32, math.log(SSM_DT_MIN), math.log(SSM_DT_MAX))
    ssm_glu_w = nrm(ks[15], (DEPTH, SSM_WIDTH, SSM_WIDTH), SSM_WIDTH ** -0.5)
    ssm_glu_b = nrm(ks[16], (DEPTH, SSM_WIDTH), 0.02)
    ssm_out_gain = 1.0 + nrm(ks[17], (DEPTH, SSM_WIDTH), 0.05)
    dn_conv_w = nrm(ks[18], (DEPTH, DN_CONV, 3 * DN_WIDTH), DN_CONV ** -0.5)
    dn_a_log = jnp.log(jax.random.uniform(ks[19], (DEPTH, DN_HEADS), f32, 1.0, 16.0))
    dt = jnp.exp(jax.random.uniform(ks[20], (DEPTH, DN_HEADS), f32, math.log(1e-3), math.log(1e-1)))
    dn_dt_bias = dt + jnp.log(-jnp.expm1(-dt))
    dn_norm_gain = 1.0 + nrm(ks[21], (DEPTH, DN_HEAD_DIM), 0.05)
    attn_out_gain = 1.0 + nrm(ks[22], (DEPTH, ATTN_WIDTH), 0.05)
    rel_bias = nrm(ks[23], (N_BUCKETS, ATTN_HEADS), 0.5)
    return {'x': x, 'norm_gains': norm_gains, 'ffn_w_gate': ffn_w_gate, 'ffn_w_up': ffn_w_up,
            'ffn_w_down': ffn_w_down, 'w_in': w_in, 'w_out': w_out,
            'ssm_lambda_re': ssm_lambda_re, 'ssm_lambda_im': ssm_lambda_im,
            'ssm_b_re': ssm_b_re, 'ssm_b_im': ssm_b_im, 'ssm_c_re': ssm_c_re, 'ssm_c_im': ssm_c_im,
            'ssm_d': ssm_d, 'ssm_log_dt': ssm_log_dt, 'ssm_glu_w': ssm_glu_w, 'ssm_glu_b': ssm_glu_b,
            'ssm_out_gain': ssm_out_gain, 'dn_conv_w': dn_conv_w, 'dn_a_log': dn_a_log,
            'dn_dt_bias': dn_dt_bias, 'dn_norm_gain': dn_norm_gain, 'attn_out_gain': attn_out_gain,
            'rel_bias': rel_bias}


def _fwd_reference(x, norm_gains, ffn_w_gate, ffn_w_up, ffn_w_down, w_in, w_out,
              ssm_lambda_re, ssm_lambda_im, ssm_b_re, ssm_b_im, ssm_c_re, ssm_c_im,
              ssm_d, ssm_log_dt, ssm_glu_w, ssm_glu_b, ssm_out_gain, dn_conv_w, dn_a_log,
              dn_dt_bias, dn_norm_gain, attn_out_gain, rel_bias):
    for l in range(DEPTH):
        gains = norm_gains[l]
        h = rms_norm(x, gains[0])
        x = x + 0.5 * rms_norm(swiglu(h, ffn_w_gate[l, 0], ffn_w_up[l, 0], ffn_w_down[l, 0]), gains[1])
        h = rms_norm(x, gains[2])
        mix = hybrid_mixer(h, w_in[l], w_out[l], ssm_lambda_re[l], ssm_lambda_im[l], ssm_b_re[l], ssm_b_im[l],
                           ssm_c_re[l], ssm_c_im[l], ssm_d[l], ssm_log_dt[l], ssm_glu_w[l], ssm_glu_b[l],
                           ssm_out_gain[l], dn_conv_w[l], dn_a_log[l], dn_dt_bias[l], dn_norm_gain[l],
                           attn_out_gain[l], rel_bias)
        x = x + rms_norm(mix, gains[3])
        h = rms_norm(x, gains[4])
        x = x + 0.5 * rms_norm(swiglu(h, ffn_w_gate[l, 1], ffn_w_up[l, 1], ffn_w_down[l, 1]), gains[5])
    return x


import jax as _jax
import jax.numpy as _jnp

TWIN_FORMAT = 'train_step'
FWD_PARAMS = ['x', 'norm_gains', 'ffn_w_gate', 'ffn_w_up', 'ffn_w_down', 'w_in', 'w_out', 'ssm_lambda_re', 'ssm_lambda_im', 'ssm_b_re', 'ssm_b_im', 'ssm_c_re', 'ssm_c_im', 'ssm_d', 'ssm_log_dt', 'ssm_glu_w', 'ssm_glu_b', 'ssm_out_gain', 'dn_conv_w', 'dn_a_log', 'dn_dt_bias', 'dn_norm_gain', 'attn_out_gain', 'rel_bias']
TWIN_WEIGHTS = ['norm_gains', 'ffn_w_gate', 'ffn_w_up', 'ffn_w_down', 'w_in', 'w_out', 'ssm_lambda_re', 'ssm_lambda_im', 'ssm_b_re', 'ssm_b_im', 'ssm_c_re', 'ssm_c_im', 'ssm_d', 'ssm_log_dt', 'ssm_glu_w', 'ssm_glu_b', 'ssm_out_gain', 'dn_conv_w', 'dn_a_log', 'dn_dt_bias', 'dn_norm_gain', 'attn_out_gain', 'rel_bias']
TWIN_DIFF_INPUT = 'x'
TWIN_INPUTS = ['x', 'norm_gains', 'ffn_w_gate', 'ffn_w_up', 'ffn_w_down', 'w_in', 'w_out', 'ssm_lambda_re', 'ssm_lambda_im', 'ssm_b_re', 'ssm_b_im', 'ssm_c_re', 'ssm_c_im', 'ssm_d', 'ssm_log_dt', 'ssm_glu_w', 'ssm_glu_b', 'ssm_out_gain', 'dn_conv_w', 'dn_a_log', 'dn_dt_bias', 'dn_norm_gain', 'attn_out_gain', 'rel_bias', 'loss_target', 'm_norm_gains', 'm_ffn_w_gate', 'm_ffn_w_up', 'm_ffn_w_down', 'm_w_in', 'm_w_out', 'm_ssm_lambda_re', 'm_ssm_lambda_im', 'm_ssm_b_re', 'm_ssm_b_im', 'm_ssm_c_re', 'm_ssm_c_im', 'm_ssm_d', 'm_ssm_log_dt', 'm_ssm_glu_w', 'm_ssm_glu_b', 'm_ssm_out_gain', 'm_dn_conv_w', 'm_dn_a_log', 'm_dn_dt_bias', 'm_dn_norm_gain', 'm_attn_out_gain', 'm_rel_bias', 'v_norm_gains', 'v_ffn_w_gate', 'v_ffn_w_up', 'v_ffn_w_down', 'v_w_in', 'v_w_out', 'v_ssm_lambda_re', 'v_ssm_lambda_im', 'v_ssm_b_re', 'v_ssm_b_im', 'v_ssm_c_re', 'v_ssm_c_im', 'v_ssm_d', 'v_ssm_log_dt', 'v_ssm_glu_w', 'v_ssm_glu_b', 'v_ssm_out_gain', 'v_dn_conv_w', 'v_dn_a_log', 'v_dn_dt_bias', 'v_dn_norm_gain', 'v_attn_out_gain', 'v_rel_bias']
TWIN_OUTPUTS = ['loss', 'grad_x', 'grad_norm_gains', 'grad_ffn_w_gate', 'grad_ffn_w_up', 'grad_ffn_w_down', 'grad_w_in', 'grad_w_out', 'grad_ssm_lambda_re', 'grad_ssm_lambda_im', 'grad_ssm_b_re', 'grad_ssm_b_im', 'grad_ssm_c_re', 'grad_ssm_c_im', 'grad_ssm_d', 'grad_ssm_log_dt', 'grad_ssm_glu_w', 'grad_ssm_glu_b', 'grad_ssm_out_gain', 'grad_dn_conv_w', 'grad_dn_a_log', 'grad_dn_dt_bias', 'grad_dn_norm_gain', 'grad_attn_out_gain', 'grad_rel_bias', 'delta_norm_gains', 'delta_ffn_w_gate', 'delta_ffn_w_up', 'delta_ffn_w_down', 'delta_w_in', 'delta_w_out', 'delta_ssm_lambda_re', 'delta_ssm_lambda_im', 'delta_ssm_b_re', 'delta_ssm_b_im', 'delta_ssm_c_re', 'delta_ssm_c_im', 'delta_ssm_d', 'delta_ssm_log_dt', 'delta_ssm_glu_w', 'delta_ssm_glu_b', 'delta_ssm_out_gain', 'delta_dn_conv_w', 'delta_dn_a_log', 'delta_dn_dt_bias', 'delta_dn_norm_gain', 'delta_attn_out_gain', 'delta_rel_bias', 'new_m_norm_gains', 'new_m_ffn_w_gate', 'new_m_ffn_w_up', 'new_m_ffn_w_down', 'new_m_w_in', 'new_m_w_out', 'new_m_ssm_lambda_re', 'new_m_ssm_lambda_im', 'new_m_ssm_b_re', 'new_m_ssm_b_im', 'new_m_ssm_c_re', 'new_m_ssm_c_im', 'new_m_ssm_d', 'new_m_ssm_log_dt', 'new_m_ssm_glu_w', 'new_m_ssm_glu_b', 'new_m_ssm_out_gain', 'new_m_dn_conv_w', 'new_m_dn_a_log', 'new_m_dn_dt_bias', 'new_m_dn_norm_gain', 'new_m_attn_out_gain', 'new_m_rel_bias', 'new_v_norm_gains', 'new_v_ffn_w_gate', 'new_v_ffn_w_up', 'new_v_ffn_w_down', 'new_v_w_in', 'new_v_w_out', 'new_v_ssm_lambda_re', 'new_v_ssm_lambda_im', 'new_v_ssm_b_re', 'new_v_ssm_b_im', 'new_v_ssm_c_re', 'new_v_ssm_c_im', 'new_v_ssm_d', 'new_v_ssm_log_dt', 'new_v_ssm_glu_w', 'new_v_ssm_glu_b', 'new_v_ssm_out_gain', 'new_v_dn_conv_w', 'new_v_dn_a_log', 'new_v_dn_dt_bias', 'new_v_dn_norm_gain', 'new_v_attn_out_gain', 'new_v_rel_bias']
TWIN_LEAF_KINDS = {'loss': 'loss', 'grad_x': 'grad_x', 'grad_norm_gains': 'grad_w', 'grad_ffn_w_gate': 'grad_w', 'grad_ffn_w_up': 'grad_w', 'grad_ffn_w_down': 'grad_w', 'grad_w_in': 'grad_w', 'grad_w_out': 'grad_w', 'grad_ssm_lambda_re': 'grad_w', 'grad_ssm_lambda_im': 'grad_w', 'grad_ssm_b_re': 'grad_w', 'grad_ssm_b_im': 'grad_w', 'grad_ssm_c_re': 'grad_w', 'grad_ssm_c_im': 'grad_w', 'grad_ssm_d': 'grad_w', 'grad_ssm_log_dt': 'grad_w', 'grad_ssm_glu_w': 'grad_w', 'grad_ssm_glu_b': 'grad_w', 'grad_ssm_out_gain': 'grad_w', 'grad_dn_conv_w': 'grad_w', 'grad_dn_a_log': 'grad_w', 'grad_dn_dt_bias': 'grad_w', 'grad_dn_norm_gain': 'grad_w', 'grad_attn_out_gain': 'grad_w', 'grad_rel_bias': 'grad_w', 'delta_norm_gains': 'delta_w', 'delta_ffn_w_gate': 'delta_w', 'delta_ffn_w_up': 'delta_w', 'delta_ffn_w_down': 'delta_w', 'delta_w_in': 'delta_w', 'delta_w_out': 'delta_w', 'delta_ssm_lambda_re': 'delta_w', 'delta_ssm_lambda_im': 'delta_w', 'delta_ssm_b_re': 'delta_w', 'delta_ssm_b_im': 'delta_w', 'delta_ssm_c_re': 'delta_w', 'delta_ssm_c_im': 'delta_w', 'delta_ssm_d': 'delta_w', 'delta_ssm_log_dt': 'delta_w', 'delta_ssm_glu_w': 'delta_w', 'delta_ssm_glu_b': 'delta_w', 'delta_ssm_out_gain': 'delta_w', 'delta_dn_conv_w': 'delta_w', 'delta_dn_a_log': 'delta_w', 'delta_dn_dt_bias': 'delta_w', 'delta_dn_norm_gain': 'delta_w', 'delta_attn_out_gain': 'delta_w', 'delta_rel_bias': 'delta_w', 'new_m_norm_gains': 'new_m', 'new_m_ffn_w_gate': 'new_m', 'new_m_ffn_w_up': 'new_m', 'new_m_ffn_w_down': 'new_m', 'new_m_w_in': 'new_m', 'new_m_w_out': 'new_m', 'new_m_ssm_lambda_re': 'new_m', 'new_m_ssm_lambda_im': 'new_m', 'new_m_ssm_b_re': 'new_m', 'new_m_ssm_b_im': 'new_m', 'new_m_ssm_c_re': 'new_m', 'new_m_ssm_c_im': 'new_m', 'new_m_ssm_d': 'new_m', 'new_m_ssm_log_dt': 'new_m', 'new_m_ssm_glu_w': 'new_m', 'new_m_ssm_glu_b': 'new_m', 'new_m_ssm_out_gain': 'new_m', 'new_m_dn_conv_w': 'new_m', 'new_m_dn_a_log': 'new_m', 'new_m_dn_dt_bias': 'new_m', 'new_m_dn_norm_gain': 'new_m', 'new_m_attn_out_gain': 'new_m', 'new_m_rel_bias': 'new_m', 'new_v_norm_gains': 'new_v', 'new_v_ffn_w_gate': 'new_v', 'new_v_ffn_w_up': 'new_v', 'new_v_ffn_w_down': 'new_v', 'new_v_w_in': 'new_v', 'new_v_w_out': 'new_v', 'new_v_ssm_lambda_re': 'new_v', 'new_v_ssm_lambda_im': 'new_v', 'new_v_ssm_b_re': 'new_v', 'new_v_ssm_b_im': 'new_v', 'new_v_ssm_c_re': 'new_v', 'new_v_ssm_c_im': 'new_v', 'new_v_ssm_d': 'new_v', 'new_v_ssm_log_dt': 'new_v', 'new_v_ssm_glu_w': 'new_v', 'new_v_ssm_glu_b': 'new_v', 'new_v_ssm_out_gain': 'new_v', 'new_v_dn_conv_w': 'new_v', 'new_v_dn_a_log': 'new_v', 'new_v_dn_dt_bias': 'new_v', 'new_v_dn_norm_gain': 'new_v', 'new_v_attn_out_gain': 'new_v', 'new_v_rel_bias': 'new_v'}


def _forward(args):
    return _fwd_reference(*[args[k] for k in FWD_PARAMS])


def _output_shape():
    def fwd():
        inp = _fwd_setup_inputs(0)
        return _fwd_reference(*[inp[k] for k in FWD_PARAMS])
    out = _jax.eval_shape(fwd)
    return out.shape, out.dtype

N_MICROBATCH = 1
ADAM_LR = 0.001
ADAM_B1 = 0.9
ADAM_B2 = 0.999
ADAM_EPS = 1e-08
ADAM_WD = 0.01
ADAM_STEP = 10
PER_EXAMPLE_BATCH_AXIS = {'x': 0, 'loss_target': 0}
SHARED_INPUTS = []
_WEIGHT_DTYPES = {'norm_gains': _jnp.float32, 'ffn_w_gate': _jnp.float32, 'ffn_w_up': _jnp.float32, 'ffn_w_down': _jnp.float32, 'w_in': _jnp.float32, 'w_out': _jnp.float32, 'ssm_lambda_re': _jnp.float32, 'ssm_lambda_im': _jnp.float32, 'ssm_b_re': _jnp.float32, 'ssm_b_im': _jnp.float32, 'ssm_c_re': _jnp.float32, 'ssm_c_im': _jnp.float32, 'ssm_d': _jnp.float32, 'ssm_log_dt': _jnp.float32, 'ssm_glu_w': _jnp.float32, 'ssm_glu_b': _jnp.float32, 'ssm_out_gain': _jnp.float32, 'dn_conv_w': _jnp.float32, 'dn_a_log': _jnp.float32, 'dn_dt_bias': _jnp.float32, 'dn_norm_gain': _jnp.float32, 'attn_out_gain': _jnp.float32, 'rel_bias': _jnp.float32}
MOMENT_SCALE = {'norm_gains': 9.377037e+00, 'ffn_w_gate': 8.485036e-01, 'ffn_w_up': 1.232397e+00, 'ffn_w_down': 2.017265e+00, 'w_in': 4.857075e+00, 'w_out': 1.047371e+01, 'ssm_lambda_re': 3.552213e-01, 'ssm_lambda_im': 2.373687e-01, 'ssm_b_re': 2.030195e-01, 'ssm_b_im': 2.189684e-01, 'ssm_c_re': 3.884331e-01, 'ssm_c_im': 3.818394e-01, 'ssm_d': 1.477553e+01, 'ssm_log_dt': 2.661577e+01, 'ssm_glu_w': 2.203955e+00, 'ssm_glu_b': 6.005740e+00, 'ssm_out_gain': 1.357902e+01, 'dn_conv_w': 1.885408e+00, 'dn_a_log': 5.721505e+00, 'dn_dt_bias': 5.491557e+00, 'dn_norm_gain': 9.162042e+00, 'attn_out_gain': 1.222308e+01, 'rel_bias': 2.709026e+00}


def _to_microbatches(a, axis):
    t = _jnp.moveaxis(a, axis, 0)
    t = t.reshape((N_MICROBATCH, t.shape[0] // N_MICROBATCH) + t.shape[1:])
    return _jnp.moveaxis(t, 1, axis + 1)


def setup_inputs(seed: int = 0) -> dict:
    inp = _fwd_setup_inputs(seed)
    key = _jax.random.fold_in(_jax.random.key(seed), 7919)
    shape, _ = _output_shape()
    out = dict(inp)
    out["loss_target"] = _jax.random.normal(_jax.random.fold_in(key, 0), shape, _jnp.float32)
    for i, name in enumerate(TWIN_WEIGHTS):
        w = inp[name].astype(_jnp.float32)
        if MOMENT_SCALE is None:
            s = _jnp.sqrt(_jnp.mean(_jnp.square(w)) + 1e-30)
        else:
            s = MOMENT_SCALE[name]
        km, kv = _jax.random.split(_jax.random.fold_in(key, i + 1))
        out[name] = w
        out["m_" + name] = s * _jax.random.normal(km, w.shape, _jnp.float32)
        out["v_" + name] = (s * s) * _jax.random.uniform(kv, w.shape, _jnp.float32, 0.5, 1.5)
    if N_MICROBATCH > 1:
        for name, axis in PER_EXAMPLE_BATCH_AXIS.items():
            out[name] = _to_microbatches(out[name], axis)
    return {'x': out['x'], 'norm_gains': out['norm_gains'], 'ffn_w_gate': out['ffn_w_gate'], 'ffn_w_up': out['ffn_w_up'], 'ffn_w_down': out['ffn_w_down'], 'w_in': out['w_in'], 'w_out': out['w_out'], 'ssm_lambda_re': out['ssm_lambda_re'], 'ssm_lambda_im': out['ssm_lambda_im'], 'ssm_b_re': out['ssm_b_re'], 'ssm_b_im': out['ssm_b_im'], 'ssm_c_re': out['ssm_c_re'], 'ssm_c_im': out['ssm_c_im'], 'ssm_d': out['ssm_d'], 'ssm_log_dt': out['ssm_log_dt'], 'ssm_glu_w': out['ssm_glu_w'], 'ssm_glu_b': out['ssm_glu_b'], 'ssm_out_gain': out['ssm_out_gain'], 'dn_conv_w': out['dn_conv_w'], 'dn_a_log': out['dn_a_log'], 'dn_dt_bias': out['dn_dt_bias'], 'dn_norm_gain': out['dn_norm_gain'], 'attn_out_gain': out['attn_out_gain'], 'rel_bias': out['rel_bias'], 'loss_target': out['loss_target'], 'm_norm_gains': out['m_norm_gains'], 'm_ffn_w_gate': out['m_ffn_w_gate'], 'm_ffn_w_up': out['m_ffn_w_up'], 'm_ffn_w_down': out['m_ffn_w_down'], 'm_w_in': out['m_w_in'], 'm_w_out': out['m_w_out'], 'm_ssm_lambda_re': out['m_ssm_lambda_re'], 'm_ssm_lambda_im': out['m_ssm_lambda_im'], 'm_ssm_b_re': out['m_ssm_b_re'], 'm_ssm_b_im': out['m_ssm_b_im'], 'm_ssm_c_re': out['m_ssm_c_re'], 'm_ssm_c_im': out['m_ssm_c_im'], 'm_ssm_d': out['m_ssm_d'], 'm_ssm_log_dt': out['m_ssm_log_dt'], 'm_ssm_glu_w': out['m_ssm_glu_w'], 'm_ssm_glu_b': out['m_ssm_glu_b'], 'm_ssm_out_gain': out['m_ssm_out_gain'], 'm_dn_conv_w': out['m_dn_conv_w'], 'm_dn_a_log': out['m_dn_a_log'], 'm_dn_dt_bias': out['m_dn_dt_bias'], 'm_dn_norm_gain': out['m_dn_norm_gain'], 'm_attn_out_gain': out['m_attn_out_gain'], 'm_rel_bias': out['m_rel_bias'], 'v_norm_gains': out['v_norm_gains'], 'v_ffn_w_gate': out['v_ffn_w_gate'], 'v_ffn_w_up': out['v_ffn_w_up'], 'v_ffn_w_down': out['v_ffn_w_down'], 'v_w_in': out['v_w_in'], 'v_w_out': out['v_w_out'], 'v_ssm_lambda_re': out['v_ssm_lambda_re'], 'v_ssm_lambda_im': out['v_ssm_lambda_im'], 'v_ssm_b_re': out['v_ssm_b_re'], 'v_ssm_b_im': out['v_ssm_b_im'], 'v_ssm_c_re': out['v_ssm_c_re'], 'v_ssm_c_im': out['v_ssm_c_im'], 'v_ssm_d': out['v_ssm_d'], 'v_ssm_log_dt': out['v_ssm_log_dt'], 'v_ssm_glu_w': out['v_ssm_glu_w'], 'v_ssm_glu_b': out['v_ssm_glu_b'], 'v_ssm_out_gain': out['v_ssm_out_gain'], 'v_dn_conv_w': out['v_dn_conv_w'], 'v_dn_a_log': out['v_dn_a_log'], 'v_dn_dt_bias': out['v_dn_dt_bias'], 'v_dn_norm_gain': out['v_dn_norm_gain'], 'v_attn_out_gain': out['v_attn_out_gain'], 'v_rel_bias': out['v_rel_bias']}


def _loss(weights, diff, rest, loss_target):
    with _jax.named_scope("forward"):
        args = {**rest, TWIN_DIFF_INPUT: diff, **{k: w.astype(_WEIGHT_DTYPES[k]) for k, w in weights.items()}}
        y = _forward(args)
    with _jax.named_scope("loss_head"):
        err = _jnp.square(y.astype(_jnp.float32) - loss_target)
        return 0.5 * _jnp.sum(_jnp.mean(err, axis=-1)) if err.ndim else 0.5 * err


def _adamw(w, g, m, v):
    m = ADAM_B1 * m + (1.0 - ADAM_B1) * g
    v = ADAM_B2 * v + (1.0 - ADAM_B2) * _jnp.square(g)
    m_hat = m / (1.0 - ADAM_B1 ** ADAM_STEP)
    v_hat = v / (1.0 - ADAM_B2 ** ADAM_STEP)
    delta = -ADAM_LR * (m_hat / (_jnp.sqrt(v_hat) + ADAM_EPS) + ADAM_WD * w)
    return delta, m, v


def reference(x, norm_gains, ffn_w_gate, ffn_w_up, ffn_w_down, w_in, w_out, ssm_lambda_re, ssm_lambda_im, ssm_b_re, ssm_b_im, ssm_c_re, ssm_c_im, ssm_d, ssm_log_dt, ssm_glu_w, ssm_glu_b, ssm_out_gain, dn_conv_w, dn_a_log, dn_dt_bias, dn_norm_gain, attn_out_gain, rel_bias, loss_target, m_norm_gains, m_ffn_w_gate, m_ffn_w_up, m_ffn_w_down, m_w_in, m_w_out, m_ssm_lambda_re, m_ssm_lambda_im, m_ssm_b_re, m_ssm_b_im, m_ssm_c_re, m_ssm_c_im, m_ssm_d, m_ssm_log_dt, m_ssm_glu_w, m_ssm_glu_b, m_ssm_out_gain, m_dn_conv_w, m_dn_a_log, m_dn_dt_bias, m_dn_norm_gain, m_attn_out_gain, m_rel_bias, v_norm_gains, v_ffn_w_gate, v_ffn_w_up, v_ffn_w_down, v_w_in, v_w_out, v_ssm_lambda_re, v_ssm_lambda_im, v_ssm_b_re, v_ssm_b_im, v_ssm_c_re, v_ssm_c_im, v_ssm_d, v_ssm_log_dt, v_ssm_glu_w, v_ssm_glu_b, v_ssm_out_gain, v_dn_conv_w, v_dn_a_log, v_dn_dt_bias, v_dn_norm_gain, v_attn_out_gain, v_rel_bias):
    given = dict(x=x, norm_gains=norm_gains, ffn_w_gate=ffn_w_gate, ffn_w_up=ffn_w_up, ffn_w_down=ffn_w_down, w_in=w_in, w_out=w_out, ssm_lambda_re=ssm_lambda_re, ssm_lambda_im=ssm_lambda_im, ssm_b_re=ssm_b_re, ssm_b_im=ssm_b_im, ssm_c_re=ssm_c_re, ssm_c_im=ssm_c_im, ssm_d=ssm_d, ssm_log_dt=ssm_log_dt, ssm_glu_w=ssm_glu_w, ssm_glu_b=ssm_glu_b, ssm_out_gain=ssm_out_gain, dn_conv_w=dn_conv_w, dn_a_log=dn_a_log, dn_dt_bias=dn_dt_bias, dn_norm_gain=dn_norm_gain, attn_out_gain=attn_out_gain, rel_bias=rel_bias, loss_target=loss_target, m_norm_gains=m_norm_gains, m_ffn_w_gate=m_ffn_w_gate, m_ffn_w_up=m_ffn_w_up, m_ffn_w_down=m_ffn_w_down, m_w_in=m_w_in, m_w_out=m_w_out, m_ssm_lambda_re=m_ssm_lambda_re, m_ssm_lambda_im=m_ssm_lambda_im, m_ssm_b_re=m_ssm_b_re, m_ssm_b_im=m_ssm_b_im, m_ssm_c_re=m_ssm_c_re, m_ssm_c_im=m_ssm_c_im, m_ssm_d=m_ssm_d, m_ssm_log_dt=m_ssm_log_dt, m_ssm_glu_w=m_ssm_glu_w, m_ssm_glu_b=m_ssm_glu_b, m_ssm_out_gain=m_ssm_out_gain, m_dn_conv_w=m_dn_conv_w, m_dn_a_log=m_dn_a_log, m_dn_dt_bias=m_dn_dt_bias, m_dn_norm_gain=m_dn_norm_gain, m_attn_out_gain=m_attn_out_gain, m_rel_bias=m_rel_bias, v_norm_gains=v_norm_gains, v_ffn_w_gate=v_ffn_w_gate, v_ffn_w_up=v_ffn_w_up, v_ffn_w_down=v_ffn_w_down, v_w_in=v_w_in, v_w_out=v_w_out, v_ssm_lambda_re=v_ssm_lambda_re, v_ssm_lambda_im=v_ssm_lambda_im, v_ssm_b_re=v_ssm_b_re, v_ssm_b_im=v_ssm_b_im, v_ssm_c_re=v_ssm_c_re, v_ssm_c_im=v_ssm_c_im, v_ssm_d=v_ssm_d, v_ssm_log_dt=v_ssm_log_dt, v_ssm_glu_w=v_ssm_glu_w, v_ssm_glu_b=v_ssm_glu_b, v_ssm_out_gain=v_ssm_out_gain, v_dn_conv_w=v_dn_conv_w, v_dn_a_log=v_dn_a_log, v_dn_dt_bias=v_dn_dt_bias, v_dn_norm_gain=v_dn_norm_gain, v_attn_out_gain=v_attn_out_gain, v_rel_bias=v_rel_bias)
    weights = {n: given[n] for n in TWIN_WEIGHTS}
    shared = {n: given[n] for n in SHARED_INPUTS}
    per_example = {n: given[n] for n in ['x']}
    grad_fn = _jax.value_and_grad(_loss, argnums=(0, 1))

    def one_microbatch(ex, loss_target):
        ex = dict(ex)
        diff = ex.pop(TWIN_DIFF_INPUT)
        return grad_fn(weights, diff, {**shared, **ex}, loss_target)

    if N_MICROBATCH == 1:
        loss, (grad_w, grad_x) = one_microbatch(per_example, given["loss_target"])
    else:
        def body(carry, xs):
            loss_sum, grad_sum = carry
            l_k, (gw_k, gx_k) = one_microbatch(xs[0], xs[1])
            with _jax.named_scope("update"):
                return (loss_sum + l_k, _jax.tree.map(_jnp.add, grad_sum, gw_k)), gx_k

        init = (_jnp.zeros((), _jnp.float32), _jax.tree.map(_jnp.zeros_like, weights))
        (loss, grad_w), grad_x = _jax.lax.scan(body, init, (per_example, given["loss_target"]))
    with _jax.named_scope("update"):
        delta_w, new_m, new_v = {}, {}, {}
        for n in TWIN_WEIGHTS:
            delta_w[n], new_m[n], new_v[n] = _adamw(weights[n], grad_w[n], given["m_" + n], given["v_" + n])
    return (loss, grad_x, *[grad_w[n] for n in TWIN_WEIGHTS], *[delta_w[n] for n in TWIN_WEIGHTS],
            *[new_m[n] for n in TWIN_WEIGHTS], *[new_v[n] for n in TWIN_WEIGHTS])
```

```python
import functools
import math

import jax
import jax.numpy as jnp
from jax import lax
from jax.experimental import pallas as pl
from jax.experimental.pallas import tpu as pltpu

F32 = jnp.float32
BF16 = jnp.bfloat16
HI = lax.Precision.HIGHEST
MESH = pl.DeviceIdType.MESH

NORM_EPS = 1e-6
NEG_INF = -1e30
SSM_GROUPS, SSM_CH, SSM_STATE, SSM_WIDTH = 32, 16, 64, 512
SSM_T = 16
DN_HEADS, DN_DIM, DN_WIDTH, DN_CONV, DN_CHUNK = 6, 128, 768, 4, 64
AT_HEADS, AT_DIM, AT_WIDTH, AT_BLK = 6, 128, 768, 128
DILATED = ((128, 1), (512, 4), (2048, 16))
N_BUCKETS, REL_MAX = 32, 2048
N_IN = SSM_WIDTH + 3 * AT_WIDTH + 4 * DN_WIDTH + 2 * DN_HEADS
N_IN_PAD = 6144
COL_AQ, COL_AK, COL_AV, COL_DQKV, COL_DZ, COL_DAB = 512, 1280, 2048, 2816, 5120, 5888
ADAM_LR, ADAM_B1, ADAM_B2, ADAM_EPS, ADAM_WD, ADAM_STEP = 0.001, 0.9, 0.999, 1e-08, 0.01, 10
V7X_VMEM_BYTES = 64 * 1024 * 1024
NN = (((1,), (0,)), ((), ()))
NT = (((1,), (1,)), ((), ()))
TN = (((0,), (0,)), ((), ()))
BNN = (((2,), (1,)), ((0,), (0,)))
BNT = (((2,), (2,)), ((0,), (0,)))
BTN = (((1,), (1,)), ((0,), (0,)))


def _params(sem, vmem_mb=None):
    kw = {}
    if vmem_mb is not None:
        kw["vmem_limit_bytes"] = min(vmem_mb * 1024 * 1024, V7X_VMEM_BYTES - 8 * 1024 * 1024)
    return pltpu.CompilerParams(dimension_semantics=sem, **kw)


def _dotf(a, b, dims=NN):
    return lax.dot_general(a, b, dims, precision=HI, preferred_element_type=F32)


def _dotb(a, b, dims=NN):
    return lax.dot_general(a.astype(BF16), b.astype(BF16), dims, preferred_element_type=F32)


def _sigmoid(x):
    return 1.0 / (1.0 + jnp.exp(-x))


def _silu(x):
    return x * _sigmoid(x)


def _softplus(x):
    return jnp.maximum(x, 0.0) + jnp.log(1.0 + jnp.exp(-jnp.abs(x)))


def _gelu(x):
    return 0.5 * x * (1.0 + jnp.tanh(math.sqrt(2.0 / math.pi) * (x + 0.044715 * x * x * x)))


def _rms(x, gain):
    return x * lax.rsqrt(jnp.mean(x * x, axis=-1, keepdims=True) + NORM_EPS) * gain


def _row_tile(s):
    for t in (512, 256, 128, 64, 32, 16, 8):
        if s % t == 0:
            return t
    return s


def _mm(name, pairs, *, grid, a_blk, a_map, b_blk, b_map, o_shape, o_blk, o_map, dims, out_dtype=F32, vmem_mb=48):
    n_red = grid[-1]
    n_pairs = len(pairs)
    acc_shape = tuple(d for d in o_blk if d is not None)

    def body(*refs):
        ins, o_ref, scr = refs[:2 * n_pairs], refs[2 * n_pairs], refs[2 * n_pairs + 1:]
        part = _dotb(ins[0][...], ins[1][...], dims)
        for p in range(1, n_pairs):
            part = part + _dotb(ins[2 * p][...], ins[2 * p + 1][...], dims)
        if n_red == 1:
            o_ref[...] = part.astype(o_ref.dtype)
        else:
            acc = scr[0]
            r = pl.program_id(len(grid) - 1)

            @pl.when(r == 0)
            def _():
                acc[...] = part

            @pl.when(r > 0)
            def _():
                acc[...] += part

            @pl.when(r == n_red - 1)
            def _():
                o_ref[...] = acc[...].astype(o_ref.dtype)

    in_specs, args = [], []
    for a, b in pairs:
        in_specs += [pl.BlockSpec(a_blk, a_map), pl.BlockSpec(b_blk, b_map)]
        args += [a, b]
    return pl.pallas_call(
        body, name=name, grid=grid, in_specs=in_specs, out_specs=pl.BlockSpec(o_blk, o_map),
        out_shape=jax.ShapeDtypeStruct(o_shape, out_dtype),
        scratch_shapes=[pltpu.VMEM(acc_shape, F32)] if n_red > 1 else [],
        compiler_params=_params(("parallel",) * (len(grid) - 1) + ("arbitrary",), vmem_mb),
    )(*args)


def _col_tile(n):
    for t in (1536, 1408, 1024, 768, 512, 384, 256, 128):
        if n % t == 0:
            return t
    return n


def _mm_cols(name, a, w, widx, out_dtype):
    s, k = a.shape
    j_n, nj = w.shape[0], w.shape[-1]
    tm, tn = _row_tile(s), _col_tile(nj)
    nt = nj // tn
    lead = (None,) * (1 + len(widx))
    return _mm(name, [(a, w)], grid=(j_n, nt, s // tm, 1),
               a_blk=(tm, k), a_map=lambda j, c, i, r: (i, 0),
               b_blk=lead + (k, tn), b_map=lambda j, c, i, r: (j, *widx, 0, c),
               o_shape=(s, j_n * nj), o_blk=(tm, tn), o_map=lambda j, c, i, r: (i, j * nt + c),
               dims=NN, out_dtype=out_dtype)


def _mm_rows(name, a, w, widx, out_dtype):
    s = a.shape[0]
    j_n, kj, n = w.shape[0], w.shape[-2], w.shape[-1]
    tm = _row_tile(s)
    lead = (None,) * (1 + len(widx))
    return _mm(name, [(a, w)], grid=(s // tm, j_n),
               a_blk=(tm, kj), a_map=lambda i, j: (i, j),
               b_blk=lead + (kj, n), b_map=lambda i, j: (j, *widx, 0, 0),
               o_shape=(s, n), o_blk=(tm, n), o_map=lambda i, j: (i, 0), dims=NN, out_dtype=out_dtype)


def _mm_rows_t(name, a, w, widx, out_dtype):
    s, n = a.shape
    j_n, kj = w.shape[0], w.shape[-2]
    tm = _row_tile(s)
    lead = (None,) * (1 + len(widx))
    return _mm(name, [(a, w)], grid=(j_n, s // tm, 1),
               a_blk=(tm, n), a_map=lambda j, i, r: (i, 0),
               b_blk=lead + (kj, n), b_map=lambda j, i, r: (j, *widx, 0, 0),
               o_shape=(s, j_n * kj), o_blk=(tm, kj), o_map=lambda j, i, r: (i, j), dims=NT, out_dtype=out_dtype)


def _mm_cols_t(name, pairs, widx, out_dtype):
    a0, w0 = pairs[0]
    s = a0.shape[0]
    j_n, k, nj = w0.shape[0], w0.shape[-2], w0.shape[-1]
    tm, tn = _row_tile(s), _col_tile(nj)
    nt = nj // tn
    lead = (None,) * (1 + len(widx))
    return _mm(name, pairs, grid=(s // tm, j_n * nt),
               a_blk=(tm, tn), a_map=lambda i, r: (i, r),
               b_blk=lead + (k, tn), b_map=lambda i, r: (lax.div(r, nt), *widx, 0, lax.rem(r, nt)),
               o_shape=(s, k), o_blk=(tm, k), o_map=lambda i, r: (i, 0), dims=NT, out_dtype=out_dtype)


def _mm_grad(name, a, b, a_cols, b_cols, out_dtype=BF16):
    s = a.shape[0]
    tm = _row_tile(s)
    if a_cols is not None:
        j_n, ka, nb = a_cols, a.shape[1] // a_cols, b.shape[1]
        tk, tn = ka, _col_tile(nb)
        a_map = lambda j, kb, c, i: (i, j)
        b_map = lambda j, kb, c, i: (i, c)
    else:
        j_n, ka, nb = b_cols, a.shape[1], b.shape[1] // b_cols
        tk, tn = min(ka, 1024), _col_tile(nb)
        nt_ = nb // tn
        a_map = lambda j, kb, c, i: (i, kb)
        b_map = lambda j, kb, c, i: (i, j * nt_ + c)
    return _mm(name, [(a, b)], grid=(j_n, ka // tk, nb // tn, s // tm),
               a_blk=(tm, tk), a_map=a_map, b_blk=(tm, tn), b_map=b_map,
               o_shape=(j_n, ka, nb), o_blk=(None, tk, tn), o_map=lambda j, kb, c, i: (j, kb, c),
               dims=TN, out_dtype=out_dtype)


def _rowwise(name, fn, rows, outs, *, bcast=(), accs=(), tm=None, vmem_mb=48):
    rows = [r if isinstance(r, tuple) else (r, r.shape[1], 0) for r in rows]
    s = rows[0][0].shape[0]
    tm = tm or min(_row_tile(s), 256)
    nr, nb, no = len(rows), len(bcast), len(outs)

    def body(*refs):
        o_refs, a_refs = refs[nr + nb:nr + nb + no], refs[nr + nb + no:]
        res = fn(*[r[...] for r in refs[:nr + nb]])
        if not isinstance(res, (tuple, list)):
            res = (res,)
        for o, v in zip(o_refs, res[:no]):
            o[...] = v.astype(o.dtype)
        if a_refs:
            i = pl.program_id(0)
            for a, v in zip(a_refs, res[no:]):
                @pl.when(i == 0)
                def _(a=a, v=v):
                    a[...] = v

                @pl.when(i > 0)
                def _(a=a, v=v):
                    a[...] += v

    in_specs = [pl.BlockSpec((tm, w), lambda i, c=c: (i, c)) for _, w, c in rows]
    in_specs += [pl.BlockSpec(b.shape, lambda i, nd=b.ndim: (0,) * nd) for b in bcast]
    out_specs = [pl.BlockSpec((tm, c), lambda i: (i, 0)) for c, _ in outs]
    out_specs += [pl.BlockSpec(sh, lambda i, nd=len(sh): (0,) * nd) for sh in accs]
    out_shape = [jax.ShapeDtypeStruct((s, c), dt) for c, dt in outs] + [jax.ShapeDtypeStruct(sh, F32) for sh in accs]
    res = pl.pallas_call(
        body, name=name, grid=(s // tm,), in_specs=in_specs, out_specs=out_specs, out_shape=out_shape,
        compiler_params=_params(("arbitrary",) if accs else ("parallel",), vmem_mb),
    )(*[r[0] for r in rows], *bcast)
    return res


def _rms_fwd(name, x, gain):
    return _rowwise(name, lambda xv, g: _rms(xv, g), [x], [(x.shape[1], BF16)], bcast=[gain])[0]


def _rms_residual(name, x, f, gain, scale):
    return _rowwise(name, lambda xv, fv, g: xv + scale * _rms(fv, g), [x, f], [(x.shape[1], F32)], bcast=[gain])[0]


def _rms_bwd_math(dy, x, gain):
    r = lax.rsqrt(jnp.mean(x * x, axis=-1, keepdims=True) + NORM_EPS)
    xh = x * r
    dxh = dy * gain
    dx = r * (dxh - xh * jnp.mean(dxh * xh, axis=-1, keepdims=True))
    return dx, jnp.sum(dy * xh, axis=0, keepdims=True)


def _rms_bwd(name, dy, x, gain, scale=1.0, residual=None, out_dtype=F32):
    d = x.shape[1]
    if residual is None:
        fn = lambda dyv, xv, g: _rms_bwd_math(scale * dyv.astype(F32), xv, g)
        rows = [dy, x]
    else:
        def fn(dyv, xv, rv, g):
            dx, dg = _rms_bwd_math(scale * dyv.astype(F32), xv, g)
            return dx + rv, dg
        rows = [dy, x, residual]
    return _rowwise(name, fn, rows, [(d, out_dtype)], bcast=[gain], accs=[(1, d)])


def _swiglu_fwd(name, a, b):
    return _rowwise(name, lambda av, bv: _silu(av.astype(F32)) * bv.astype(F32), [a, b], [(a.shape[1], BF16)])[0]


def _swiglu_bwd(name, ds, a, b):
    def fn(dsv, av, bv):
        dsv, av, bv = dsv.astype(F32), av.astype(F32), bv.astype(F32)
        sg = _sigmoid(av)
        return dsv * bv * sg * (1.0 + av * (1.0 - sg)), dsv * av * sg
    f = a.shape[1]
    return _rowwise(name, fn, [ds, a, b], [(f, BF16), (f, BF16)])


def _loss_and_grad(y, target):
    d = y.shape[1]

    def fn(yv, tv):
        e = yv - tv
        part = 0.5 * jnp.sum(jnp.mean(e * e, axis=-1, keepdims=True), axis=0, keepdims=True)
        return e * (1.0 / d), jnp.broadcast_to(part, (1, 128))
    dy, loss = _rowwise("loss_head", fn, [y, target], [(d, F32)], accs=[(1, 128)])
    return loss[0, 0], dy


def _s5_operators(lam_re, lam_im, b_re, b_im, c_re, c_im, d_skip, log_dt):
    t_n, ch, p_n = SSM_T, SSM_CH, SSM_STATE
    dt = jnp.exp(log_dt)[:, None]
    ld_re, ld_im = lam_re * dt, lam_im * dt
    k = jnp.arange(t_n + 1, dtype=F32)[None, :, None]
    mag = jnp.exp(ld_re[:, None, :] * k)
    pw_re, pw_im = mag * jnp.cos(ld_im[:, None, :] * k), mag * jnp.sin(ld_im[:, None, :] * k)
    lb_re, lb_im = pw_re[:, 1], pw_im[:, 1]
    den = lam_re * lam_re + lam_im * lam_im
    f_re = ((lb_re - 1.0) * lam_re + lb_im * lam_im) / den
    f_im = (lb_im * lam_re - (lb_re - 1.0) * lam_im) / den
    bb_re = f_re[..., None] * b_re - f_im[..., None] * b_im
    bb_im = f_re[..., None] * b_im + f_im[..., None] * b_re
    cp_re = c_re[:, None] * pw_re[:, :t_n, None, :] - c_im[:, None] * pw_im[:, :t_n, None, :]
    cp_im = c_re[:, None] * pw_im[:, :t_n, None, :] + c_im[:, None] * pw_re[:, :t_n, None, :]
    taps = (jnp.einsum("gtcp,gpd->gtcd", cp_re, bb_re, precision=HI)
            - jnp.einsum("gtcp,gpd->gtcd", cp_im, bb_im, precision=HI))
    jt = jnp.arange(t_n)
    lag = jt[None, :] - jt[:, None]
    m5 = jnp.where((lag >= 0)[None, :, :, None, None], taps[:, jnp.maximum(lag, 0)], 0.0)
    m_op = jnp.transpose(m5, (0, 1, 4, 2, 3)).reshape(SSM_GROUPS, t_n * ch, t_n * ch)
    m_op = m_op + jnp.eye(t_n * ch, dtype=F32)[None] * jnp.tile(d_skip.reshape(SSM_GROUPS, 1, ch), (1, t_n, 1)).reshape(
        SSM_GROUPS, 1, t_n * ch)
    rv_re, rv_im = pw_re[:, t_n - 1::-1][:, :t_n], pw_im[:, t_n - 1::-1][:, :t_n]
    bo_re = rv_re[:, :, None, :] * jnp.swapaxes(bb_re, 1, 2)[:, None] - rv_im[:, :, None, :] * jnp.swapaxes(bb_im, 1, 2)[:, None]
    bo_im = rv_re[:, :, None, :] * jnp.swapaxes(bb_im, 1, 2)[:, None] + rv_im[:, :, None, :] * jnp.swapaxes(bb_re, 1, 2)[:, None]
    b_op = jnp.concatenate([bo_re, bo_im], axis=-1).reshape(SSM_GROUPS, t_n * ch, 2 * p_n)
    q_re = c_re[:, None] * pw_re[:, 1:, None, :] - c_im[:, None] * pw_im[:, 1:, None, :]
    q_im = c_re[:, None] * pw_im[:, 1:, None, :] + c_im[:, None] * pw_re[:, 1:, None, :]
    c_op = jnp.concatenate([q_re, -q_im], axis=-1).reshape(SSM_GROUPS, t_n * ch, 2 * p_n)
    c_op = jnp.swapaxes(c_op, 1, 2)
    a1 = jnp.concatenate([pw_re[:, t_n], pw_re[:, t_n]], axis=-1)[:, None, :]
    a2 = jnp.concatenate([-pw_im[:, t_n], pw_im[:, t_n]], axis=-1)[:, None, :]
    return m_op, b_op, c_op, a1, a2


def _s5_scan_fwd(u_g, m_op, b_op, c_op, a1, a2):
    g_n, nc, w = u_g.shape
    p2 = 2 * SSM_STATE

    def body(u_ref, m_ref, b_ref, c_ref, a1_ref, a2_ref, y_ref, s_ref, z_ref):
        u = u_ref[...]
        z_ref[...] = _dotf(u, b_ref[...])
        a1v, a2v = a1_ref[...], a2_ref[...]

        def step(n, s):
            s_ref[pl.ds(n, 1), :] = s
            return a1v * s + a2v * pltpu.roll(s, SSM_STATE, 1) + z_ref[pl.ds(n, 1), :]

        lax.fori_loop(0, nc, step, jnp.zeros((1, p2), F32))
        y_ref[...] = _dotf(u, m_ref[...]) + _dotf(s_ref[...], c_ref[...])

    blk = lambda a, b: pl.BlockSpec((None, a, b), lambda g: (g, 0, 0))
    return pl.pallas_call(
        body, name="s5_scan_fwd", grid=(g_n,),
        in_specs=[blk(nc, w), blk(w, w), blk(w, p2), blk(p2, w), blk(1, p2), blk(1, p2)],
        out_specs=[blk(nc, w), blk(nc, p2)],
        out_shape=[jax.ShapeDtypeStruct((g_n, nc, w), F32), jax.ShapeDtypeStruct((g_n, nc, p2), F32)],
        scratch_shapes=[pltpu.VMEM((nc, p2), F32)],
        compiler_params=_params(("parallel",)),
    )(u_g, m_op, b_op, c_op, a1, a2)


def _s5_scan_bwd(dy_g, u_g, s_g, m_op, b_op, c_op, a1, a2):
    g_n, nc, w = u_g.shape
    p2 = 2 * SSM_STATE

    def body(dy_ref, u_ref, s_ref, m_ref, b_ref, c_ref, a1_ref, a2_ref,
             du_ref, dm_ref, db_ref, dc_ref, da1_ref, da2_ref, ds_ref, dz_ref):
        dy, u = dy_ref[...], u_ref[...]
        ds_ref[...] = _dotf(dy, c_ref[...], NT)
        a1v, a2v = a1_ref[...], a2_ref[...]

        def step(k, carry):
            g, d1, d2 = carry
            n = nc - 1 - k
            dz_ref[pl.ds(n, 1), :] = g
            sn = s_ref[pl.ds(n, 1), :]
            d1 = d1 + g * sn
            d2 = d2 + g * pltpu.roll(sn, SSM_STATE, 1)
            g = ds_ref[pl.ds(n, 1), :] + a1v * g + pltpu.roll(a2v * g, SSM_STATE, 1)
            return g, d1, d2

        zero = jnp.zeros((1, p2), F32)
        _, d1, d2 = lax.fori_loop(0, nc, step, (zero, zero, zero))
        da1_ref[...] = d1
        da2_ref[...] = d2
        dz = dz_ref[...]
        du_ref[...] = _dotf(dy, m_ref[...], NT) + _dotf(dz, b_ref[...], NT)
        dm_ref[...] = _dotf(u, dy, TN)
        db_ref[...] = _dotf(u, dz, TN)
        dc_ref[...] = _dotf(s_ref[...], dy, TN)

    blk = lambda a, b: pl.BlockSpec((None, a, b), lambda g: (g, 0, 0))
    sds = lambda a, b: jax.ShapeDtypeStruct((g_n, a, b), F32)
    return pl.pallas_call(
        body, name="s5_scan_bwd", grid=(g_n,),
        in_specs=[blk(nc, w), blk(nc, w), blk(nc, p2), blk(w, w), blk(w, p2), blk(p2, w), blk(1, p2), blk(1, p2)],
        out_specs=[blk(nc, w), blk(w, w), blk(w, p2), blk(p2, w), blk(1, p2), blk(1, p2)],
        out_shape=[sds(nc, w), sds(w, w), sds(w, p2), sds(p2, w), sds(1, p2), sds(1, p2)],
        scratch_shapes=[pltpu.VMEM((nc, p2), F32), pltpu.VMEM((nc, p2), F32)],
        compiler_params=_params(("parallel",)),
    )(dy_g, u_g, s_g, m_op, b_op, c_op, a1, a2)


def _to_groups(u):
    s = u.shape[0]
    return u.reshape(s // SSM_T, SSM_T, SSM_GROUPS, SSM_CH).transpose(2, 0, 1, 3).reshape(
        SSM_GROUPS, s // SSM_T, SSM_T * SSM_CH)


def _from_groups(y_g):
    nc = y_g.shape[1]
    return y_g.reshape(SSM_GROUPS, nc, SSM_T, SSM_CH).transpose(1, 2, 0, 3).reshape(nc * SSM_T, SSM_WIDTH)


def _s5_post_math(y, glu_w, glu_b, gain):
    y2 = _gelu(y)
    o = y2 * _sigmoid(_dotb(y2, glu_w) + glu_b)
    return _rms(o, gain)


def _s5_post_fwd(y, glu_w, glu_b, gain):
    return _rowwise("s5_post_fwd", _s5_post_math, [y], [(SSM_WIDTH, F32)], bcast=[glu_w, glu_b, gain])[0]


def _s5_post_bwd(dout, y, glu_w, glu_b, gain):
    def fn(dv, yv, w, b, g):
        _, vjp = jax.vjp(_s5_post_math, yv, w, b, g)
        return vjp(dv)
    return _rowwise("s5_post_bwd", fn, [dout, y], [(SSM_WIDTH, F32)], bcast=[glu_w, glu_b, gain],
                    accs=[(SSM_WIDTH, SSM_WIDTH), (1, SSM_WIDTH), (1, SSM_WIDTH)])


def _t5_bucket(dist):
    max_exact = N_BUCKETS // 2
    d = jnp.maximum(dist, 1).astype(F32)
    large = max_exact + jnp.log(d / max_exact) / math.log(REL_MAX / max_exact) * (N_BUCKETS - max_exact)
    large = jnp.minimum(large.astype(jnp.int32), N_BUCKETS - 1)
    return jnp.where(dist < max_exact, dist, large)


def _attn_bias_tables(rel_bias):
    blk = AT_BLK
    tabs = []
    for window, dil in DILATED:
        rel = blk + jnp.arange(blk)[:, None] - jnp.arange(2 * blk)[None, :]
        valid = (rel >= 0) & (rel <= window // dil)
        bias = jnp.moveaxis(rel_bias[_t5_bucket(jnp.maximum(rel, 0) * dil)], -1, 0)
        tabs.append(jnp.where(valid[None], bias, NEG_INF))
    return jnp.stack(tabs)


def _attn_branch_fwd(b_idx, dil, q2, k2, v2, bias):
    sub, blk, dh = q2.shape[0], AT_BLK, AT_DIM
    nb = sub // blk
    scale = dh ** -0.5

    def body(q_ref, k_ref, v_ref, b_ref, o_ref, l_ref):
        def block(n, carry):
            cur = pl.multiple_of(n * blk, blk)
            prv = pl.multiple_of(jnp.maximum(n - 1, 0) * blk, blk)
            q = q_ref[pl.ds(cur, blk), :] * scale
            lc = _dotb(q, k_ref[pl.ds(cur, blk), :], NT) + b_ref[:, blk:]
            lp = _dotb(q, k_ref[pl.ds(prv, blk), :], NT) + b_ref[:, :blk]
            lp = jnp.where(n > 0, lp, NEG_INF)
            m = jnp.maximum(jnp.max(lc, axis=1, keepdims=True), jnp.max(lp, axis=1, keepdims=True))
            pc, pp = jnp.exp(lc - m), jnp.exp(lp - m)
            den = jnp.sum(pc, axis=1, keepdims=True) + jnp.sum(pp, axis=1, keepdims=True)
            inv = 1.0 / den
            o = _dotb(pc * inv, v_ref[pl.ds(cur, blk), :]) + _dotb(pp * inv, v_ref[pl.ds(prv, blk), :])
            o_ref[pl.ds(cur, blk), :] = o
            l_ref[pl.ds(cur, blk), :] = jnp.broadcast_to(m + jnp.log(den), (blk, dh))
            return carry

        lax.fori_loop(0, nb, block, 0)

    seq = pl.BlockSpec((sub, dh), lambda h, r: (0, r * AT_HEADS + h))
    return pl.pallas_call(
        body, name=f"attn_fwd_d{dil}", grid=(AT_HEADS, dil),
        in_specs=[seq, seq, seq, pl.BlockSpec((None, None, blk, 2 * blk), lambda h, r: (b_idx, h, 0, 0))],
        out_specs=[seq, seq],
        out_shape=[jax.ShapeDtypeStruct(q2.shape, F32), jax.ShapeDtypeStruct(q2.shape, F32)],
        compiler_params=_params(("parallel", "parallel")),
    )(q2, k2, v2, bias)


def _attn_branch_bwd(b_idx, dil, q2, k2, v2, do2, lse2, dlt2, bias):
    sub, blk, dh = q2.shape[0], AT_BLK, AT_DIM
    nb = sub // blk
    scale = dh ** -0.5

    def body(q_ref, k_ref, v_ref, do_ref, l_ref, d_ref, b_ref, dq_ref, dk_ref, dv_ref, db_ref):
        r = pl.program_id(1)
        dk_ref[...] = jnp.zeros_like(dk_ref)
        dv_ref[...] = jnp.zeros_like(dv_ref)

        @pl.when(r == 0)
        def _():
            db_ref[...] = jnp.zeros_like(db_ref)

        def block(n, carry):
            cur = pl.multiple_of(n * blk, blk)
            prv = pl.multiple_of(jnp.maximum(n - 1, 0) * blk, blk)
            q = q_ref[pl.ds(cur, blk), :] * scale
            do = do_ref[pl.ds(cur, blk), :]
            lse = l_ref[pl.ds(cur, blk), :][:, :1]
            dlt = d_ref[pl.ds(cur, blk), :][:, :1]
            kc, kp = k_ref[pl.ds(cur, blk), :], k_ref[pl.ds(prv, blk), :]
            vc, vp = v_ref[pl.ds(cur, blk), :], v_ref[pl.ds(prv, blk), :]
            lc = _dotb(q, kc, NT) + b_ref[:, blk:]
            lp = jnp.where(n > 0, _dotb(q, kp, NT) + b_ref[:, :blk], NEG_INF)
            pc, pp = jnp.exp(lc - lse), jnp.exp(lp - lse)
            dsc = pc * (_dotb(do, vc, NT) - dlt)
            dsp = pp * (_dotb(do, vp, NT) - dlt)
            dq_ref[pl.ds(cur, blk), :] = (_dotb(dsc, kc) + _dotb(dsp, kp)) * scale
            dk_ref[pl.ds(cur, blk), :] += _dotb(dsc, q, TN)
            dk_ref[pl.ds(prv, blk), :] += _dotb(dsp, q, TN)
            dv_ref[pl.ds(cur, blk), :] += _dotb(pc, do, TN)
            dv_ref[pl.ds(prv, blk), :] += _dotb(pp, do, TN)
            db_ref[:, blk:] += dsc
            db_ref[:, :blk] += dsp
            return carry

        lax.fori_loop(0, nb, block, 0)

    seq = pl.BlockSpec((sub, dh), lambda h, r: (0, r * AT_HEADS + h))
    tab = pl.BlockSpec((None, None, blk, 2 * blk), lambda h, r: (b_idx, h, 0, 0))
    return pl.pallas_call(
        body, name=f"attn_bwd_d{dil}", grid=(AT_HEADS, dil),
        in_specs=[seq, seq, seq, seq, seq, seq, tab],
        out_specs=[seq, seq, seq, pl.BlockSpec((None, blk, 2 * blk), lambda h, r: (h, 0, 0))],
        out_shape=[jax.ShapeDtypeStruct(q2.shape, F32)] * 3 + [jax.ShapeDtypeStruct((AT_HEADS, blk, 2 * blk), F32)],
        compiler_params=_params(("parallel", "arbitrary"), 56),
    )(q2, k2, v2, do2, lse2, dlt2, bias)


def _per_head(fn, *xs):
    return jnp.concatenate([fn(*[x[:, h * AT_DIM:(h + 1) * AT_DIM] for x in xs]) for h in range(AT_HEADS)], axis=1)


def _attn_merge_math(o1, o2, o3, l1, l2, l3, gain):
    m = jnp.maximum(jnp.maximum(l1, l2), l3)
    e1, e2, e3 = jnp.exp(l1 - m), jnp.exp(l2 - m), jnp.exp(l3 - m)
    den = e1 + e2 + e3
    o = (e1 * o1 + e2 * o2 + e3 * o3) / den
    return _rms(o, gain), o, m + jnp.log(den)


def _attn_merge_fwd(os_, ls_, gain):
    w = AT_WIDTH
    return _rowwise("attn_merge_fwd", _attn_merge_math, [*os_, *ls_], [(w, F32)] * 3, bcast=[gain])


def _attn_merge_bwd(dy, o, gain):
    def fn(dyv, ov, g):
        do, dg = _rms_bwd_math(dyv, ov, g)
        dlt = _per_head(lambda a, b: jnp.broadcast_to(jnp.sum(a * b, axis=1, keepdims=True), a.shape), do, ov)
        return do, dlt, dg
    return _rowwise("attn_merge_bwd", fn, [dy, o], [(AT_WIDTH, F32)] * 2, bcast=[gain], accs=[(1, AT_WIDTH)])


def _add3(name, a, b, c):
    return _rowwise(name, lambda x, y, z: x + y + z, [a, b, c], [(a.shape[1], F32)])[0]


def _conv_taps(x, w):
    row = lax.broadcasted_iota(jnp.int32, x.shape, 0)
    y = w[DN_CONV - 1:DN_CONV, :] * x
    for sh in range(1, DN_CONV):
        y = y + w[DN_CONV - 1 - sh:DN_CONV - sh, :] * jnp.where(row >= sh, pltpu.roll(x, sh, 0), 0.0)
    return y


def _gdn_prep_fwd(proj, conv_w):
    s = proj.shape[0]
    ncb = 3 * DN_HEADS
    c0 = COL_DQKV // 128

    def body(x_ref, w_ref, o_ref):
        o_ref[...] = _silu(_conv_taps(x_ref[...], w_ref[...]))

    return pl.pallas_call(
        body, name="gdn_prep_fwd", grid=(ncb,),
        in_specs=[pl.BlockSpec((s, 128), lambda c: (0, c0 + c)), pl.BlockSpec((DN_CONV, 128), lambda c: (0, c))],
        out_specs=pl.BlockSpec((None, s, 128), lambda c: (c, 0, 0)),
        out_shape=jax.ShapeDtypeStruct((ncb, s, 128), F32),
        compiler_params=_params(("parallel",)),
    )(proj, conv_w)


def _gdn_prep_bwd(dact, proj, conv_w):
    s = proj.shape[0]
    ncb = 3 * DN_HEADS
    c0 = COL_DQKV // 128

    def body(d_ref, x_ref, w_ref, dx_ref, dw_ref):
        x, w = x_ref[...], w_ref[...]
        pre = _conv_taps(x, w)
        sg = _sigmoid(pre)
        dpre = d_ref[...] * sg * (1.0 + pre * (1.0 - sg))
        row = lax.broadcasted_iota(jnp.int32, x.shape, 0)
        dx = w[DN_CONV - 1:DN_CONV, :] * dpre
        dw_ref[pl.ds(DN_CONV - 1, 1), :] = jnp.sum(dpre * x, axis=0, keepdims=True)
        for sh in range(1, DN_CONV):
            dx = dx + w[DN_CONV - 1 - sh:DN_CONV - sh, :] * jnp.where(row < s - sh, pltpu.roll(dpre, s - sh, 0), 0.0)
            dw_ref[pl.ds(DN_CONV - 1 - sh, 1), :] = jnp.sum(
                dpre * jnp.where(row >= sh, pltpu.roll(x, sh, 0), 0.0), axis=0, keepdims=True)
        dx_ref[...] = dx

    return pl.pallas_call(
        body, name="gdn_prep_bwd", grid=(ncb,),
        in_specs=[pl.BlockSpec((None, s, 128), lambda c: (c, 0, 0)), pl.BlockSpec((s, 128), lambda c: (0, c0 + c)),
                  pl.BlockSpec((DN_CONV, 128), lambda c: (0, c))],
        out_specs=[pl.BlockSpec((s, 128), lambda c: (0, c)), pl.BlockSpec((DN_CONV, 128), lambda c: (0, c))],
        out_shape=[jax.ShapeDtypeStruct((s, ncb * 128), F32), jax.ShapeDtypeStruct((DN_CONV, ncb * 128), F32)],
        compiler_params=_params(("parallel",)),
    )(dact, proj, conv_w)


def _gates_math(ab, alog, dtb):
    lane = lax.broadcasted_iota(jnp.int32, ab.shape, 1)
    g = -jnp.exp(alog) * _softplus(ab + dtb)
    return jnp.where(lane < DN_HEADS, g, jnp.where(lane < 2 * DN_HEADS, _sigmoid(ab), 0.0))


def _gates_fwd(proj, alog, dtb):
    return _rowwise("gdn_gates_fwd", _gates_math, [(proj, 128, COL_DAB // 128)], [(128, F32)], bcast=[alog, dtb])[0]


def _gates_bwd(dgates, proj, alog, dtb):
    def fn(dv, ab, a, d):
        _, vjp = jax.vjp(_gates_math, ab, a, d)
        return vjp(dv)
    return _rowwise("gdn_gates_bwd", fn, [dgates, (proj, 128, COL_DAB // 128)], [(128, F32)], bcast=[alog, dtb],
                    accs=[(1, 128), (1, 128)])


def _l2n(x):
    return x * lax.rsqrt(jnp.sum(x * x, axis=-1, keepdims=True) + NORM_EPS)


def _gdn_intra_math(q, k, v, gcol, grow, bcol):
    c = DN_CHUNK
    ii = lax.broadcasted_iota(jnp.int32, (1, c, c), 1)
    jj = lax.broadcasted_iota(jnp.int32, (1, c, c), 2)
    gc_col = jnp.sum(jnp.where(ii >= jj, grow, 0.0), axis=2, keepdims=True)
    gc_row = jnp.sum(jnp.where(ii <= jj, gcol, 0.0), axis=1, keepdims=True)
    gc_last = jnp.sum(gcol, axis=1, keepdims=True)
    decay = jnp.exp(jnp.where(ii >= jj, gc_col - gc_row, NEG_INF))
    qn = _l2n(q) * (DN_DIM ** -0.5)
    kn = _l2n(k)
    kb = kn * bcol
    a_mat = jnp.where(ii > jj, _dotf(kb, kn, BNT) * decay, 0.0)
    nil = -a_mat
    t_inv = jnp.where(ii == jj, 1.0, 0.0) + nil
    for _ in range(5):
        nil = _dotf(nil, nil, BNN)
        t_inv = t_inv + _dotf(t_inv, nil, BNN)
    e_col = jnp.exp(gc_col)
    u = _dotf(t_inv, v * bcol, BNN)
    w = _dotf(t_inv, kb * e_col, BNN)
    attn = _dotf(qn, kn, BNT) * decay
    return u, w, attn, qn * e_col, kn * jnp.exp(gc_last - gc_col), jnp.broadcast_to(jnp.exp(gc_last), (DN_HEADS, 1, 128))


def _gdn_specs(s):
    nc = s // DN_CHUNK
    h, c, d = DN_HEADS, DN_CHUNK, DN_DIM
    return dict(
        qkv=pl.BlockSpec((3 * h, c, d), lambda n: (0, n, 0)),
        hcd=pl.BlockSpec((h, c, d), lambda n: (0, n, 0)),
        col=pl.BlockSpec((h, c, 1), lambda n: (0, n, 0)),
        row=pl.BlockSpec((h, None, 1, c), lambda n: (0, n, 0, 0)),
        att=pl.BlockSpec((h, c, c), lambda n: (0, n, 0)),
        dec=pl.BlockSpec((h, None, 1, 128), lambda n: (0, n, 0, 0)),
        s_hcd=jax.ShapeDtypeStruct((h, s, d), F32), s_col=jax.ShapeDtypeStruct((h, s, 1), F32),
        s_row=jax.ShapeDtypeStruct((h, nc, 1, c), F32), s_att=jax.ShapeDtypeStruct((h, s, c), F32),
        s_dec=jax.ShapeDtypeStruct((h, nc, 1, 128), F32), s_qkv=jax.ShapeDtypeStruct((3 * h, s, d), F32),
    )


def _gdn_intra_fwd(act, gcol, grow, bcol):
    s = act.shape[1]
    sp = _gdn_specs(s)
    h = DN_HEADS

    def body(a_ref, gc_ref, gr_ref, bc_ref, u_ref, w_ref, at_ref, qd_ref, kt_ref, dec_ref):
        outs = _gdn_intra_math(a_ref[0:h], a_ref[h:2 * h], a_ref[2 * h:3 * h], gc_ref[...], gr_ref[...], bc_ref[...])
        for ref, val in zip((u_ref, w_ref, at_ref, qd_ref, kt_ref, dec_ref), outs):
            ref[...] = val

    return pl.pallas_call(
        body, name="gdn_intra_fwd", grid=(s // DN_CHUNK,),
        in_specs=[sp["qkv"], sp["col"], sp["row"], sp["col"]],
        out_specs=[sp["hcd"], sp["hcd"], sp["att"], sp["hcd"], sp["hcd"], sp["dec"]],
        out_shape=[sp["s_hcd"], sp["s_hcd"], sp["s_att"], sp["s_hcd"], sp["s_hcd"], sp["s_dec"]],
        compiler_params=_params(("parallel",)),
    )(act, gcol, grow, bcol)


def _gdn_intra_bwd(act, gcol, grow, bcol, du, dw, dattn, dqd, dkt, ddec):
    s = act.shape[1]
    sp = _gdn_specs(s)
    h = DN_HEADS

    def body(a_ref, gc_ref, gr_ref, bc_ref, du_ref, dw_ref, dat_ref, dqd_ref, dkt_ref, dde_ref,
             dact_ref, dgc_ref, dgr_ref, dbc_ref):
        _, vjp = jax.vjp(_gdn_intra_math, a_ref[0:h], a_ref[h:2 * h], a_ref[2 * h:3 * h],
                         gc_ref[...], gr_ref[...], bc_ref[...])
        dq, dk, dv, dgc, dgr, dbc = vjp((du_ref[...], dw_ref[...], dat_ref[...], dqd_ref[...], dkt_ref[...], dde_ref[...]))
        dact_ref[0:h] = dq
        dact_ref[h:2 * h] = dk
        dact_ref[2 * h:3 * h] = dv
        dgc_ref[...] = dgc
        dgr_ref[...] = dgr
        dbc_ref[...] = dbc

    return pl.pallas_call(
        body, name="gdn_intra_bwd", grid=(s // DN_CHUNK,),
        in_specs=[sp["qkv"], sp["col"], sp["row"], sp["col"], sp["hcd"], sp["hcd"], sp["att"], sp["hcd"], sp["hcd"], sp["dec"]],
        out_specs=[sp["qkv"], sp["col"], sp["row"], sp["col"]],
        out_shape=[sp["s_qkv"], sp["s_col"], sp["s_row"], sp["s_col"]],
        compiler_params=_params(("parallel",)),
    )(act, gcol, grow, bcol, du, dw, dattn, dqd, dkt, ddec)


def _gdn_step_math(state, u, w, attn, qd, kt, dec):
    v_new = u - _dotf(w, state, BNN)
    o = _dotf(qd, state, BNN) + _dotf(attn, v_new, BNN)
    return state * dec[:, :, :1] + _dotf(kt, v_new, BTN), o


def _gdn_scan_fwd(u, w, attn, qd, kt, dec):
    s = u.shape[1]
    nc = s // DN_CHUNK
    sp = _gdn_specs(s)
    h, d = DN_HEADS, DN_DIM

    def body(u_ref, w_ref, at_ref, qd_ref, kt_ref, dec_ref, o_ref, st_ref, state):
        @pl.when(pl.program_id(0) == 0)
        def _():
            state[...] = jnp.zeros_like(state)

        st_ref[...] = state[...]
        new, o = _gdn_step_math(state[...], u_ref[...], w_ref[...], at_ref[...], qd_ref[...], kt_ref[...], dec_ref[...])
        state[...] = new
        o_ref[...] = o

    return pl.pallas_call(
        body, name="gdn_scan_fwd", grid=(nc,),
        in_specs=[sp["hcd"], sp["hcd"], sp["att"], sp["hcd"], sp["hcd"], sp["dec"]],
        out_specs=[sp["hcd"], pl.BlockSpec((None, h, d, d), lambda n: (n, 0, 0, 0))],
        out_shape=[sp["s_hcd"], jax.ShapeDtypeStruct((nc, h, d, d), F32)],
        scratch_shapes=[pltpu.VMEM((h, d, d), F32)],
        compiler_params=_params(("arbitrary",)),
    )(u, w, attn, qd, kt, dec)


def _gdn_scan_bwd(do, states, u, w, attn, qd, kt, dec):
    s = u.shape[1]
    nc = s // DN_CHUNK
    h, c, d = DN_HEADS, DN_CHUNK, DN_DIM
    rev = lambda n: nc - 1 - n
    hcd = pl.BlockSpec((h, c, d), lambda n: (0, rev(n), 0))
    att = pl.BlockSpec((h, c, c), lambda n: (0, rev(n), 0))
    dec_s = pl.BlockSpec((h, None, 1, 128), lambda n: (0, rev(n), 0, 0))
    sp = _gdn_specs(s)

    def body(do_ref, st_ref, u_ref, w_ref, at_ref, qd_ref, kt_ref, dec_ref,
             du_ref, dw_ref, dat_ref, dqd_ref, dkt_ref, dde_ref, dstate):
        @pl.when(pl.program_id(0) == 0)
        def _():
            dstate[...] = jnp.zeros_like(dstate)

        _, vjp = jax.vjp(_gdn_step_math, st_ref[...], u_ref[...], w_ref[...], at_ref[...], qd_ref[...], kt_ref[...],
                         dec_ref[...])
        dst, du, dw, dat, dqd, dkt, dde = vjp((dstate[...], do_ref[...]))
        dstate[...] = dst
        for ref, val in zip((du_ref, dw_ref, dat_ref, dqd_ref, dkt_ref, dde_ref), (du, dw, dat, dqd, dkt, dde)):
            ref[...] = val

    return pl.pallas_call(
        body, name="gdn_scan_bwd", grid=(nc,),
        in_specs=[hcd, pl.BlockSpec((None, h, d, d), lambda n: (rev(n), 0, 0, 0)), hcd, hcd, att, hcd, hcd, dec_s],
        out_specs=[hcd, hcd, att, hcd, hcd, dec_s],
        out_shape=[sp["s_hcd"], sp["s_hcd"], sp["s_att"], sp["s_hcd"], sp["s_hcd"], sp["s_dec"]],
        scratch_shapes=[pltpu.VMEM((h, d, d), F32)],
        compiler_params=_params(("arbitrary",)),
    )(do, states, u, w, attn, qd, kt, dec)


def _gdn_out_math(o, z, gain):
    return _rms(o, gain) * _silu(z)


def _gdn_out_fwd(o_rows, z_rows, gain):
    return _rowwise("gdn_out_fwd", _gdn_out_math, [o_rows, z_rows], [(DN_DIM, F32)], bcast=[gain])[0]


def _gdn_out_bwd(dy_rows, o_rows, z_rows, gain):
    def fn(dv, ov, zv, g):
        _, vjp = jax.vjp(_gdn_out_math, ov, zv, g)
        return vjp(dv)
    return _rowwise("gdn_out_bwd", fn, [dy_rows, o_rows, z_rows], [(DN_DIM, F32)] * 2, bcast=[gain], accs=[(1, DN_DIM)])


def _heads_major(x):
    s = x.shape[0]
    return x.reshape(s, -1, DN_DIM).transpose(1, 0, 2)


def _heads_minor(x):
    return x.transpose(1, 0, 2).reshape(x.shape[1], -1)


def _pad_row(v, width=128):
    return jnp.pad(v.reshape(1, -1), ((0, 0), (0, width - v.size)))


def _ffn_fwd(tag, x, gains_in, gains_out, wg, wu, wd, widx):
    h = _rms_fwd(f"{tag}_prenorm", x, gains_in)
    a = _mm_cols(f"{tag}_gate", h, wg, widx, BF16)
    b = _mm_cols(f"{tag}_up", h, wu, widx, BF16)
    s = _swiglu_fwd(f"{tag}_swiglu", a, b)
    f = _mm_rows(f"{tag}_down", s, wd, widx, F32)
    x_new = _rms_residual(f"{tag}_postnorm", x, f, gains_out, 0.5)
    return x_new, (x, h, a, b, s, f)


def _ffn_bwd(tag, dx_new, saved, gains_in, gains_out, wg, wu, wd, widx):
    x, h, a, b, s, f = saved
    df, dg_out = _rms_bwd(f"{tag}_postnorm_bwd", dx_new, f, gains_out, 0.5, out_dtype=BF16)
    ds = _mm_rows_t(f"{tag}_down_bwd", df, wd, widx, BF16)
    d_wd = _mm_grad(f"{tag}_down_grad", s, df, wd.shape[0], None)
    da, db = _swiglu_bwd(f"{tag}_swiglu_bwd", ds, a, b)
    d_wg = _mm_grad(f"{tag}_gate_grad", h, da, None, wg.shape[0])
    d_wu = _mm_grad(f"{tag}_up_grad", h, db, None, wu.shape[0])
    dh = _mm_cols_t(f"{tag}_gateup_bwd", [(da, wg), (db, wu)], widx, F32)
    dx, dg_in = _rms_bwd(f"{tag}_prenorm_bwd", dh, x, gains_in, 1.0, residual=dx_new)
    return dx, dg_in, dg_out, d_wg, d_wu, d_wd


def _attn_views(t, dil):
    return t.reshape(t.shape[0] // dil, dil * AT_WIDTH)


def _mixer_fwd(l, h, w, p, bias_tabs):
    s = h.shape[0]
    nc = s // DN_CHUNK
    proj = _mm_cols("w_in_fwd", h, w["w_in"], (l,), F32)
    ops, ops_vjp = jax.vjp(_s5_operators, p["ssm_lambda_re"][l], p["ssm_lambda_im"][l], p["ssm_b_re"][l], p["ssm_b_im"][l],
                           p["ssm_c_re"][l], p["ssm_c_im"][l], p["ssm_d"][l], p["ssm_log_dt"][l])
    u_g = _to_groups(proj[:, :SSM_WIDTH])
    y_g, s_g = _s5_scan_fwd(u_g, *ops)
    y = _from_groups(y_g)
    glu_w, glu_b, gain_ssm = p["ssm_glu_w"][l], p["ssm_glu_b"][l][None], p["ssm_out_gain"][l][None]
    y_ssm = _s5_post_fwd(y, glu_w, glu_b, gain_ssm)
    conv_w, alog, dtb = p["dn_conv_w"][l], _pad_row(p["dn_a_log"][l]), _pad_row(p["dn_dt_bias"][l])
    gain_dn = p["dn_norm_gain"][l][None]
    act = _gdn_prep_fwd(proj, conv_w)
    gates_t = _gates_fwd(proj, alog, dtb)[:, :2 * DN_HEADS].T
    gcol, bcol = gates_t[:DN_HEADS, :, None], gates_t[DN_HEADS:, :, None]
    grow = gates_t[:DN_HEADS].reshape(DN_HEADS, nc, 1, DN_CHUNK)
    u, wy, attn, qd, kt, dec = _gdn_intra_fwd(act, gcol, grow, bcol)
    o_dn, states = _gdn_scan_fwd(u, wy, attn, qd, kt, dec)
    o_rows = o_dn.reshape(DN_HEADS * s, DN_DIM)
    z_rows = _heads_major(proj[:, COL_DZ:COL_DAB]).reshape(DN_HEADS * s, DN_DIM)
    y_dn = _heads_minor(_gdn_out_fwd(o_rows, z_rows, gain_dn).reshape(DN_HEADS, s, DN_DIM))
    q, k, v = proj[:, COL_AQ:COL_AK], proj[:, COL_AK:COL_AV], proj[:, COL_AV:COL_DQKV]
    outs, lses = [], []
    for bi, (_, dil) in enumerate(DILATED):
        o_b, l_b = _attn_branch_fwd(bi, dil, _attn_views(q, dil), _attn_views(k, dil), _attn_views(v, dil), bias_tabs)
        outs.append(o_b.reshape(s, AT_WIDTH))
        lses.append(l_b.reshape(s, AT_WIDTH))
    gain_at = p["attn_out_gain"][l][None]
    y_at, o_at, lse = _attn_merge_fwd(outs, lses, gain_at)
    mix = jnp.concatenate([y_ssm, y_dn, y_at], axis=1).astype(BF16)
    out = _mm_rows("w_out_fwd", mix, w["w_out"], (l,), F32)
    saved = dict(h=h, proj=proj, ops=ops, ops_vjp=ops_vjp, u_g=u_g, s_g=s_g, y=y, act=act, gcol=gcol, grow=grow, bcol=bcol,
                 u=u, wy=wy, attn=attn, qd=qd, kt=kt, dec=dec, states=states, o_rows=o_rows, z_rows=z_rows,
                 q=q, k=k, v=v, o_at=o_at, lse=lse, mix=mix)
    return out, saved


def _mixer_bwd(l, dout, sv, w, p, bias_tabs):
    s = dout.shape[0]
    proj = sv["proj"]
    g = {}
    g["w_out"] = _mm_grad("w_out_grad", sv["mix"], dout, w["w_out"].shape[0], None)
    dmix = _mm_rows_t("w_out_bwd", dout, w["w_out"], (l,), F32)
    d_ssm, d_dn, d_at = dmix[:, :SSM_WIDTH], dmix[:, SSM_WIDTH:SSM_WIDTH + DN_WIDTH], dmix[:, SSM_WIDTH + DN_WIDTH:]
    glu_w, glu_b, gain_ssm = p["ssm_glu_w"][l], p["ssm_glu_b"][l][None], p["ssm_out_gain"][l][None]
    dy, g["ssm_glu_w"], dglu_b, dgain_ssm = _s5_post_bwd(d_ssm, sv["y"], glu_w, glu_b, gain_ssm)
    g["ssm_glu_b"], g["ssm_out_gain"] = dglu_b[0], dgain_ssm[0]
    du_g, *d_ops = _s5_scan_bwd(_to_groups(dy), sv["u_g"], sv["s_g"], *sv["ops"])
    (g["ssm_lambda_re"], g["ssm_lambda_im"], g["ssm_b_re"], g["ssm_b_im"], g["ssm_c_re"], g["ssm_c_im"], g["ssm_d"],
     g["ssm_log_dt"]) = sv["ops_vjp"](tuple(d_ops))
    d_u = _from_groups(du_g)
    gain_at = p["attn_out_gain"][l][None]
    do, dlt, dgain_at = _attn_merge_bwd(d_at, sv["o_at"], gain_at)
    g["attn_out_gain"] = dgain_at[0]
    dqs, dks, dvs, dbs = [], [], [], []
    for bi, (_, dil) in enumerate(DILATED):
        vw = lambda t: _attn_views(t, dil)
        dq2, dk2, dv2, db = _attn_branch_bwd(bi, dil, vw(sv["q"]), vw(sv["k"]), vw(sv["v"]), vw(do), vw(sv["lse"]), vw(dlt),
                                            bias_tabs)
        dqs.append(dq2.reshape(s, AT_WIDTH))
        dks.append(dk2.reshape(s, AT_WIDTH))
        dvs.append(dv2.reshape(s, AT_WIDTH))
        dbs.append(db)
    dq, dk, dv = _add3("attn_dq_sum", *dqs), _add3("attn_dk_sum", *dks), _add3("attn_dv_sum", *dvs)
    g["bias_tabs"] = jnp.stack(dbs)
    conv_w, alog, dtb = p["dn_conv_w"][l], _pad_row(p["dn_a_log"][l]), _pad_row(p["dn_dt_bias"][l])
    gain_dn = p["dn_norm_gain"][l][None]
    dy_rows = _heads_major(d_dn).reshape(DN_HEADS * s, DN_DIM)
    do_rows, dz_rows, dgain_dn = _gdn_out_bwd(dy_rows, sv["o_rows"], sv["z_rows"], gain_dn)
    g["dn_norm_gain"] = dgain_dn[0]
    d_scan = _gdn_scan_bwd(do_rows.reshape(DN_HEADS, s, DN_DIM), sv["states"], sv["u"], sv["wy"], sv["attn"], sv["qd"],
                           sv["kt"], sv["dec"])
    dact, dgc, dgr, dbc = _gdn_intra_bwd(sv["act"], sv["gcol"], sv["grow"], sv["bcol"], *d_scan)
    dgates_t = jnp.concatenate([dgc[..., 0] + dgr.reshape(DN_HEADS, s), dbc[..., 0]], axis=0)
    dgates = jnp.pad(dgates_t.T, ((0, 0), (0, 128 - 2 * DN_HEADS)))
    dab, dalog, ddtb = _gates_bwd(dgates, proj, alog, dtb)
    g["dn_a_log"], g["dn_dt_bias"] = dalog[0, :DN_HEADS], ddtb[0, :DN_HEADS]
    dqkv, g["dn_conv_w"] = _gdn_prep_bwd(dact, proj, conv_w)
    dz = _heads_minor(dz_rows.reshape(DN_HEADS, s, DN_DIM))
    dproj = jnp.concatenate([d_u, dq, dk, dv, dqkv, dz, dab, jnp.zeros((s, N_IN_PAD - COL_DAB - 128), F32)], axis=1)
    g["w_in"] = _mm_grad("w_in_grad", sv["h"], dproj, None, 1)[0]
    dh = _mm_cols_t("w_in_bwd", [(dproj, w["w_in"])], (l,), F32)
    return dh, g


def _local_step(x, target, w, p):
    depth = p["norm_gains"].shape[0]
    gains = p["norm_gains"]
    bias_tabs, bias_vjp = jax.vjp(_attn_bias_tables, p["rel_bias"])
    saved = []
    for l in range(depth):
        gn = lambda i: gains[l, i][None]
        x, sv1 = _ffn_fwd("ffn1", x, gn(0), gn(1), w["ffn_w_gate"], w["ffn_w_up"], w["ffn_w_down"], (l, 0))
        h = _rms_fwd("mix_prenorm", x, gn(2))
        out, svm = _mixer_fwd(l, h, w, p, bias_tabs)
        x_mid = x
        x = _rms_residual("mix_postnorm", x, out, gn(3), 1.0)
        x, sv2 = _ffn_fwd("ffn2", x, gn(4), gn(5), w["ffn_w_gate"], w["ffn_w_up"], w["ffn_w_down"], (l, 1))
        saved.append((sv1, svm, x_mid, out, sv2))
    loss, dx = _loss_and_grad(x, target)

    small = ["ssm_lambda_re", "ssm_lambda_im", "ssm_b_re", "ssm_b_im", "ssm_c_re", "ssm_c_im", "ssm_d", "ssm_log_dt",
             "ssm_glu_w", "ssm_glu_b", "ssm_out_gain", "dn_conv_w", "dn_a_log", "dn_dt_bias", "dn_norm_gain", "attn_out_gain"]
    per_layer = {n: [None] * depth for n in small + ["norm_gains", "w_in", "w_out", "ffn_w_gate", "ffn_w_up", "ffn_w_down"]}
    d_tabs = None
    for l in reversed(range(depth)):
        gn = lambda i: gains[l, i][None]
        sv1, svm, x_mid, out, sv2 = saved[l]
        dx, dg4, dg5, wg2, wu2, wd2 = _ffn_bwd("ffn2", dx, sv2, gn(4), gn(5), w["ffn_w_gate"], w["ffn_w_up"],
                                               w["ffn_w_down"], (l, 1))
        dout, dg3 = _rms_bwd("mix_postnorm_bwd", dx, out, gn(3), 1.0)
        dh, gm = _mixer_bwd(l, dout, svm, w, p, bias_tabs)
        dx, dg2 = _rms_bwd("mix_prenorm_bwd", dh, x_mid, gn(2), 1.0, residual=dx)
        dx, dg0, dg1, wg1, wu1, wd1 = _ffn_bwd("ffn1", dx, sv1, gn(0), gn(1), w["ffn_w_gate"], w["ffn_w_up"],
                                               w["ffn_w_down"], (l, 0))
        per_layer["norm_gains"][l] = jnp.concatenate([dg0, dg1, dg2, dg3, dg4, dg5], axis=0)
        per_layer["ffn_w_gate"][l] = jnp.stack([wg1, wg2], axis=1)
        per_layer["ffn_w_up"][l] = jnp.stack([wu1, wu2], axis=1)
        per_layer["ffn_w_down"][l] = jnp.stack([wd1, wd2], axis=1)
        d_tabs = gm["bias_tabs"] if d_tabs is None else d_tabs + gm["bias_tabs"]
        for n in small + ["w_in", "w_out"]:
            per_layer[n][l] = gm[n]
    grads = {n: jnp.stack(per_layer[n], axis=0) for n in small + ["norm_gains", "w_in"]}
    for n in ("w_out", "ffn_w_gate", "ffn_w_up", "ffn_w_down"):
        grads[n] = jnp.stack(per_layer[n], axis=1)
    grads["rel_bias"] = bias_vjp(d_tabs)[0]
    return loss, dx, grads


_ANY = pl.BlockSpec(memory_space=pl.ANY)


def _place():
    return lax.axis_index("x"), lax.axis_index("y"), lax.axis_index("c")


def _all_gather_xy(name, shard):
    def body(src, out, send_sems, recv_sems, local_sem):
        x, y, c = _place()
        sib = (x, y, 1 - c)
        chips = [(1 - x, y), (x, 1 - y), (1 - x, 1 - y)]
        blk = lambda cx, cy: 2 * cx + cy

        def copy(k, src_ref, j, half, to):
            return pltpu.make_async_remote_copy(src_ref=src_ref, dst_ref=out.at[j, half], send_sem=send_sems.at[k],
                                                recv_sem=recv_sems.at[k], device_id=to, device_id_type=MESH)

        mine = pltpu.make_async_copy(src, out.at[blk(x, y)], local_sem)
        mine.start()
        first = [copy(k, src.at[c], blk(x, y), c, (*chip, c)) for k, chip in enumerate(chips)]
        for cp in first:
            cp.start()
        passed = [copy(3 + k, out.at[blk(*chip), c], blk(*chip), c, sib) for k, chip in enumerate(chips)]
        for k, chip in enumerate(chips):
            copy(k, src.at[c], blk(*chip), c, (*chip, c)).wait_recv()
            passed[k].start()
        for k, chip in enumerate(chips):
            copy(3 + k, src.at[c], blk(*chip), 1 - c, sib).wait_recv()
        for cp in first + passed:
            cp.wait_send()
        mine.wait()

    return pl.pallas_call(
        body, name=name, in_specs=[_ANY], out_specs=_ANY,
        out_shape=jax.ShapeDtypeStruct((4,) + shard.shape, shard.dtype),
        scratch_shapes=[pltpu.SemaphoreType.DMA((6,)), pltpu.SemaphoreType.DMA((6,)), pltpu.SemaphoreType.DMA],
    )(shard)


def _exchange8(name, src, same_to_all=False):
    blk_shape = src.shape if same_to_all else src.shape[1:]

    def body(src_ref, dst, send_sems, recv_sems, local_sem):
        x, y, c = _place()
        me = 4 * x + 2 * y + c
        part = (lambda i: src_ref) if same_to_all else (lambda i: src_ref.at[i])

        def peer(k):
            return (1 - x if k & 4 else x, 1 - y if k & 2 else y, 1 - c if k & 1 else c)

        def copy(k, dst_block):
            px, py, pc = peer(k)
            return pltpu.make_async_remote_copy(src_ref=part(4 * px + 2 * py + pc), dst_ref=dst.at[dst_block],
                                                send_sem=send_sems.at[k - 1], recv_sem=recv_sems.at[k - 1],
                                                device_id=(px, py, pc), device_id_type=MESH)

        mine = pltpu.make_async_copy(part(me), dst.at[me], local_sem)
        mine.start()
        sends = [copy(k, me) for k in range(1, 8)]
        for cp in sends:
            cp.start()
        for k in range(1, 8):
            px, py, pc = peer(k)
            copy(k, 4 * px + 2 * py + pc).wait_recv()
        for cp in sends:
            cp.wait_send()
        mine.wait()

    return pl.pallas_call(
        body, name=name, in_specs=[_ANY], out_specs=_ANY,
        out_shape=jax.ShapeDtypeStruct((8,) + blk_shape, src.dtype),
        scratch_shapes=[pltpu.SemaphoreType.DMA((7,)), pltpu.SemaphoreType.DMA((7,)), pltpu.SemaphoreType.DMA],
    )(src)


def _pair_swap(name, half):
    def body(src, out, send_sem, recv_sem, local_sem):
        x, y, c = _place()
        mine = pltpu.make_async_copy(src, out.at[c], local_sem)
        mine.start()
        push = pltpu.make_async_remote_copy(src_ref=src, dst_ref=out.at[c], send_sem=send_sem, recv_sem=recv_sem,
                                            device_id=(x, y, 1 - c), device_id_type=MESH)
        push.start()
        pltpu.make_async_remote_copy(src_ref=src, dst_ref=out.at[1 - c], send_sem=send_sem, recv_sem=recv_sem,
                                     device_id=(x, y, 1 - c), device_id_type=MESH).wait_recv()
        push.wait_send()
        mine.wait()

    return pl.pallas_call(
        body, name=name, in_specs=[_ANY], out_specs=_ANY,
        out_shape=jax.ShapeDtypeStruct((2,) + half.shape, half.dtype),
        scratch_shapes=[pltpu.SemaphoreType.DMA, pltpu.SemaphoreType.DMA, pltpu.SemaphoreType.DMA],
    )(half)


def _sum8(name, parts):
    _, r, c = parts.shape
    tm = min(_row_tile(r), 256)

    def body(p_ref, o_ref):
        acc = p_ref[0].astype(F32)
        for i in range(1, 8):
            acc = acc + p_ref[i].astype(F32)
        o_ref[...] = acc

    return pl.pallas_call(
        body, name=name, grid=(r // tm,), in_specs=[pl.BlockSpec((8, tm, c), lambda i: (0, i, 0))],
        out_specs=pl.BlockSpec((tm, c), lambda i: (i, 0)), out_shape=jax.ShapeDtypeStruct((r, c), F32),
        compiler_params=_params(("parallel",), 48),
    )(parts)


def _cast_bf16(name, w2d):
    return _rowwise(name, lambda v: v, [w2d], [(w2d.shape[1], BF16)])[0]


def _adamw(name, g, w, m, v):
    def fn(gv, wv, mv, vv):
        m2 = ADAM_B1 * mv + (1.0 - ADAM_B1) * gv
        v2 = ADAM_B2 * vv + (1.0 - ADAM_B2) * (gv * gv)
        m_hat = m2 / (1.0 - ADAM_B1 ** ADAM_STEP)
        v_hat = v2 / (1.0 - ADAM_B2 ** ADAM_STEP)
        return -ADAM_LR * (m_hat / (jnp.sqrt(v_hat) + ADAM_EPS) + ADAM_WD * wv), m2, v2
    return _rowwise(name, fn, [g, w, m, v], [(g.shape[1], F32)] * 3)


def _pack(tensors):
    flat = jnp.concatenate([t.reshape(-1).astype(F32) for t in tensors])
    rows = -(-flat.size // (128 * 16)) * 16
    return jnp.pad(flat, (0, rows * 128 - flat.size)).reshape(rows, 128)


def _unpack(buf, shapes):
    flat, out, at = buf.reshape(-1), [], 0
    for sh in shapes:
        n = math.prod(sh)
        out.append(flat[at:at + n].reshape(sh))
        at += n
    return out


_WEIGHTS = ("norm_gains", "ffn_w_gate", "ffn_w_up", "ffn_w_down", "w_in", "w_out", "ssm_lambda_re", "ssm_lambda_im",
            "ssm_b_re", "ssm_b_im", "ssm_c_re", "ssm_c_im", "ssm_d", "ssm_log_dt", "ssm_glu_w", "ssm_glu_b",
            "ssm_out_gain", "dn_conv_w", "dn_a_log", "dn_dt_bias", "dn_norm_gain", "attn_out_gain", "rel_bias")
_MATRICES = ("ffn_w_gate", "ffn_w_up", "ffn_w_down", "w_in", "w_out")
_CUT_SMALL = {"norm_gains": 2, "ssm_glu_w": 1, "dn_conv_w": 2}
_REPLICATED = tuple(n for n in _WEIGHTS if n not in _MATRICES and n not in _CUT_SMALL)


def _gather_matrix(name, shard):
    c = shard.shape[-1]
    half = _cast_bf16(f"{name}_cast", shard.reshape(-1, c)).reshape(2, -1, c)
    return _all_gather_xy(f"{name}_gather", half).reshape((4,) + shard.shape)


def _reduce_matrix(name, g, shard_shape):
    c = g.shape[-1]
    parts = _exchange8(f"{name}_exchange", g.reshape(8, -1, c))
    half = _sum8(f"{name}_sum", parts)
    return _pair_swap(f"{name}_swap", half).reshape(shard_shape)


def kernel(x, norm_gains, ffn_w_gate, ffn_w_up, ffn_w_down, w_in, w_out, ssm_lambda_re, ssm_lambda_im, ssm_b_re, ssm_b_im, ssm_c_re, ssm_c_im, ssm_d, ssm_log_dt, ssm_glu_w, ssm_glu_b, ssm_out_gain, dn_conv_w, dn_a_log, dn_dt_bias, dn_norm_gain, attn_out_gain, rel_bias, loss_target, m_norm_gains, m_ffn_w_gate, m_ffn_w_up, m_ffn_w_down, m_w_in, m_w_out, m_ssm_lambda_re, m_ssm_lambda_im, m_ssm_b_re, m_ssm_b_im, m_ssm_c_re, m_ssm_c_im, m_ssm_d, m_ssm_log_dt, m_ssm_glu_w, m_ssm_glu_b, m_ssm_out_gain, m_dn_conv_w, m_dn_a_log, m_dn_dt_bias, m_dn_norm_gain, m_attn_out_gain, m_rel_bias, v_norm_gains, v_ffn_w_gate, v_ffn_w_up, v_ffn_w_down, v_w_in, v_w_out, v_ssm_lambda_re, v_ssm_lambda_im, v_ssm_b_re, v_ssm_b_im, v_ssm_c_re, v_ssm_c_im, v_ssm_d, v_ssm_log_dt, v_ssm_glu_w, v_ssm_glu_b, v_ssm_out_gain, v_dn_conv_w, v_dn_a_log, v_dn_dt_bias, v_dn_norm_gain, v_attn_out_gain, v_rel_bias):
    wts = dict(zip(_WEIGHTS, (norm_gains, ffn_w_gate, ffn_w_up, ffn_w_down, w_in, w_out, ssm_lambda_re, ssm_lambda_im, ssm_b_re, ssm_b_im, ssm_c_re, ssm_c_im, ssm_d, ssm_log_dt, ssm_glu_w, ssm_glu_b, ssm_out_gain, dn_conv_w, dn_a_log, dn_dt_bias, dn_norm_gain, attn_out_gain, rel_bias)))
    mom = dict(zip(_WEIGHTS, (m_norm_gains, m_ffn_w_gate, m_ffn_w_up, m_ffn_w_down, m_w_in, m_w_out, m_ssm_lambda_re, m_ssm_lambda_im, m_ssm_b_re, m_ssm_b_im, m_ssm_c_re, m_ssm_c_im, m_ssm_d, m_ssm_log_dt, m_ssm_glu_w, m_ssm_glu_b, m_ssm_out_gain, m_dn_conv_w, m_dn_a_log, m_dn_dt_bias, m_dn_norm_gain, m_attn_out_gain, m_rel_bias)))
    var = dict(zip(_WEIGHTS, (v_norm_gains, v_ffn_w_gate, v_ffn_w_up, v_ffn_w_down, v_w_in, v_w_out, v_ssm_lambda_re, v_ssm_lambda_im, v_ssm_b_re, v_ssm_b_im, v_ssm_c_re, v_ssm_c_im, v_ssm_d, v_ssm_log_dt, v_ssm_glu_w, v_ssm_glu_b, v_ssm_out_gain, v_dn_conv_w, v_dn_a_log, v_dn_dt_bias, v_dn_norm_gain, v_attn_out_gain, v_rel_bias)))
    depth, d_model = norm_gains.shape[0], x.shape[-1]
    chip = 2 * lax.axis_index("x") + lax.axis_index("y")

    w = {n: _gather_matrix(n, wts[n]) for n in ("ffn_w_gate", "ffn_w_up", "ffn_w_down", "w_out")}
    w_in_all = _gather_matrix("w_in", w_in)
    w_in_all = jnp.transpose(w_in_all, (1, 2, 0, 3)).reshape(depth, d_model, N_IN)
    w["w_in"] = jnp.pad(w_in_all, ((0, 0), (0, 0), (0, N_IN_PAD - N_IN)))[None]
    cut_names = tuple(_CUT_SMALL)
    cut_pack = _pack([wts[n] for n in cut_names])
    cut_all = _all_gather_xy("small_gather", cut_pack.reshape(2, -1, 128)).reshape(4, -1, 128)
    p = {n: wts[n] for n in _REPLICATED}
    per_chip = [_unpack(cut_all[j], [wts[n].shape for n in cut_names]) for j in range(4)]
    for i, n in enumerate(cut_names):
        p[n] = jnp.concatenate([per_chip[j][i] for j in range(4)], axis=_CUT_SMALL[n])

    loss, dx, grads = _local_step(x[0], loss_target[0], w, p)
    loss = lax.psum(loss, ("x", "y", "c"))

    total = {}
    for n in ("ffn_w_gate", "ffn_w_up", "ffn_w_down", "w_out"):
        total[n] = _reduce_matrix(n, grads[n], wts[n].shape)
    g_in = grads["w_in"][:, :, :N_IN].reshape(depth, d_model, 4, N_IN // 4)
    total["w_in"] = _reduce_matrix("w_in", jnp.transpose(g_in, (2, 0, 1, 3)), w_in.shape)
    small_names = _REPLICATED + cut_names
    small_sum = _sum8("small_sum", _exchange8("small_exchange", _pack([grads[n] for n in small_names]), same_to_all=True))
    for n, g in zip(small_names, _unpack(small_sum, [grads[n].shape for n in small_names])):
        if n in _CUT_SMALL:
            ax = _CUT_SMALL[n]
            g = lax.dynamic_slice_in_dim(g, chip * wts[n].shape[ax], wts[n].shape[ax], axis=ax)
        total[n] = g

    delta, new_m, new_v = {}, {}, {}
    for n in _MATRICES + cut_names:
        c = wts[n].shape[-1]
        d2, m2, v2 = _adamw(f"{n}_adamw", total[n].reshape(-1, c), wts[n].reshape(-1, c), mom[n].reshape(-1, c),
                            var[n].reshape(-1, c))
        delta[n], new_m[n], new_v[n] = (t.reshape(wts[n].shape) for t in (d2, m2, v2))
    rep_shapes = [wts[n].shape for n in _REPLICATED]
    packed = _adamw("small_adamw", _pack([total[n] for n in _REPLICATED]), _pack([wts[n] for n in _REPLICATED]),
                    _pack([mom[n] for n in _REPLICATED]), _pack([var[n] for n in _REPLICATED]))
    for dst, buf in zip((delta, new_m, new_v), packed):
        dst.update(zip(_REPLICATED, _unpack(buf, rep_shapes)))

    return (loss, dx[None], *[total[n] for n in _WEIGHTS], *[delta[n] for n in _WEIGHTS],
            *[new_m[n] for n in _WEIGHTS], *[new_v[n] for n in _WEIGHTS])
```

```python
import functools
import math

import jax
import jax.numpy as jnp
from jax import lax
from jax.experimental import pallas as pl
from jax.experimental.pallas import tpu as pltpu

F32 = jnp.float32
BF16 = jnp.bfloat16
HI = lax.Precision.HIGHEST
MESH = pl.DeviceIdType.MESH

NORM_EPS = 1e-6
NEG_INF = -1e30
SSM_GROUPS, SSM_CH, SSM_STATE, SSM_WIDTH = 32, 16, 64, 512
SSM_T = 16
DN_HEADS, DN_DIM, DN_WIDTH, DN_CONV, DN_CHUNK = 6, 128, 768, 4, 64
AT_HEADS, AT_DIM, AT_WIDTH, AT_BLK = 6, 128, 768, 128
DILATED = ((128, 1), (512, 4), (2048, 16))
N_BUCKETS, REL_MAX = 32, 2048
N_IN = SSM_WIDTH + 3 * AT_WIDTH + 4 * DN_WIDTH + 2 * DN_HEADS
N_IN_PAD = 6144
COL_AQ, COL_AK, COL_AV, COL_DQKV, COL_DZ, COL_DAB = 512, 1280, 2048, 2816, 5120, 5888
ADAM_LR, ADAM_B1, ADAM_B2, ADAM_EPS, ADAM_WD, ADAM_STEP = 0.001, 0.9, 0.999, 1e-08, 0.01, 10
V7X_VMEM_BYTES = 64 * 1024 * 1024
NN = (((1,), (0,)), ((), ()))
NT = (((1,), (1,)), ((), ()))
TN = (((0,), (0,)), ((), ()))
BNN = (((2,), (1,)), ((0,), (0,)))
BNT = (((2,), (2,)), ((0,), (0,)))
BTN = (((1,), (1,)), ((0,), (0,)))


def _params(sem, vmem_mb=None):
    kw = {}
    if vmem_mb is not None:
        kw["vmem_limit_bytes"] = min(vmem_mb * 1024 * 1024, V7X_VMEM_BYTES - 8 * 1024 * 1024)
    return pltpu.CompilerParams(dimension_semantics=sem, **kw)


def _dotf(a, b, dims=NN):
    return lax.dot_general(a, b, dims, precision=HI, preferred_element_type=F32)


def _dot3(a, b, dims=NN):
    return lax.dot_general(a, b, dims, precision=lax.Precision.HIGH, preferred_element_type=F32)


def _dotb(a, b, dims=NN):
    return lax.dot_general(a.astype(BF16), b.astype(BF16), dims, preferred_element_type=F32)


def _sigmoid(x):
    return 1.0 / (1.0 + jnp.exp(-x))


def _silu(x):
    return x * _sigmoid(x)


def _softplus(x):
    return jnp.maximum(x, 0.0) + jnp.log(1.0 + jnp.exp(-jnp.abs(x)))


def _gelu(x):
    return 0.5 * x * (1.0 + jnp.tanh(math.sqrt(2.0 / math.pi) * (x + 0.044715 * x * x * x)))


def _rms(x, gain):
    return x * lax.rsqrt(jnp.mean(x * x, axis=-1, keepdims=True) + NORM_EPS) * gain


def _row_tile(s):
    for t in (512, 256, 128, 64, 32, 16, 8):
        if s % t == 0:
            return t
    return s


def _mm(name, pairs, *, grid, a_blk, a_map, b_blk, b_map, o_shape, o_blk, o_map, dims, out_dtype=F32, vmem_mb=48):
    n_red = grid[-1]
    n_pairs = len(pairs)
    acc_shape = tuple(d for d in o_blk if d is not None)

    def body(*refs):
        ins, o_ref, scr = refs[:2 * n_pairs], refs[2 * n_pairs], refs[2 * n_pairs + 1:]
        part = _dotb(ins[0][...], ins[1][...], dims)
        for p in range(1, n_pairs):
            part = part + _dotb(ins[2 * p][...], ins[2 * p + 1][...], dims)
        if n_red == 1:
            o_ref[...] = part.astype(o_ref.dtype)
        else:
            acc = scr[0]
            r = pl.program_id(len(grid) - 1)

            @pl.when(r == 0)
            def _():
                acc[...] = part

            @pl.when(r > 0)
            def _():
                acc[...] += part

            @pl.when(r == n_red - 1)
            def _():
                o_ref[...] = acc[...].astype(o_ref.dtype)

    in_specs, args = [], []
    for a, b in pairs:
        in_specs += [pl.BlockSpec(a_blk, a_map), pl.BlockSpec(b_blk, b_map)]
        args += [a, b]
    return pl.pallas_call(
        body, name=name, grid=grid, in_specs=in_specs, out_specs=pl.BlockSpec(o_blk, o_map),
        out_shape=jax.ShapeDtypeStruct(o_shape, out_dtype),
        scratch_shapes=[pltpu.VMEM(acc_shape, F32)] if n_red > 1 else [],
        compiler_params=_params(("parallel",) * (len(grid) - 1) + ("arbitrary",), vmem_mb),
    )(*args)


def _col_tile(n):
    for t in (1536, 1408, 1024, 768, 512, 384, 256, 128):
        if n % t == 0:
            return t
    return n


def _mm_cols(name, a, w, widx, out_dtype):
    s, k = a.shape
    j_n, nj = w.shape[0], w.shape[-1]
    tm, tn = _row_tile(s), _col_tile(nj)
    nt = nj // tn
    lead = (None,) * (1 + len(widx))
    return _mm(name, [(a, w)], grid=(j_n, nt, s // tm, 1),
               a_blk=(tm, k), a_map=lambda j, c, i, r: (i, 0),
               b_blk=lead + (k, tn), b_map=lambda j, c, i, r: (j, *widx, 0, c),
               o_shape=(s, j_n * nj), o_blk=(tm, tn), o_map=lambda j, c, i, r: (i, j * nt + c),
               dims=NN, out_dtype=out_dtype)


def _mm_rows(name, a, w, widx, out_dtype):
    s = a.shape[0]
    j_n, kj, n = w.shape[0], w.shape[-2], w.shape[-1]
    tm = _row_tile(s)
    lead = (None,) * (1 + len(widx))
    return _mm(name, [(a, w)], grid=(s // tm, j_n),
               a_blk=(tm, kj), a_map=lambda i, j: (i, j),
               b_blk=lead + (kj, n), b_map=lambda i, j: (j, *widx, 0, 0),
               o_shape=(s, n), o_blk=(tm, n), o_map=lambda i, j: (i, 0), dims=NN, out_dtype=out_dtype)


def _mm_rows_t(name, a, w, widx, out_dtype):
    s, n = a.shape
    j_n, kj = w.shape[0], w.shape[-2]
    tm = _row_tile(s)
    lead = (None,) * (1 + len(widx))
    return _mm(name, [(a, w)], grid=(j_n, s // tm, 1),
               a_blk=(tm, n), a_map=lambda j, i, r: (i, 0),
               b_blk=lead + (kj, n), b_map=lambda j, i, r: (j, *widx, 0, 0),
               o_shape=(s, j_n * kj), o_blk=(tm, kj), o_map=lambda j, i, r: (i, j), dims=NT, out_dtype=out_dtype)


def _mm_cols_t(name, pairs, widx, out_dtype):
    a0, w0 = pairs[0]
    s = a0.shape[0]
    j_n, k, nj = w0.shape[0], w0.shape[-2], w0.shape[-1]
    tm, tn = _row_tile(s), _col_tile(nj)
    nt = nj // tn
    lead = (None,) * (1 + len(widx))
    return _mm(name, pairs, grid=(s // tm, j_n * nt),
               a_blk=(tm, tn), a_map=lambda i, r: (i, r),
               b_blk=lead + (k, tn), b_map=lambda i, r: (lax.div(r, nt), *widx, 0, lax.rem(r, nt)),
               o_shape=(s, k), o_blk=(tm, k), o_map=lambda i, r: (i, 0), dims=NT, out_dtype=out_dtype)


def _mm_grad(name, a, b, a_cols, b_cols, out_dtype=BF16):
    s = a.shape[0]
    tm = _row_tile(s)
    if a_cols is not None:
        j_n, ka, nb = a_cols, a.shape[1] // a_cols, b.shape[1]
        tk, tn = ka, _col_tile(nb)
        a_map = lambda j, kb, c, i: (i, j)
        b_map = lambda j, kb, c, i: (i, c)
    else:
        j_n, ka, nb = b_cols, a.shape[1], b.shape[1] // b_cols
        tk, tn = min(ka, 1024), _col_tile(nb)
        nt_ = nb // tn
        a_map = lambda j, kb, c, i: (i, kb)
        b_map = lambda j, kb, c, i: (i, j * nt_ + c)
    return _mm(name, [(a, b)], grid=(j_n, ka // tk, nb // tn, s // tm),
               a_blk=(tm, tk), a_map=a_map, b_blk=(tm, tn), b_map=b_map,
               o_shape=(j_n, ka, nb), o_blk=(None, tk, tn), o_map=lambda j, kb, c, i: (j, kb, c),
               dims=TN, out_dtype=out_dtype)


def _rowwise(name, fn, rows, outs, *, bcast=(), accs=(), tm=None, vmem_mb=48):
    rows = [r if isinstance(r, tuple) else (r, r.shape[1], 0) for r in rows]
    s = rows[0][0].shape[0]
    tm = tm or min(_row_tile(s), 256)
    nr, nb, no = len(rows), len(bcast), len(outs)

    def body(*refs):
        o_refs, a_refs = refs[nr + nb:nr + nb + no], refs[nr + nb + no:]
        res = fn(*[r[...] for r in refs[:nr + nb]])
        if not isinstance(res, (tuple, list)):
            res = (res,)
        for o, v in zip(o_refs, res[:no]):
            o[...] = v.astype(o.dtype)
        if a_refs:
            i = pl.program_id(0)
            for a, v in zip(a_refs, res[no:]):
                @pl.when(i == 0)
                def _(a=a, v=v):
                    a[...] = v

                @pl.when(i > 0)
                def _(a=a, v=v):
                    a[...] += v

    in_specs = [pl.BlockSpec((tm, w), lambda i, c=c: (i, c)) for _, w, c in rows]
    in_specs += [pl.BlockSpec(b.shape, lambda i, nd=b.ndim: (0,) * nd) for b in bcast]
    out_specs = [pl.BlockSpec((tm, c), lambda i: (i, 0)) for c, _ in outs]
    out_specs += [pl.BlockSpec(sh, lambda i, nd=len(sh): (0,) * nd) for sh in accs]
    out_shape = [jax.ShapeDtypeStruct((s, c), dt) for c, dt in outs] + [jax.ShapeDtypeStruct(sh, F32) for sh in accs]
    res = pl.pallas_call(
        body, name=name, grid=(s // tm,), in_specs=in_specs, out_specs=out_specs, out_shape=out_shape,
        compiler_params=_params(("arbitrary",) if accs else ("parallel",), vmem_mb),
    )(*[r[0] for r in rows], *bcast)
    return res


def _rms_fwd(name, x, gain):
    return _rowwise(name, lambda xv, g: _rms(xv, g), [x], [(x.shape[1], BF16)], bcast=[gain])[0]


def _rms_residual(name, x, f, gain, scale):
    return _rowwise(name, lambda xv, fv, g: xv + scale * _rms(fv, g), [x, f], [(x.shape[1], F32)], bcast=[gain])[0]


def _rms_bwd_math(dy, x, gain):
    r = lax.rsqrt(jnp.mean(x * x, axis=-1, keepdims=True) + NORM_EPS)
    xh = x * r
    dxh = dy * gain
    dx = r * (dxh - xh * jnp.mean(dxh * xh, axis=-1, keepdims=True))
    return dx, jnp.sum(dy * xh, axis=0, keepdims=True)


def _rms_bwd(name, dy, x, gain, scale=1.0, residual=None, out_dtype=F32):
    d = x.shape[1]
    if residual is None:
        fn = lambda dyv, xv, g: _rms_bwd_math(scale * dyv.astype(F32), xv, g)
        rows = [dy, x]
    else:
        def fn(dyv, xv, rv, g):
            dx, dg = _rms_bwd_math(scale * dyv.astype(F32), xv, g)
            return dx + rv, dg
        rows = [dy, x, residual]
    return _rowwise(name, fn, rows, [(d, out_dtype)], bcast=[gain], accs=[(1, d)])


def _swiglu_fwd(name, a, b):
    return _rowwise(name, lambda av, bv: _silu(av.astype(F32)) * bv.astype(F32), [a, b], [(a.shape[1], BF16)])[0]


def _swiglu_bwd(name, ds, a, b):
    def fn(dsv, av, bv):
        dsv, av, bv = dsv.astype(F32), av.astype(F32), bv.astype(F32)
        sg = _sigmoid(av)
        return dsv * bv * sg * (1.0 + av * (1.0 - sg)), dsv * av * sg
    f = a.shape[1]
    return _rowwise(name, fn, [ds, a, b], [(f, BF16), (f, BF16)])


def _loss_and_grad(y, target):
    d = y.shape[1]

    def fn(yv, tv):
        e = yv - tv
        part = 0.5 * jnp.sum(jnp.mean(e * e, axis=-1, keepdims=True), axis=0, keepdims=True)
        return e * (1.0 / d), jnp.broadcast_to(part, (1, 128))
    dy, loss = _rowwise("loss_head", fn, [y, target], [(d, F32)], accs=[(1, 128)])
    return loss[0, 0], dy


def _s5_operators(lam_re, lam_im, b_re, b_im, c_re, c_im, d_skip, log_dt):
    t_n, ch, p_n = SSM_T, SSM_CH, SSM_STATE
    dt = jnp.exp(log_dt)[:, None]
    ld_re, ld_im = lam_re * dt, lam_im * dt
    k = jnp.arange(t_n + 1, dtype=F32)[None, :, None]
    mag = jnp.exp(ld_re[:, None, :] * k)
    pw_re, pw_im = mag * jnp.cos(ld_im[:, None, :] * k), mag * jnp.sin(ld_im[:, None, :] * k)
    lb_re, lb_im = pw_re[:, 1], pw_im[:, 1]
    den = lam_re * lam_re + lam_im * lam_im
    f_re = ((lb_re - 1.0) * lam_re + lb_im * lam_im) / den
    f_im = (lb_im * lam_re - (lb_re - 1.0) * lam_im) / den
    bb_re = f_re[..., None] * b_re - f_im[..., None] * b_im
    bb_im = f_re[..., None] * b_im + f_im[..., None] * b_re
    cp_re = c_re[:, None] * pw_re[:, :t_n, None, :] - c_im[:, None] * pw_im[:, :t_n, None, :]
    cp_im = c_re[:, None] * pw_im[:, :t_n, None, :] + c_im[:, None] * pw_re[:, :t_n, None, :]
    taps = (jnp.einsum("gtcp,gpd->gtcd", cp_re, bb_re, precision=HI)
            - jnp.einsum("gtcp,gpd->gtcd", cp_im, bb_im, precision=HI))
    jt = jnp.arange(t_n)
    lag = jt[None, :] - jt[:, None]
    m5 = jnp.where((lag >= 0)[None, :, :, None, None], taps[:, jnp.maximum(lag, 0)], 0.0)
    m_op = jnp.transpose(m5, (0, 1, 4, 2, 3)).reshape(SSM_GROUPS, t_n * ch, t_n * ch)
    m_op = m_op + jnp.eye(t_n * ch, dtype=F32)[None] * jnp.tile(d_skip.reshape(SSM_GROUPS, 1, ch), (1, t_n, 1)).reshape(
        SSM_GROUPS, 1, t_n * ch)
    rv_re, rv_im = pw_re[:, t_n - 1::-1][:, :t_n], pw_im[:, t_n - 1::-1][:, :t_n]
    bo_re = rv_re[:, :, None, :] * jnp.swapaxes(bb_re, 1, 2)[:, None] - rv_im[:, :, None, :] * jnp.swapaxes(bb_im, 1, 2)[:, None]
    bo_im = rv_re[:, :, None, :] * jnp.swapaxes(bb_im, 1, 2)[:, None] + rv_im[:, :, None, :] * jnp.swapaxes(bb_re, 1, 2)[:, None]
    b_op = jnp.concatenate([bo_re, bo_im], axis=-1).reshape(SSM_GROUPS, t_n * ch, 2 * p_n)
    q_re = c_re[:, None] * pw_re[:, 1:, None, :] - c_im[:, None] * pw_im[:, 1:, None, :]
    q_im = c_re[:, None] * pw_im[:, 1:, None, :] + c_im[:, None] * pw_re[:, 1:, None, :]
    c_op = jnp.concatenate([q_re, -q_im], axis=-1).reshape(SSM_GROUPS, t_n * ch, 2 * p_n)
    c_op = jnp.swapaxes(c_op, 1, 2)
    a1 = jnp.concatenate([pw_re[:, t_n], pw_re[:, t_n]], axis=-1)[:, None, :]
    a2 = jnp.concatenate([-pw_im[:, t_n], pw_im[:, t_n]], axis=-1)[:, None, :]
    return m_op, b_op, c_op, a1, a2


def _s5_groups(name, fn, ins, out_dims):
    g_n = ins[0].shape[0]
    blk = lambda a, b: pl.BlockSpec((None, a, b), lambda g: (g, 0, 0))

    def body(*refs):
        res = fn(*[r[...] for r in refs[:len(ins)]])
        for o, v in zip(refs[len(ins):], res):
            o[...] = v

    return pl.pallas_call(
        body, name=name, grid=(g_n,), in_specs=[blk(*a.shape[1:]) for a in ins], out_specs=[blk(*d) for d in out_dims],
        out_shape=[jax.ShapeDtypeStruct((g_n,) + tuple(d), F32) for d in out_dims],
        compiler_params=_params(("parallel",)),
    )(*ins)


def _s5_state_scan(z_t, a1, a2):
    nc, g_n, p2 = z_t.shape

    def body(z_ref, a1_ref, a2_ref, s_ref):
        a1v, a2v = a1_ref[...], a2_ref[...]

        def step(n, s):
            s_ref[n] = s
            return a1v * s + a2v * pltpu.roll(s, SSM_STATE, 1) + z_ref[n]

        lax.fori_loop(0, nc, step, jnp.zeros((g_n, p2), F32))

    return pl.pallas_call(body, name="s5_state_scan", out_shape=jax.ShapeDtypeStruct(z_t.shape, F32))(z_t, a1, a2)


def _s5_state_scan_bwd(ds_t, s_t, a1, a2):
    nc, g_n, p2 = s_t.shape

    def body(ds_ref, s_ref, a1_ref, a2_ref, dz_ref, da1_ref, da2_ref):
        a1v, a2v = a1_ref[...], a2_ref[...]

        def step(k, carry):
            g, d1, d2 = carry
            n = nc - 1 - k
            dz_ref[n] = g
            sn = s_ref[n]
            d1 = d1 + g * sn
            d2 = d2 + g * pltpu.roll(sn, SSM_STATE, 1)
            g = ds_ref[n] + a1v * g + pltpu.roll(a2v * g, SSM_STATE, 1)
            return g, d1, d2

        zero = jnp.zeros((g_n, p2), F32)
        _, d1, d2 = lax.fori_loop(0, nc, step, (zero, zero, zero))
        da1_ref[...] = d1
        da2_ref[...] = d2

    row = jax.ShapeDtypeStruct((g_n, p2), F32)
    return pl.pallas_call(body, name="s5_state_scan_bwd",
                          out_shape=[jax.ShapeDtypeStruct(s_t.shape, F32), row, row])(ds_t, s_t, a1, a2)


def _s5_scan_fwd(u_g, m_op, b_op, c_op, a1, a2):
    _, nc, w = u_g.shape
    p2 = 2 * SSM_STATE
    z_g, = _s5_groups("s5_chunk_inputs", lambda u, b: (_dotf(u, b),), [u_g, b_op], [(nc, p2)])
    s_g = jnp.swapaxes(_s5_state_scan(jnp.swapaxes(z_g, 0, 1), a1[:, 0], a2[:, 0]), 0, 1)
    y_g, = _s5_groups("s5_outputs", lambda u, s, m, c: (_dotf(u, m) + _dotf(s, c),), [u_g, s_g, m_op, c_op], [(nc, w)])
    return y_g, s_g


def _s5_scan_bwd(dy_g, u_g, s_g, m_op, b_op, c_op, a1, a2):
    _, nc, w = u_g.shape
    p2 = 2 * SSM_STATE
    ds_g, = _s5_groups("s5_dstate", lambda dy, c: (_dotf(dy, c, NT),), [dy_g, c_op], [(nc, p2)])
    dz_t, da1, da2 = _s5_state_scan_bwd(jnp.swapaxes(ds_g, 0, 1), jnp.swapaxes(s_g, 0, 1), a1[:, 0], a2[:, 0])
    dz_g = jnp.swapaxes(dz_t, 0, 1)

    def grads(dy, dz, u, s, m, b):
        return _dotf(dy, m, NT) + _dotf(dz, b, NT), _dotf(u, dy, TN), _dotf(u, dz, TN), _dotf(s, dy, TN)

    du, dm, db, dc = _s5_groups("s5_grads", grads, [dy_g, dz_g, u_g, s_g, m_op, b_op], [(nc, w), (w, w), (w, p2), (p2, w)])
    return du, dm, db, dc, da1[:, None], da2[:, None]


def _to_groups(u):
    s = u.shape[0]
    return u.reshape(s // SSM_T, SSM_T, SSM_GROUPS, SSM_CH).transpose(2, 0, 1, 3).reshape(
        SSM_GROUPS, s // SSM_T, SSM_T * SSM_CH)


def _from_groups(y_g):
    nc = y_g.shape[1]
    return y_g.reshape(SSM_GROUPS, nc, SSM_T, SSM_CH).transpose(1, 2, 0, 3).reshape(nc * SSM_T, SSM_WIDTH)


def _s5_post_math(y, glu_w, glu_b, gain):
    y2 = _gelu(y)
    o = y2 * _sigmoid(_dotb(y2, glu_w) + glu_b)
    return _rms(o, gain)


def _s5_post_fwd(y, glu_w, glu_b, gain):
    return _rowwise("s5_post_fwd", _s5_post_math, [y], [(SSM_WIDTH, F32)], bcast=[glu_w, glu_b, gain])[0]


def _s5_post_bwd(dout, y, glu_w, glu_b, gain):
    def fn(dv, yv, w, b, g):
        _, vjp = jax.vjp(_s5_post_math, yv, w, b, g)
        return vjp(dv)
    return _rowwise("s5_post_bwd", fn, [dout, y], [(SSM_WIDTH, F32)], bcast=[glu_w, glu_b, gain],
                    accs=[(SSM_WIDTH, SSM_WIDTH), (1, SSM_WIDTH), (1, SSM_WIDTH)])


def _t5_bucket(dist):
    max_exact = N_BUCKETS // 2
    d = jnp.maximum(dist, 1).astype(F32)
    large = max_exact + jnp.log(d / max_exact) / math.log(REL_MAX / max_exact) * (N_BUCKETS - max_exact)
    large = jnp.minimum(large.astype(jnp.int32), N_BUCKETS - 1)
    return jnp.where(dist < max_exact, dist, large)


def _attn_bias_tables(rel_bias):
    blk = AT_BLK
    tabs = []
    for window, dil in DILATED:
        rel = blk + jnp.arange(blk)[:, None] - jnp.arange(2 * blk)[None, :]
        valid = (rel >= 0) & (rel <= window // dil)
        bias = jnp.moveaxis(rel_bias[_t5_bucket(jnp.maximum(rel, 0) * dil)], -1, 0)
        tabs.append(jnp.where(valid[None], bias, NEG_INF))
    return jnp.stack(tabs)


def _attn_branch_fwd(b_idx, dil, q2, k2, v2, bias):
    sub, blk, dh = q2.shape[0], AT_BLK, AT_DIM
    nb = sub // blk
    scale = dh ** -0.5

    def body(q_ref, k_ref, v_ref, b_ref, o_ref, l_ref):
        def block(n, carry):
            cur = pl.multiple_of(n * blk, blk)
            prv = pl.multiple_of(jnp.maximum(n - 1, 0) * blk, blk)
            q = q_ref[pl.ds(cur, blk), :] * scale
            lc = _dotb(q, k_ref[pl.ds(cur, blk), :], NT) + b_ref[:, blk:]
            lp = _dotb(q, k_ref[pl.ds(prv, blk), :], NT) + b_ref[:, :blk]
            lp = jnp.where(n > 0, lp, NEG_INF)
            m = jnp.maximum(jnp.max(lc, axis=1, keepdims=True), jnp.max(lp, axis=1, keepdims=True))
            pc, pp = jnp.exp(lc - m), jnp.exp(lp - m)
            den = jnp.sum(pc, axis=1, keepdims=True) + jnp.sum(pp, axis=1, keepdims=True)
            inv = 1.0 / den
            o = _dotb(pc * inv, v_ref[pl.ds(cur, blk), :]) + _dotb(pp * inv, v_ref[pl.ds(prv, blk), :])
            o_ref[pl.ds(cur, blk), :] = o
            l_ref[pl.ds(cur, blk), :] = jnp.broadcast_to(m + jnp.log(den), (blk, dh))
            return carry

        lax.fori_loop(0, nb, block, 0)

    seq = pl.BlockSpec((sub, dh), lambda h, r: (0, r * AT_HEADS + h))
    return pl.pallas_call(
        body, name=f"attn_fwd_d{dil}", grid=(AT_HEADS, dil),
        in_specs=[seq, seq, seq, pl.BlockSpec((None, None, blk, 2 * blk), lambda h, r: (b_idx, h, 0, 0))],
        out_specs=[seq, seq],
        out_shape=[jax.ShapeDtypeStruct(q2.shape, F32), jax.ShapeDtypeStruct(q2.shape, F32)],
        compiler_params=_params(("parallel", "parallel")),
    )(q2, k2, v2, bias)


def _attn_branch_bwd(b_idx, dil, q2, k2, v2, do2, lse2, dlt2, bias):
    sub, blk, dh = q2.shape[0], AT_BLK, AT_DIM
    nb = sub // blk
    scale = dh ** -0.5

    def body(q_ref, k_ref, v_ref, do_ref, l_ref, d_ref, b_ref, dq_ref, dk_ref, dv_ref, db_ref):
        r = pl.program_id(1)
        dk_ref[...] = jnp.zeros_like(dk_ref)
        dv_ref[...] = jnp.zeros_like(dv_ref)

        @pl.when(r == 0)
        def _():
            db_ref[...] = jnp.zeros_like(db_ref)

        def block(n, carry):
            cur = pl.multiple_of(n * blk, blk)
            prv = pl.multiple_of(jnp.maximum(n - 1, 0) * blk, blk)
            q = q_ref[pl.ds(cur, blk), :] * scale
            do = do_ref[pl.ds(cur, blk), :]
            lse = l_ref[pl.ds(cur, blk), :][:, :1]
            dlt = d_ref[pl.ds(cur, blk), :][:, :1]
            kc, kp = k_ref[pl.ds(cur, blk), :], k_ref[pl.ds(prv, blk), :]
            vc, vp = v_ref[pl.ds(cur, blk), :], v_ref[pl.ds(prv, blk), :]
            lc = _dotb(q, kc, NT) + b_ref[:, blk:]
            lp = jnp.where(n > 0, _dotb(q, kp, NT) + b_ref[:, :blk], NEG_INF)
            pc, pp = jnp.exp(lc - lse), jnp.exp(lp - lse)
            dsc = pc * (_dotb(do, vc, NT) - dlt)
            dsp = pp * (_dotb(do, vp, NT) - dlt)
            dq_ref[pl.ds(cur, blk), :] = (_dotb(dsc, kc) + _dotb(dsp, kp)) * scale
            dk_ref[pl.ds(cur, blk), :] += _dotb(dsc, q, TN)
            dk_ref[pl.ds(prv, blk), :] += _dotb(dsp, q, TN)
            dv_ref[pl.ds(cur, blk), :] += _dotb(pc, do, TN)
            dv_ref[pl.ds(prv, blk), :] += _dotb(pp, do, TN)
            db_ref[:, blk:] += dsc
            db_ref[:, :blk] += dsp
            return carry

        lax.fori_loop(0, nb, block, 0)

    seq = pl.BlockSpec((sub, dh), lambda h, r: (0, r * AT_HEADS + h))
    tab = pl.BlockSpec((None, None, blk, 2 * blk), lambda h, r: (b_idx, h, 0, 0))
    return pl.pallas_call(
        body, name=f"attn_bwd_d{dil}", grid=(AT_HEADS, dil),
        in_specs=[seq, seq, seq, seq, seq, seq, tab],
        out_specs=[seq, seq, seq, pl.BlockSpec((None, blk, 2 * blk), lambda h, r: (h, 0, 0))],
        out_shape=[jax.ShapeDtypeStruct(q2.shape, F32)] * 3 + [jax.ShapeDtypeStruct((AT_HEADS, blk, 2 * blk), F32)],
        compiler_params=_params(("parallel", "arbitrary"), 56),
    )(q2, k2, v2, do2, lse2, dlt2, bias)


def _per_head(fn, *xs):
    return jnp.concatenate([fn(*[x[:, h * AT_DIM:(h + 1) * AT_DIM] for x in xs]) for h in range(AT_HEADS)], axis=1)


def _attn_merge_math(o1, o2, o3, l1, l2, l3, gain):
    m = jnp.maximum(jnp.maximum(l1, l2), l3)
    e1, e2, e3 = jnp.exp(l1 - m), jnp.exp(l2 - m), jnp.exp(l3 - m)
    den = e1 + e2 + e3
    o = (e1 * o1 + e2 * o2 + e3 * o3) / den
    return _rms(o, gain), o, m + jnp.log(den)


def _attn_merge_fwd(os_, ls_, gain):
    w = AT_WIDTH
    return _rowwise("attn_merge_fwd", _attn_merge_math, [*os_, *ls_], [(w, F32)] * 3, bcast=[gain])


def _attn_merge_bwd(dy, o, gain):
    def fn(dyv, ov, g):
        do, dg = _rms_bwd_math(dyv, ov, g)
        dlt = _per_head(lambda a, b: jnp.broadcast_to(jnp.sum(a * b, axis=1, keepdims=True), a.shape), do, ov)
        return do, dlt, dg
    return _rowwise("attn_merge_bwd", fn, [dy, o], [(AT_WIDTH, F32)] * 2, bcast=[gain], accs=[(1, AT_WIDTH)])


def _add3(name, a, b, c):
    return _rowwise(name, lambda x, y, z: x + y + z, [a, b, c], [(a.shape[1], F32)])[0]


def _conv_taps(x, w):
    row = lax.broadcasted_iota(jnp.int32, x.shape, 0)
    y = w[DN_CONV - 1:DN_CONV, :] * x
    for sh in range(1, DN_CONV):
        y = y + w[DN_CONV - 1 - sh:DN_CONV - sh, :] * jnp.where(row >= sh, pltpu.roll(x, sh, 0), 0.0)
    return y


def _gdn_prep_fwd(proj, conv_w):
    s = proj.shape[0]
    ncb = 3 * DN_HEADS
    c0 = COL_DQKV // 128

    def body(x_ref, w_ref, o_ref):
        o_ref[...] = _silu(_conv_taps(x_ref[...], w_ref[...]))

    return pl.pallas_call(
        body, name="gdn_prep_fwd", grid=(ncb,),
        in_specs=[pl.BlockSpec((s, 128), lambda c: (0, c0 + c)), pl.BlockSpec((DN_CONV, 128), lambda c: (0, c))],
        out_specs=pl.BlockSpec((None, s, 128), lambda c: (c, 0, 0)),
        out_shape=jax.ShapeDtypeStruct((ncb, s, 128), F32),
        compiler_params=_params(("parallel",)),
    )(proj, conv_w)


def _gdn_prep_bwd(dact, proj, conv_w):
    s = proj.shape[0]
    ncb = 3 * DN_HEADS
    c0 = COL_DQKV // 128

    def body(d_ref, x_ref, w_ref, dx_ref, dw_ref):
        x, w = x_ref[...], w_ref[...]
        pre = _conv_taps(x, w)
        sg = _sigmoid(pre)
        dpre = d_ref[...] * sg * (1.0 + pre * (1.0 - sg))
        row = lax.broadcasted_iota(jnp.int32, x.shape, 0)
        dx = w[DN_CONV - 1:DN_CONV, :] * dpre
        dw_ref[pl.ds(DN_CONV - 1, 1), :] = jnp.sum(dpre * x, axis=0, keepdims=True)
        for sh in range(1, DN_CONV):
            dx = dx + w[DN_CONV - 1 - sh:DN_CONV - sh, :] * jnp.where(row < s - sh, pltpu.roll(dpre, s - sh, 0), 0.0)
            dw_ref[pl.ds(DN_CONV - 1 - sh, 1), :] = jnp.sum(
                dpre * jnp.where(row >= sh, pltpu.roll(x, sh, 0), 0.0), axis=0, keepdims=True)
        dx_ref[...] = dx

    return pl.pallas_call(
        body, name="gdn_prep_bwd", grid=(ncb,),
        in_specs=[pl.BlockSpec((None, s, 128), lambda c: (c, 0, 0)), pl.BlockSpec((s, 128), lambda c: (0, c0 + c)),
                  pl.BlockSpec((DN_CONV, 128), lambda c: (0, c))],
        out_specs=[pl.BlockSpec((s, 128), lambda c: (0, c)), pl.BlockSpec((DN_CONV, 128), lambda c: (0, c))],
        out_shape=[jax.ShapeDtypeStruct((s, ncb * 128), F32), jax.ShapeDtypeStruct((DN_CONV, ncb * 128), F32)],
        compiler_params=_params(("parallel",)),
    )(dact, proj, conv_w)


def _gates_math(ab, alog, dtb):
    lane = lax.broadcasted_iota(jnp.int32, ab.shape, 1)
    g = -jnp.exp(alog) * _softplus(ab + dtb)
    return jnp.where(lane < DN_HEADS, g, jnp.where(lane < 2 * DN_HEADS, _sigmoid(ab), 0.0))


def _gates_fwd(proj, alog, dtb):
    return _rowwise("gdn_gates_fwd", _gates_math, [(proj, 128, COL_DAB // 128)], [(128, F32)], bcast=[alog, dtb])[0]


def _gates_bwd(dgates, proj, alog, dtb):
    def fn(dv, ab, a, d):
        _, vjp = jax.vjp(_gates_math, ab, a, d)
        return vjp(dv)
    return _rowwise("gdn_gates_bwd", fn, [dgates, (proj, 128, COL_DAB // 128)], [(128, F32)], bcast=[alog, dtb],
                    accs=[(1, 128), (1, 128)])


def _l2n(x):
    return x * lax.rsqrt(jnp.sum(x * x, axis=-1, keepdims=True) + NORM_EPS)


def _gdn_intra_math(q, k, v, gcol, grow, bcol):
    c = DN_CHUNK
    ii = lax.broadcasted_iota(jnp.int32, (1, c, c), 1)
    jj = lax.broadcasted_iota(jnp.int32, (1, c, c), 2)
    gc_col = jnp.sum(jnp.where(ii >= jj, grow, 0.0), axis=2, keepdims=True)
    gc_row = jnp.sum(jnp.where(ii <= jj, gcol, 0.0), axis=1, keepdims=True)
    gc_last = jnp.sum(gcol, axis=1, keepdims=True)
    decay = jnp.exp(jnp.where(ii >= jj, gc_col - gc_row, NEG_INF))
    qn = _l2n(q) * (DN_DIM ** -0.5)
    kn = _l2n(k)
    kb = kn * bcol
    a_mat = jnp.where(ii > jj, _dot3(kb, kn, BNT) * decay, 0.0)
    nil = -a_mat
    t_inv = jnp.where(ii == jj, 1.0, 0.0) + nil
    for _ in range(5):
        nil = _dot3(nil, nil, BNN)
        t_inv = t_inv + _dot3(t_inv, nil, BNN)
    e_col = jnp.exp(gc_col)
    u = _dot3(t_inv, v * bcol, BNN)
    w = _dot3(t_inv, kb * e_col, BNN)
    attn = _dot3(qn, kn, BNT) * decay
    return u, w, attn, qn * e_col, kn * jnp.exp(gc_last - gc_col), jnp.broadcast_to(jnp.exp(gc_last), (DN_HEADS, 1, 128))


def _gdn_specs(s):
    nc = s // DN_CHUNK
    h, c, d = DN_HEADS, DN_CHUNK, DN_DIM
    return dict(
        qkv=pl.BlockSpec((3 * h, c, d), lambda n: (0, n, 0)),
        hcd=pl.BlockSpec((h, c, d), lambda n: (0, n, 0)),
        col=pl.BlockSpec((h, c, 1), lambda n: (0, n, 0)),
        row=pl.BlockSpec((h, None, 1, c), lambda n: (0, n, 0, 0)),
        att=pl.BlockSpec((h, c, c), lambda n: (0, n, 0)),
        dec=pl.BlockSpec((h, None, 1, 128), lambda n: (0, n, 0, 0)),
        s_hcd=jax.ShapeDtypeStruct((h, s, d), F32), s_col=jax.ShapeDtypeStruct((h, s, 1), F32),
        s_row=jax.ShapeDtypeStruct((h, nc, 1, c), F32), s_att=jax.ShapeDtypeStruct((h, s, c), F32),
        s_dec=jax.ShapeDtypeStruct((h, nc, 1, 128), F32), s_qkv=jax.ShapeDtypeStruct((3 * h, s, d), F32),
    )


def _gdn_intra_fwd(act, gcol, grow, bcol):
    s = act.shape[1]
    sp = _gdn_specs(s)
    h = DN_HEADS

    def body(a_ref, gc_ref, gr_ref, bc_ref, u_ref, w_ref, at_ref, qd_ref, kt_ref, dec_ref):
        outs = _gdn_intra_math(a_ref[0:h], a_ref[h:2 * h], a_ref[2 * h:3 * h], gc_ref[...], gr_ref[...], bc_ref[...])
        for ref, val in zip((u_ref, w_ref, at_ref, qd_ref, kt_ref, dec_ref), outs):
            ref[...] = val

    return pl.pallas_call(
        body, name="gdn_intra_fwd", grid=(s // DN_CHUNK,),
        in_specs=[sp["qkv"], sp["col"], sp["row"], sp["col"]],
        out_specs=[sp["hcd"], sp["hcd"], sp["att"], sp["hcd"], sp["hcd"], sp["dec"]],
        out_shape=[sp["s_hcd"], sp["s_hcd"], sp["s_att"], sp["s_hcd"], sp["s_hcd"], sp["s_dec"]],
        compiler_params=_params(("parallel",)),
    )(act, gcol, grow, bcol)


def _gdn_intra_bwd(act, gcol, grow, bcol, du, dw, dattn, dqd, dkt, ddec):
    s = act.shape[1]
    sp = _gdn_specs(s)
    h = DN_HEADS

    def body(a_ref, gc_ref, gr_ref, bc_ref, du_ref, dw_ref, dat_ref, dqd_ref, dkt_ref, dde_ref,
             dact_ref, dgc_ref, dgr_ref, dbc_ref):
        _, vjp = jax.vjp(_gdn_intra_math, a_ref[0:h], a_ref[h:2 * h], a_ref[2 * h:3 * h],
                         gc_ref[...], gr_ref[...], bc_ref[...])
        dq, dk, dv, dgc, dgr, dbc = vjp((du_ref[...], dw_ref[...], dat_ref[...], dqd_ref[...], dkt_ref[...], dde_ref[...]))
        dact_ref[0:h] = dq
        dact_ref[h:2 * h] = dk
        dact_ref[2 * h:3 * h] = dv
        dgc_ref[...] = dgc
        dgr_ref[...] = dgr
        dbc_ref[...] = dbc

    return pl.pallas_call(
        body, name="gdn_intra_bwd", grid=(s // DN_CHUNK,),
        in_specs=[sp["qkv"], sp["col"], sp["row"], sp["col"], sp["hcd"], sp["hcd"], sp["att"], sp["hcd"], sp["hcd"], sp["dec"]],
        out_specs=[sp["qkv"], sp["col"], sp["row"], sp["col"]],
        out_shape=[sp["s_qkv"], sp["s_col"], sp["s_row"], sp["s_col"]],
        compiler_params=_params(("parallel",)),
    )(act, gcol, grow, bcol, du, dw, dattn, dqd, dkt, ddec)


def _gdn_step_math(state, u, w, attn, qd, kt, dec):
    v_new = u - _dot3(w, state, BNN)
    o = _dot3(qd, state, BNN) + _dot3(attn, v_new, BNN)
    return state * dec[:, :, :1] + _dot3(kt, v_new, BTN), o


def _gdn_scan_fwd(u, w, attn, qd, kt, dec):
    s = u.shape[1]
    nc = s // DN_CHUNK
    sp = _gdn_specs(s)
    h, d = DN_HEADS, DN_DIM

    def body(u_ref, w_ref, at_ref, qd_ref, kt_ref, dec_ref, o_ref, st_ref, state):
        @pl.when(pl.program_id(0) == 0)
        def _():
            state[...] = jnp.zeros_like(state)

        st_ref[...] = state[...]
        new, o = _gdn_step_math(state[...], u_ref[...], w_ref[...], at_ref[...], qd_ref[...], kt_ref[...], dec_ref[...])
        state[...] = new
        o_ref[...] = o

    return pl.pallas_call(
        body, name="gdn_scan_fwd", grid=(nc,),
        in_specs=[sp["hcd"], sp["hcd"], sp["att"], sp["hcd"], sp["hcd"], sp["dec"]],
        out_specs=[sp["hcd"], pl.BlockSpec((None, h, d, d), lambda n: (n, 0, 0, 0))],
        out_shape=[sp["s_hcd"], jax.ShapeDtypeStruct((nc, h, d, d), F32)],
        scratch_shapes=[pltpu.VMEM((h, d, d), F32)],
        compiler_params=_params(("arbitrary",)),
    )(u, w, attn, qd, kt, dec)


def _gdn_scan_bwd(do, states, u, w, attn, qd, kt, dec):
    s = u.shape[1]
    nc = s // DN_CHUNK
    h, c, d = DN_HEADS, DN_CHUNK, DN_DIM
    rev = lambda n: nc - 1 - n
    hcd = pl.BlockSpec((h, c, d), lambda n: (0, rev(n), 0))
    att = pl.BlockSpec((h, c, c), lambda n: (0, rev(n), 0))
    dec_s = pl.BlockSpec((h, None, 1, 128), lambda n: (0, rev(n), 0, 0))
    sp = _gdn_specs(s)

    def body(do_ref, st_ref, u_ref, w_ref, at_ref, qd_ref, kt_ref, dec_ref,
             du_ref, dw_ref, dat_ref, dqd_ref, dkt_ref, dde_ref, dstate):
        @pl.when(pl.program_id(0) == 0)
        def _():
            dstate[...] = jnp.zeros_like(dstate)

        _, vjp = jax.vjp(_gdn_step_math, st_ref[...], u_ref[...], w_ref[...], at_ref[...], qd_ref[...], kt_ref[...],
                         dec_ref[...])
        dst, du, dw, dat, dqd, dkt, dde = vjp((dstate[...], do_ref[...]))
        dstate[...] = dst
        for ref, val in zip((du_ref, dw_ref, dat_ref, dqd_ref, dkt_ref, dde_ref), (du, dw, dat, dqd, dkt, dde)):
            ref[...] = val

    return pl.pallas_call(
        body, name="gdn_scan_bwd", grid=(nc,),
        in_specs=[hcd, pl.BlockSpec((None, h, d, d), lambda n: (rev(n), 0, 0, 0)), hcd, hcd, att, hcd, hcd, dec_s],
        out_specs=[hcd, hcd, att, hcd, hcd, dec_s],
        out_shape=[sp["s_hcd"], sp["s_hcd"], sp["s_att"], sp["s_hcd"], sp["s_hcd"], sp["s_dec"]],
        scratch_shapes=[pltpu.VMEM((h, d, d), F32)],
        compiler_params=_params(("arbitrary",)),
    )(do, states, u, w, attn, qd, kt, dec)


def _gdn_out_math(o, z, gain):
    return _rms(o, gain) * _silu(z)


def _gdn_out_fwd(o_rows, z_rows, gain):
    return _rowwise("gdn_out_fwd", _gdn_out_math, [o_rows, z_rows], [(DN_DIM, F32)], bcast=[gain])[0]


def _gdn_out_bwd(dy_rows, o_rows, z_rows, gain):
    def fn(dv, ov, zv, g):
        _, vjp = jax.vjp(_gdn_out_math, ov, zv, g)
        return vjp(dv)
    return _rowwise("gdn_out_bwd", fn, [dy_rows, o_rows, z_rows], [(DN_DIM, F32)] * 2, bcast=[gain], accs=[(1, DN_DIM)])


def _heads_major(x):
    s = x.shape[0]
    return x.reshape(s, -1, DN_DIM).transpose(1, 0, 2)


def _heads_minor(x):
    return x.transpose(1, 0, 2).reshape(x.shape[1], -1)


def _pad_row(v, width=128):
    return jnp.pad(v.reshape(1, -1), ((0, 0), (0, width - v.size)))


def _ffn_fwd(tag, x, gains_in, gains_out, wg, wu, wd, widx):
    h = _rms_fwd(f"{tag}_prenorm", x, gains_in)
    a = _mm_cols(f"{tag}_gate", h, wg, widx, BF16)
    b = _mm_cols(f"{tag}_up", h, wu, widx, BF16)
    s = _swiglu_fwd(f"{tag}_swiglu", a, b)
    f = _mm_rows(f"{tag}_down", s, wd, widx, F32)
    x_new = _rms_residual(f"{tag}_postnorm", x, f, gains_out, 0.5)
    return x_new, (x, h, a, b, s, f)


def _ffn_bwd(tag, dx_new, saved, gains_in, gains_out, wg, wu, wd, widx):
    x, h, a, b, s, f = saved
    df, dg_out = _rms_bwd(f"{tag}_postnorm_bwd", dx_new, f, gains_out, 0.5, out_dtype=BF16)
    ds = _mm_rows_t(f"{tag}_down_bwd", df, wd, widx, BF16)
    d_wd = _mm_grad(f"{tag}_down_grad", s, df, wd.shape[0], None)
    da, db = _swiglu_bwd(f"{tag}_swiglu_bwd", ds, a, b)
    d_wg = _mm_grad(f"{tag}_gate_grad", h, da, None, wg.shape[0])
    d_wu = _mm_grad(f"{tag}_up_grad", h, db, None, wu.shape[0])
    dh = _mm_cols_t(f"{tag}_gateup_bwd", [(da, wg), (db, wu)], widx, F32)
    dx, dg_in = _rms_bwd(f"{tag}_prenorm_bwd", dh, x, gains_in, 1.0, residual=dx_new)
    return dx, dg_in, dg_out, d_wg, d_wu, d_wd


def _attn_views(t, dil):
    return t.reshape(t.shape[0] // dil, dil * AT_WIDTH)


def _mixer_fwd(l, h, w, p, bias_tabs):
    s = h.shape[0]
    nc = s // DN_CHUNK
    proj = _mm_cols("w_in_fwd", h, w["w_in"], (l,), F32)
    ops, ops_vjp = jax.vjp(_s5_operators, p["ssm_lambda_re"][l], p["ssm_lambda_im"][l], p["ssm_b_re"][l], p["ssm_b_im"][l],
                           p["ssm_c_re"][l], p["ssm_c_im"][l], p["ssm_d"][l], p["ssm_log_dt"][l])
    u_g = _to_groups(proj[:, :SSM_WIDTH])
    y_g, s_g = _s5_scan_fwd(u_g, *ops)
    y = _from_groups(y_g)
    glu_w, glu_b, gain_ssm = p["ssm_glu_w"][l], p["ssm_glu_b"][l][None], p["ssm_out_gain"][l][None]
    y_ssm = _s5_post_fwd(y, glu_w, glu_b, gain_ssm)
    conv_w, alog, dtb = p["dn_conv_w"][l], _pad_row(p["dn_a_log"][l]), _pad_row(p["dn_dt_bias"][l])
    gain_dn = p["dn_norm_gain"][l][None]
    act = _gdn_prep_fwd(proj, conv_w)
    gates_t = _gates_fwd(proj, alog, dtb)[:, :2 * DN_HEADS].T
    gcol, bcol = gates_t[:DN_HEADS, :, None], gates_t[DN_HEADS:, :, None]
    grow = gates_t[:DN_HEADS].reshape(DN_HEADS, nc, 1, DN_CHUNK)
    u, wy, attn, qd, kt, dec = _gdn_intra_fwd(act, gcol, grow, bcol)
    o_dn, states = _gdn_scan_fwd(u, wy, attn, qd, kt, dec)
    o_rows = o_dn.reshape(DN_HEADS * s, DN_DIM)
    z_rows = _heads_major(proj[:, COL_DZ:COL_DAB]).reshape(DN_HEADS * s, DN_DIM)
    y_dn = _heads_minor(_gdn_out_fwd(o_rows, z_rows, gain_dn).reshape(DN_HEADS, s, DN_DIM))
    q, k, v = proj[:, COL_AQ:COL_AK], proj[:, COL_AK:COL_AV], proj[:, COL_AV:COL_DQKV]
    outs, lses = [], []
    for bi, (_, dil) in enumerate(DILATED):
        o_b, l_b = _attn_branch_fwd(bi, dil, _attn_views(q, dil), _attn_views(k, dil), _attn_views(v, dil), bias_tabs)
        outs.append(o_b.reshape(s, AT_WIDTH))
        lses.append(l_b.reshape(s, AT_WIDTH))
    gain_at = p["attn_out_gain"][l][None]
    y_at, o_at, lse = _attn_merge_fwd(outs, lses, gain_at)
    mix = jnp.concatenate([y_ssm, y_dn, y_at], axis=1).astype(BF16)
    out = _mm_rows("w_out_fwd", mix, w["w_out"], (l,), F32)
    saved = dict(h=h, proj=proj, ops=ops, ops_vjp=ops_vjp, u_g=u_g, s_g=s_g, y=y, act=act, gcol=gcol, grow=grow, bcol=bcol,
                 u=u, wy=wy, attn=attn, qd=qd, kt=kt, dec=dec, states=states, o_rows=o_rows, z_rows=z_rows,
                 q=q, k=k, v=v, o_at=o_at, lse=lse, mix=mix)
    return out, saved


def _mixer_bwd(l, dout, sv, w, p, bias_tabs):
    s = dout.shape[0]
    proj = sv["proj"]
    g = {}
    g["w_out"] = _mm_grad("w_out_grad", sv["mix"], dout, w["w_out"].shape[0], None)
    dmix = _mm_rows_t("w_out_bwd", dout, w["w_out"], (l,), F32)
    d_ssm, d_dn, d_at = dmix[:, :SSM_WIDTH], dmix[:, SSM_WIDTH:SSM_WIDTH + DN_WIDTH], dmix[:, SSM_WIDTH + DN_WIDTH:]
    glu_w, glu_b, gain_ssm = p["ssm_glu_w"][l], p["ssm_glu_b"][l][None], p["ssm_out_gain"][l][None]
    dy, g["ssm_glu_w"], dglu_b, dgain_ssm = _s5_post_bwd(d_ssm, sv["y"], glu_w, glu_b, gain_ssm)
    g["ssm_glu_b"], g["ssm_out_gain"] = dglu_b[0], dgain_ssm[0]
    du_g, *d_ops = _s5_scan_bwd(_to_groups(dy), sv["u_g"], sv["s_g"], *sv["ops"])
    (g["ssm_lambda_re"], g["ssm_lambda_im"], g["ssm_b_re"], g["ssm_b_im"], g["ssm_c_re"], g["ssm_c_im"], g["ssm_d"],
     g["ssm_log_dt"]) = sv["ops_vjp"](tuple(d_ops))
    d_u = _from_groups(du_g)
    gain_at = p["attn_out_gain"][l][None]
    do, dlt, dgain_at = _attn_merge_bwd(d_at, sv["o_at"], gain_at)
    g["attn_out_gain"] = dgain_at[0]
    dqs, dks, dvs, dbs = [], [], [], []
    for bi, (_, dil) in enumerate(DILATED):
        vw = lambda t: _attn_views(t, dil)
        dq2, dk2, dv2, db = _attn_branch_bwd(bi, dil, vw(sv["q"]), vw(sv["k"]), vw(sv["v"]), vw(do), vw(sv["lse"]), vw(dlt),
                                            bias_tabs)
        dqs.append(dq2.reshape(s, AT_WIDTH))
        dks.append(dk2.reshape(s, AT_WIDTH))
        dvs.append(dv2.reshape(s, AT_WIDTH))
        dbs.append(db)
    dq, dk, dv = _add3("attn_dq_sum", *dqs), _add3("attn_dk_sum", *dks), _add3("attn_dv_sum", *dvs)
    g["bias_tabs"] = jnp.stack(dbs)
    conv_w, alog, dtb = p["dn_conv_w"][l], _pad_row(p["dn_a_log"][l]), _pad_row(p["dn_dt_bias"][l])
    gain_dn = p["dn_norm_gain"][l][None]
    dy_rows = _heads_major(d_dn).reshape(DN_HEADS * s, DN_DIM)
    do_rows, dz_rows, dgain_dn = _gdn_out_bwd(dy_rows, sv["o_rows"], sv["z_rows"], gain_dn)
    g["dn_norm_gain"] = dgain_dn[0]
    d_scan = _gdn_scan_bwd(do_rows.reshape(DN_HEADS, s, DN_DIM), sv["states"], sv["u"], sv["wy"], sv["attn"], sv["qd"],
                           sv["kt"], sv["dec"])
    dact, dgc, dgr, dbc = _gdn_intra_bwd(sv["act"], sv["gcol"], sv["grow"], sv["bcol"], *d_scan)
    dgates_t = jnp.concatenate([dgc[..., 0] + dgr.reshape(DN_HEADS, s), dbc[..., 0]], axis=0)
    dgates = jnp.pad(dgates_t.T, ((0, 0), (0, 128 - 2 * DN_HEADS)))
    dab, dalog, ddtb = _gates_bwd(dgates, proj, alog, dtb)
    g["dn_a_log"], g["dn_dt_bias"] = dalog[0, :DN_HEADS], ddtb[0, :DN_HEADS]
    dqkv, g["dn_conv_w"] = _gdn_prep_bwd(dact, proj, conv_w)
    dz = _heads_minor(dz_rows.reshape(DN_HEADS, s, DN_DIM))
    dproj = jnp.concatenate([d_u, dq, dk, dv, dqkv, dz, dab, jnp.zeros((s, N_IN_PAD - COL_DAB - 128), F32)], axis=1)
    g["w_in"] = _mm_grad("w_in_grad", sv["h"], dproj, None, 1)[0]
    dh = _mm_cols_t("w_in_bwd", [(dproj, w["w_in"])], (l,), F32)
    return dh, g


def _local_step(x, target, w, p):
    depth = p["norm_gains"].shape[0]
    gains = p["norm_gains"]
    bias_tabs, bias_vjp = jax.vjp(_attn_bias_tables, p["rel_bias"])
    saved = []
    for l in range(depth):
        gn = lambda i: gains[l, i][None]
        x, sv1 = _ffn_fwd("ffn1", x, gn(0), gn(1), w["ffn_w_gate"], w["ffn_w_up"], w["ffn_w_down"], (l, 0))
        h = _rms_fwd("mix_prenorm", x, gn(2))
        out, svm = _mixer_fwd(l, h, w, p, bias_tabs)
        x_mid = x
        x = _rms_residual("mix_postnorm", x, out, gn(3), 1.0)
        x, sv2 = _ffn_fwd("ffn2", x, gn(4), gn(5), w["ffn_w_gate"], w["ffn_w_up"], w["ffn_w_down"], (l, 1))
        saved.append((sv1, svm, x_mid, out, sv2))
    loss, dx = _loss_and_grad(x, target)

    small = ["ssm_lambda_re", "ssm_lambda_im", "ssm_b_re", "ssm_b_im", "ssm_c_re", "ssm_c_im", "ssm_d", "ssm_log_dt",
             "ssm_glu_w", "ssm_glu_b", "ssm_out_gain", "dn_conv_w", "dn_a_log", "dn_dt_bias", "dn_norm_gain", "attn_out_gain"]
    per_layer = {n: [None] * depth for n in small + ["norm_gains", "w_in", "w_out", "ffn_w_gate", "ffn_w_up", "ffn_w_down"]}
    d_tabs = None
    for l in reversed(range(depth)):
        gn = lambda i: gains[l, i][None]
        sv1, svm, x_mid, out, sv2 = saved[l]
        dx, dg4, dg5, wg2, wu2, wd2 = _ffn_bwd("ffn2", dx, sv2, gn(4), gn(5), w["ffn_w_gate"], w["ffn_w_up"],
                                               w["ffn_w_down"], (l, 1))
        dout, dg3 = _rms_bwd("mix_postnorm_bwd", dx, out, gn(3), 1.0)
        dh, gm = _mixer_bwd(l, dout, svm, w, p, bias_tabs)
        dx, dg2 = _rms_bwd("mix_prenorm_bwd", dh, x_mid, gn(2), 1.0, residual=dx)
        dx, dg0, dg1, wg1, wu1, wd1 = _ffn_bwd("ffn1", dx, sv1, gn(0), gn(1), w["ffn_w_gate"], w["ffn_w_up"],
                                               w["ffn_w_down"], (l, 0))
        per_layer["norm_gains"][l] = jnp.concatenate([dg0, dg1, dg2, dg3, dg4, dg5], axis=0)
        per_layer["ffn_w_gate"][l] = jnp.stack([wg1, wg2], axis=1)
        per_layer["ffn_w_up"][l] = jnp.stack([wu1, wu2], axis=1)
        per_layer["ffn_w_down"][l] = jnp.stack([wd1, wd2], axis=1)
        d_tabs = gm["bias_tabs"] if d_tabs is None else d_tabs + gm["bias_tabs"]
        for n in small + ["w_in", "w_out"]:
            per_layer[n][l] = gm[n]
    grads = {n: jnp.stack(per_layer[n], axis=0) for n in small + ["norm_gains", "w_in"]}
    for n in ("w_out", "ffn_w_gate", "ffn_w_up", "ffn_w_down"):
        grads[n] = jnp.stack(per_layer[n], axis=1)
    grads["rel_bias"] = bias_vjp(d_tabs)[0]
    return loss, dx, grads


_ANY = pl.BlockSpec(memory_space=pl.ANY)


def _place():
    return lax.axis_index("x"), lax.axis_index("y"), lax.axis_index("c")


DMA_CHUNKS = 16


class _Transfer:
    def __init__(self, make, src, dst):
        self.make, self.src, self.dst = make, src, dst

    def start(self):
        rows = self.src.shape[0]
        k = DMA_CHUNKS
        while k > 1 and rows % (16 * k):
            k //= 2
        step = rows // k
        for i in range(k):
            self.make(self.src.at[pl.ds(i * step, step)], self.dst.at[pl.ds(i * step, step)]).start()

    def whole(self):
        return self.make(self.src, self.dst)


def _all_gather_xy(name, shard):
    def body(src, out, send_sems, recv_sems, local_sem):
        x, y, c = _place()
        sib = (x, y, 1 - c)
        chips = [(1 - x, y), (x, 1 - y), (1 - x, 1 - y)]
        blk = lambda cx, cy: 2 * cx + cy

        def copy(k, src_ref, j, half, to):
            make = lambda s, d: pltpu.make_async_remote_copy(src_ref=s, dst_ref=d, send_sem=send_sems.at[k],
                                                             recv_sem=recv_sems.at[k], device_id=to, device_id_type=MESH)
            return _Transfer(make, src_ref, out.at[j, half])

        local = lambda s, d: pltpu.make_async_copy(s, d, local_sem)
        for half in range(2):
            _Transfer(local, src.at[half], out.at[blk(x, y), half]).start()
        first = [copy(k, src.at[c], blk(x, y), c, (*chip, c)) for k, chip in enumerate(chips)]
        for cp in first:
            cp.start()
        passed = [copy(3 + k, out.at[blk(*chip), c], blk(*chip), c, sib) for k, chip in enumerate(chips)]
        for k, chip in enumerate(chips):
            copy(k, src.at[c], blk(*chip), c, (*chip, c)).whole().wait_recv()
            passed[k].start()
        for k, chip in enumerate(chips):
            copy(3 + k, src.at[c], blk(*chip), 1 - c, sib).whole().wait_recv()
        for cp in first + passed:
            cp.whole().wait_send()
        local(src, out.at[blk(x, y)]).wait()

    return pl.pallas_call(
        body, name=name, in_specs=[_ANY], out_specs=_ANY,
        out_shape=jax.ShapeDtypeStruct((4,) + shard.shape, shard.dtype),
        scratch_shapes=[pltpu.SemaphoreType.DMA((6,)), pltpu.SemaphoreType.DMA((6,)), pltpu.SemaphoreType.DMA],
    )(shard)


def _exchange8(name, src, same_to_all=False):
    blk_shape = src.shape if same_to_all else src.shape[1:]

    def body(src_ref, dst, send_sems, recv_sems, local_sem):
        x, y, c = _place()
        me = 4 * x + 2 * y + c
        part = (lambda i: src_ref) if same_to_all else (lambda i: src_ref.at[i])

        def peer(k):
            return (1 - x if k & 4 else x, 1 - y if k & 2 else y, 1 - c if k & 1 else c)

        def copy(k, dst_block):
            px, py, pc = peer(k)
            make = lambda s, d: pltpu.make_async_remote_copy(src_ref=s, dst_ref=d, send_sem=send_sems.at[k - 1],
                                                             recv_sem=recv_sems.at[k - 1], device_id=(px, py, pc),
                                                             device_id_type=MESH)
            return _Transfer(make, part(4 * px + 2 * py + pc), dst.at[dst_block])

        mine = _Transfer(lambda s, d: pltpu.make_async_copy(s, d, local_sem), part(me), dst.at[me])
        mine.start()
        sends = [copy(k, me) for k in range(1, 8)]
        for cp in sends:
            cp.start()
        for k in range(1, 8):
            px, py, pc = peer(k)
            copy(k, 4 * px + 2 * py + pc).whole().wait_recv()
        for cp in sends:
            cp.whole().wait_send()
        mine.whole().wait()

    return pl.pallas_call(
        body, name=name, in_specs=[_ANY], out_specs=_ANY,
        out_shape=jax.ShapeDtypeStruct((8,) + blk_shape, src.dtype),
        scratch_shapes=[pltpu.SemaphoreType.DMA((7,)), pltpu.SemaphoreType.DMA((7,)), pltpu.SemaphoreType.DMA],
    )(src)


def _pair_swap(name, half):
    def body(src, out, send_sem, recv_sem, local_sem):
        x, y, c = _place()
        remote = lambda s, d: pltpu.make_async_remote_copy(src_ref=s, dst_ref=d, send_sem=send_sem, recv_sem=recv_sem,
                                                           device_id=(x, y, 1 - c), device_id_type=MESH)
        mine = _Transfer(lambda s, d: pltpu.make_async_copy(s, d, local_sem), src, out.at[c])
        mine.start()
        push = _Transfer(remote, src, out.at[c])
        push.start()
        remote(src, out.at[1 - c]).wait_recv()
        push.whole().wait_send()
        mine.whole().wait()

    return pl.pallas_call(
        body, name=name, in_specs=[_ANY], out_specs=_ANY,
        out_shape=jax.ShapeDtypeStruct((2,) + half.shape, half.dtype),
        scratch_shapes=[pltpu.SemaphoreType.DMA, pltpu.SemaphoreType.DMA, pltpu.SemaphoreType.DMA],
    )(half)


def _sum8(name, parts):
    _, r, c = parts.shape
    tm = min(_row_tile(r), 256)

    def body(p_ref, o_ref):
        acc = p_ref[0].astype(F32)
        for i in range(1, 8):
            acc = acc + p_ref[i].astype(F32)
        o_ref[...] = acc

    return pl.pallas_call(
        body, name=name, grid=(r // tm,), in_specs=[pl.BlockSpec((8, tm, c), lambda i: (0, i, 0))],
        out_specs=pl.BlockSpec((tm, c), lambda i: (i, 0)), out_shape=jax.ShapeDtypeStruct((r, c), F32),
        compiler_params=_params(("parallel",), 48),
    )(parts)


def _cast_bf16(name, w2d):
    return _rowwise(name, lambda v: v, [w2d], [(w2d.shape[1], BF16)])[0]


def _adamw(name, g, w, m, v):
    def fn(gv, wv, mv, vv):
        m2 = ADAM_B1 * mv + (1.0 - ADAM_B1) * gv
        v2 = ADAM_B2 * vv + (1.0 - ADAM_B2) * (gv * gv)
        m_hat = m2 / (1.0 - ADAM_B1 ** ADAM_STEP)
        v_hat = v2 / (1.0 - ADAM_B2 ** ADAM_STEP)
        return -ADAM_LR * (m_hat / (jnp.sqrt(v_hat) + ADAM_EPS) + ADAM_WD * wv), m2, v2
    return _rowwise(name, fn, [g, w, m, v], [(g.shape[1], F32)] * 3)


def _pack(tensors):
    flat = jnp.concatenate([t.reshape(-1).astype(F32) for t in tensors])
    rows = -(-flat.size // (128 * 512)) * 512
    return jnp.pad(flat, (0, rows * 128 - flat.size)).reshape(rows, 128)


def _unpack(buf, shapes):
    flat, out, at = buf.reshape(-1), [], 0
    for sh in shapes:
        n = math.prod(sh)
        out.append(flat[at:at + n].reshape(sh))
        at += n
    return out


_WEIGHTS = ("norm_gains", "ffn_w_gate", "ffn_w_up", "ffn_w_down", "w_in", "w_out", "ssm_lambda_re", "ssm_lambda_im",
            "ssm_b_re", "ssm_b_im", "ssm_c_re", "ssm_c_im", "ssm_d", "ssm_log_dt", "ssm_glu_w", "ssm_glu_b",
            "ssm_out_gain", "dn_conv_w", "dn_a_log", "dn_dt_bias", "dn_norm_gain", "attn_out_gain", "rel_bias")
_MATRICES = ("ffn_w_gate", "ffn_w_up", "ffn_w_down", "w_in", "w_out")
_CUT_SMALL = {"norm_gains": 2, "ssm_glu_w": 1, "dn_conv_w": 2}
_REPLICATED = tuple(n for n in _WEIGHTS if n not in _MATRICES and n not in _CUT_SMALL)


def _gather_matrix(name, shard):
    c = shard.shape[-1]
    half = _cast_bf16(f"{name}_cast", shard.reshape(-1, c)).reshape(2, -1, c)
    return _all_gather_xy(f"{name}_gather", half).reshape((4,) + shard.shape)


def _reduce_matrix(name, g, shard_shape):
    c = g.shape[-1]
    parts = _exchange8(f"{name}_exchange", g.reshape(8, -1, c))
    half = _sum8(f"{name}_sum", parts)
    return _pair_swap(f"{name}_swap", half).reshape(shard_shape)


def kernel(x, norm_gains, ffn_w_gate, ffn_w_up, ffn_w_down, w_in, w_out, ssm_lambda_re, ssm_lambda_im, ssm_b_re, ssm_b_im, ssm_c_re, ssm_c_im, ssm_d, ssm_log_dt, ssm_glu_w, ssm_glu_b, ssm_out_gain, dn_conv_w, dn_a_log, dn_dt_bias, dn_norm_gain, attn_out_gain, rel_bias, loss_target, m_norm_gains, m_ffn_w_gate, m_ffn_w_up, m_ffn_w_down, m_w_in, m_w_out, m_ssm_lambda_re, m_ssm_lambda_im, m_ssm_b_re, m_ssm_b_im, m_ssm_c_re, m_ssm_c_im, m_ssm_d, m_ssm_log_dt, m_ssm_glu_w, m_ssm_glu_b, m_ssm_out_gain, m_dn_conv_w, m_dn_a_log, m_dn_dt_bias, m_dn_norm_gain, m_attn_out_gain, m_rel_bias, v_norm_gains, v_ffn_w_gate, v_ffn_w_up, v_ffn_w_down, v_w_in, v_w_out, v_ssm_lambda_re, v_ssm_lambda_im, v_ssm_b_re, v_ssm_b_im, v_ssm_c_re, v_ssm_c_im, v_ssm_d, v_ssm_log_dt, v_ssm_glu_w, v_ssm_glu_b, v_ssm_out_gain, v_dn_conv_w, v_dn_a_log, v_dn_dt_bias, v_dn_norm_gain, v_attn_out_gain, v_rel_bias):
    wts = dict(zip(_WEIGHTS, (norm_gains, ffn_w_gate, ffn_w_up, ffn_w_down, w_in, w_out, ssm_lambda_re, ssm_lambda_im, ssm_b_re, ssm_b_im, ssm_c_re, ssm_c_im, ssm_d, ssm_log_dt, ssm_glu_w, ssm_glu_b, ssm_out_gain, dn_conv_w, dn_a_log, dn_dt_bias, dn_norm_gain, attn_out_gain, rel_bias)))
    mom = dict(zip(_WEIGHTS, (m_norm_gains, m_ffn_w_gate, m_ffn_w_up, m_ffn_w_down, m_w_in, m_w_out, m_ssm_lambda_re, m_ssm_lambda_im, m_ssm_b_re, m_ssm_b_im, m_ssm_c_re, m_ssm_c_im, m_ssm_d, m_ssm_log_dt, m_ssm_glu_w, m_ssm_glu_b, m_ssm_out_gain, m_dn_conv_w, m_dn_a_log, m_dn_dt_bias, m_dn_norm_gain, m_attn_out_gain, m_rel_bias)))
    var = dict(zip(_WEIGHTS, (v_norm_gains, v_ffn_w_gate, v_ffn_w_up, v_ffn_w_down, v_w_in, v_w_out, v_ssm_lambda_re, v_ssm_lambda_im, v_ssm_b_re, v_ssm_b_im, v_ssm_c_re, v_ssm_c_im, v_ssm_d, v_ssm_log_dt, v_ssm_glu_w, v_ssm_glu_b, v_ssm_out_gain, v_dn_conv_w, v_dn_a_log, v_dn_dt_bias, v_dn_norm_gain, v_attn_out_gain, v_rel_bias)))
    depth, d_model = norm_gains.shape[0], x.shape[-1]
    chip = 2 * lax.axis_index("x") + lax.axis_index("y")

    w = {n: _gather_matrix(n, wts[n]) for n in ("ffn_w_gate", "ffn_w_up", "ffn_w_down", "w_out")}
    w_in_all = _gather_matrix("w_in", w_in)
    w_in_all = jnp.transpose(w_in_all, (1, 2, 0, 3)).reshape(depth, d_model, N_IN)
    w["w_in"] = jnp.pad(w_in_all, ((0, 0), (0, 0), (0, N_IN_PAD - N_IN)))[None]
    cut_names = tuple(_CUT_SMALL)
    cut_pack = _pack([wts[n] for n in cut_names])
    cut_all = _all_gather_xy("small_gather", cut_pack.reshape(2, -1, 128)).reshape(4, -1, 128)
    p = {n: wts[n] for n in _REPLICATED}
    per_chip = [_unpack(cut_all[j], [wts[n].shape for n in cut_names]) for j in range(4)]
    for i, n in enumerate(cut_names):
        p[n] = jnp.concatenate([per_chip[j][i] for j in range(4)], axis=_CUT_SMALL[n])

    loss, dx, grads = _local_step(x[0], loss_target[0], w, p)
    loss = lax.psum(loss, ("x", "y", "c"))

    total = {}
    for n in ("ffn_w_gate", "ffn_w_up", "ffn_w_down", "w_out"):
        total[n] = _reduce_matrix(n, grads[n], wts[n].shape)
    g_in = grads["w_in"][:, :, :N_IN].reshape(depth, d_model, 4, N_IN // 4)
    total["w_in"] = _reduce_matrix("w_in", jnp.transpose(g_in, (2, 0, 1, 3)), w_in.shape)
    small_names = _REPLICATED + cut_names
    small_sum = _sum8("small_sum", _exchange8("small_exchange", _pack([grads[n] for n in small_names]), same_to_all=True))
    for n, g in zip(small_names, _unpack(small_sum, [grads[n].shape for n in small_names])):
        if n in _CUT_SMALL:
            ax = _CUT_SMALL[n]
            g = lax.dynamic_slice_in_dim(g, chip * wts[n].shape[ax], wts[n].shape[ax], axis=ax)
        total[n] = g

    delta, new_m, new_v = {}, {}, {}
    for n in _MATRICES + cut_names:
        c = wts[n].shape[-1]
        d2, m2, v2 = _adamw(f"{n}_adamw", total[n].reshape(-1, c), wts[n].reshape(-1, c), mom[n].reshape(-1, c),
                            var[n].reshape(-1, c))
        delta[n], new_m[n], new_v[n] = (t.reshape(wts[n].shape) for t in (d2, m2, v2))
    rep_shapes = [wts[n].shape for n in _REPLICATED]
    packed = _adamw("small_adamw", _pack([total[n] for n in _REPLICATED]), _pack([wts[n] for n in _REPLICATED]),
                    _pack([mom[n] for n in _REPLICATED]), _pack([var[n] for n in _REPLICATED]))
    for dst, buf in zip((delta, new_m, new_v), packed):
        dst.update(zip(_REPLICATED, _unpack(buf, rep_shapes)))

    return (loss, dx[None], *[total[n] for n in _WEIGHTS], *[delta[n] for n in _WEIGHTS],
            *[new_m[n] for n in _WEIGHTS], *[new_v[n] for n in _WEIGHTS])
```

```python
import functools
import math

import jax
import jax.numpy as jnp
from jax import lax
from jax.experimental import pallas as pl
from jax.experimental.pallas import tpu as pltpu

F32 = jnp.float32
BF16 = jnp.bfloat16
HI = lax.Precision.HIGHEST
MESH = pl.DeviceIdType.MESH

NORM_EPS = 1e-6
NEG_INF = -1e30
SSM_GROUPS, SSM_CH, SSM_STATE, SSM_WIDTH = 32, 16, 64, 512
SSM_T = 16
DN_HEADS, DN_DIM, DN_WIDTH, DN_CONV, DN_CHUNK = 6, 128, 768, 4, 64
AT_HEADS, AT_DIM, AT_WIDTH, AT_BLK = 6, 128, 768, 128
DILATED = ((128, 1), (512, 4), (2048, 16))
N_BUCKETS, REL_MAX = 32, 2048
N_IN = SSM_WIDTH + 3 * AT_WIDTH + 4 * DN_WIDTH + 2 * DN_HEADS
N_IN_PAD = 6144
COL_AQ, COL_AK, COL_AV, COL_DQKV, COL_DZ, COL_DAB = 512, 1280, 2048, 2816, 5120, 5888
ADAM_LR, ADAM_B1, ADAM_B2, ADAM_EPS, ADAM_WD, ADAM_STEP = 0.001, 0.9, 0.999, 1e-08, 0.01, 10
V7X_VMEM_BYTES = 64 * 1024 * 1024
NN = (((1,), (0,)), ((), ()))
NT = (((1,), (1,)), ((), ()))
TN = (((0,), (0,)), ((), ()))
BNN = (((2,), (1,)), ((0,), (0,)))
BNT = (((2,), (2,)), ((0,), (0,)))
BTN = (((1,), (1,)), ((0,), (0,)))


def _params(sem, vmem_mb=None):
    kw = {}
    if vmem_mb is not None:
        kw["vmem_limit_bytes"] = min(vmem_mb * 1024 * 1024, V7X_VMEM_BYTES - 8 * 1024 * 1024)
    return pltpu.CompilerParams(dimension_semantics=sem, **kw)


def _dotf(a, b, dims=NN):
    return lax.dot_general(a, b, dims, precision=HI, preferred_element_type=F32)


def _dot3(a, b, dims=NN):
    return lax.dot_general(a, b, dims, precision=lax.Precision.HIGH, preferred_element_type=F32)


def _dotb(a, b, dims=NN):
    return lax.dot_general(a.astype(BF16), b.astype(BF16), dims, preferred_element_type=F32)


def _sigmoid(x):
    return 1.0 / (1.0 + jnp.exp(-x))


def _silu(x):
    return x * _sigmoid(x)


def _softplus(x):
    return jnp.maximum(x, 0.0) + jnp.log(1.0 + jnp.exp(-jnp.abs(x)))


def _gelu(x):
    return 0.5 * x * (1.0 + jnp.tanh(math.sqrt(2.0 / math.pi) * (x + 0.044715 * x * x * x)))


def _rms(x, gain):
    return x * lax.rsqrt(jnp.mean(x * x, axis=-1, keepdims=True) + NORM_EPS) * gain


def _row_tile(s):
    for t in (512, 256, 128, 64, 32, 16, 8):
        if s % t == 0:
            return t
    return s


def _mm(name, pairs, *, grid, a_blk, a_map, b_blk, b_map, o_shape, o_blk, o_map, dims, out_dtype=F32, vmem_mb=48,
        into=None):
    n_red = grid[-1]
    n_pairs = len(pairs)
    n_in = 2 * n_pairs + (into is not None)
    acc_shape = tuple(d for d in o_blk if d is not None)

    def body(*refs):
        ins, o_ref, scr = refs[:2 * n_pairs], refs[n_in], refs[n_in + 1:]
        part = _dotb(ins[0][...], ins[1][...], dims)
        for p in range(1, n_pairs):
            part = part + _dotb(ins[2 * p][...], ins[2 * p + 1][...], dims)
        if n_red == 1:
            o_ref[...] = part.astype(o_ref.dtype)
        else:
            acc = scr[0]
            r = pl.program_id(len(grid) - 1)

            @pl.when(r == 0)
            def _():
                acc[...] = part

            @pl.when(r > 0)
            def _():
                acc[...] += part

            @pl.when(r == n_red - 1)
            def _():
                o_ref[...] = acc[...].astype(o_ref.dtype)

    in_specs, args = [], []
    for a, b in pairs:
        in_specs += [pl.BlockSpec(a_blk, a_map), pl.BlockSpec(b_blk, b_map)]
        args += [a, b]
    if into is not None:
        in_specs.append(pl.BlockSpec(memory_space=pl.ANY))
        args.append(into)
    return pl.pallas_call(
        body, name=name, grid=grid, in_specs=in_specs, out_specs=pl.BlockSpec(o_blk, o_map),
        out_shape=jax.ShapeDtypeStruct(o_shape, out_dtype),
        input_output_aliases={} if into is None else {n_in - 1: 0},
        scratch_shapes=[pltpu.VMEM(acc_shape, F32)] if n_red > 1 else [],
        compiler_params=_params(("parallel",) * (len(grid) - 1) + ("arbitrary",), vmem_mb),
    )(*args)


def _col_tile(n):
    for t in (1536, 1408, 1024, 768, 512, 384, 256, 128):
        if n % t == 0:
            return t
    return n


def _mm_cols(name, a, w, widx, out_dtype):
    s, k = a.shape
    j_n, nj = w.shape[0], w.shape[-1]
    tm, tn = _row_tile(s), _col_tile(nj)
    nt = nj // tn
    lead = (None,) * (1 + len(widx))
    return _mm(name, [(a, w)], grid=(j_n, nt, s // tm, 1),
               a_blk=(tm, k), a_map=lambda j, c, i, r: (i, 0),
               b_blk=lead + (k, tn), b_map=lambda j, c, i, r: (j, *widx, 0, c),
               o_shape=(s, j_n * nj), o_blk=(tm, tn), o_map=lambda j, c, i, r: (i, j * nt + c),
               dims=NN, out_dtype=out_dtype)


def _mm_rows(name, a, w, widx, out_dtype):
    s = a.shape[0]
    j_n, kj, n = w.shape[0], w.shape[-2], w.shape[-1]
    tm = _row_tile(s)
    lead = (None,) * (1 + len(widx))
    return _mm(name, [(a, w)], grid=(s // tm, j_n),
               a_blk=(tm, kj), a_map=lambda i, j: (i, j),
               b_blk=lead + (kj, n), b_map=lambda i, j: (j, *widx, 0, 0),
               o_shape=(s, n), o_blk=(tm, n), o_map=lambda i, j: (i, 0), dims=NN, out_dtype=out_dtype)


def _mm_rows_t(name, a, w, widx, out_dtype):
    s, n = a.shape
    j_n, kj = w.shape[0], w.shape[-2]
    tm = _row_tile(s)
    lead = (None,) * (1 + len(widx))
    return _mm(name, [(a, w)], grid=(j_n, s // tm, 1),
               a_blk=(tm, n), a_map=lambda j, i, r: (i, 0),
               b_blk=lead + (kj, n), b_map=lambda j, i, r: (j, *widx, 0, 0),
               o_shape=(s, j_n * kj), o_blk=(tm, kj), o_map=lambda j, i, r: (i, j), dims=NT, out_dtype=out_dtype)


def _mm_cols_t(name, pairs, widx, out_dtype):
    a0, w0 = pairs[0]
    s = a0.shape[0]
    j_n, k, nj = w0.shape[0], w0.shape[-2], w0.shape[-1]
    tm, tn = _row_tile(s), _col_tile(nj)
    nt = nj // tn
    lead = (None,) * (1 + len(widx))
    return _mm(name, pairs, grid=(s // tm, j_n * nt),
               a_blk=(tm, tn), a_map=lambda i, r: (i, r),
               b_blk=lead + (k, tn), b_map=lambda i, r: (lax.div(r, nt), *widx, 0, lax.rem(r, nt)),
               o_shape=(s, k), o_blk=(tm, k), o_map=lambda i, r: (i, 0), dims=NT, out_dtype=out_dtype)


def _mm_grad(name, a, b, a_cols, b_cols, out_dtype=BF16, lead=(), lead_dims=(), into=None):
    s = a.shape[0]
    tm = _row_tile(s)
    if a_cols is not None:
        j_n, ka, nb = a_cols, a.shape[1] // a_cols, b.shape[1]
        tk, tn = ka, _col_tile(nb)
        a_map = lambda j, kb, c, i: (i, j)
        b_map = lambda j, kb, c, i: (i, c)
    else:
        j_n, ka, nb = b_cols, a.shape[1], b.shape[1] // b_cols
        tk, tn = min(ka, 1024), _col_tile(nb)
        nt_ = nb // tn
        a_map = lambda j, kb, c, i: (i, kb)
        b_map = lambda j, kb, c, i: (i, j * nt_ + c)
    return _mm(name, [(a, b)], grid=(j_n, ka // tk, nb // tn, s // tm),
               a_blk=(tm, tk), a_map=a_map, b_blk=(tm, tn), b_map=b_map,
               o_shape=(j_n, *lead_dims, ka, nb), o_blk=(None,) * (1 + len(lead)) + (tk, tn),
               o_map=lambda j, kb, c, i: (j, *lead, kb, c), dims=TN, out_dtype=out_dtype, into=into)


def _rowwise(name, fn, rows, outs, *, bcast=(), accs=(), tm=None, vmem_mb=48):
    rows = [r if isinstance(r, tuple) else (r, r.shape[1], 0) for r in rows]
    s = rows[0][0].shape[0]
    tm = tm or min(_row_tile(s), 256)
    nr, nb, no = len(rows), len(bcast), len(outs)

    def body(*refs):
        o_refs, a_refs = refs[nr + nb:nr + nb + no], refs[nr + nb + no:]
        res = fn(*[r[...] for r in refs[:nr + nb]])
        if not isinstance(res, (tuple, list)):
            res = (res,)
        for o, v in zip(o_refs, res[:no]):
            o[...] = v.astype(o.dtype)
        if a_refs:
            i = pl.program_id(0)
            for a, v in zip(a_refs, res[no:]):
                @pl.when(i == 0)
                def _(a=a, v=v):
                    a[...] = v

                @pl.when(i > 0)
                def _(a=a, v=v):
                    a[...] += v

    in_specs = [pl.BlockSpec((tm, w), lambda i, c=c: (i, c)) for _, w, c in rows]
    in_specs += [pl.BlockSpec(b.shape, lambda i, nd=b.ndim: (0,) * nd) for b in bcast]
    out_specs = [pl.BlockSpec((tm, c), lambda i: (i, 0)) for c, _ in outs]
    out_specs += [pl.BlockSpec(sh, lambda i, nd=len(sh): (0,) * nd) for sh in accs]
    out_shape = [jax.ShapeDtypeStruct((s, c), dt) for c, dt in outs] + [jax.ShapeDtypeStruct(sh, F32) for sh in accs]
    res = pl.pallas_call(
        body, name=name, grid=(s // tm,), in_specs=in_specs, out_specs=out_specs, out_shape=out_shape,
        compiler_params=_params(("arbitrary",) if accs else ("parallel",), vmem_mb),
    )(*[r[0] for r in rows], *bcast)
    return res


def _rms_fwd(name, x, gain):
    return _rowwise(name, lambda xv, g: _rms(xv, g), [x], [(x.shape[1], BF16)], bcast=[gain])[0]


def _rms_residual(name, x, f, gain, scale):
    return _rowwise(name, lambda xv, fv, g: xv + scale * _rms(fv, g), [x, f], [(x.shape[1], F32)], bcast=[gain])[0]


def _rms_bwd_math(dy, x, gain):
    r = lax.rsqrt(jnp.mean(x * x, axis=-1, keepdims=True) + NORM_EPS)
    xh = x * r
    dxh = dy * gain
    dx = r * (dxh - xh * jnp.mean(dxh * xh, axis=-1, keepdims=True))
    return dx, jnp.sum(dy * xh, axis=0, keepdims=True)


def _rms_bwd(name, dy, x, gain, scale=1.0, residual=None, out_dtype=F32):
    d = x.shape[1]
    if residual is None:
        fn = lambda dyv, xv, g: _rms_bwd_math(scale * dyv.astype(F32), xv, g)
        rows = [dy, x]
    else:
        def fn(dyv, xv, rv, g):
            dx, dg = _rms_bwd_math(scale * dyv.astype(F32), xv, g)
            return dx + rv, dg
        rows = [dy, x, residual]
    return _rowwise(name, fn, rows, [(d, out_dtype)], bcast=[gain], accs=[(1, d)])


def _swiglu_fwd(name, a, b):
    return _rowwise(name, lambda av, bv: _silu(av.astype(F32)) * bv.astype(F32), [a, b], [(a.shape[1], BF16)])[0]


def _swiglu_bwd(name, ds, a, b):
    def fn(dsv, av, bv):
        dsv, av, bv = dsv.astype(F32), av.astype(F32), bv.astype(F32)
        sg = _sigmoid(av)
        return dsv * bv * sg * (1.0 + av * (1.0 - sg)), dsv * av * sg
    f = a.shape[1]
    return _rowwise(name, fn, [ds, a, b], [(f, BF16), (f, BF16)])


def _loss_and_grad(y, target):
    d = y.shape[1]

    def fn(yv, tv):
        e = yv - tv
        part = 0.5 * jnp.sum(jnp.mean(e * e, axis=-1, keepdims=True), axis=0, keepdims=True)
        return e * (1.0 / d), jnp.broadcast_to(part, (1, 128))
    dy, loss = _rowwise("loss_head", fn, [y, target], [(d, F32)], accs=[(1, 128)])
    return loss[0, 0], dy


def _s5_operators(lam_re, lam_im, b_re, b_im, c_re, c_im, d_skip, log_dt):
    t_n, ch, p_n = SSM_T, SSM_CH, SSM_STATE
    dt = jnp.exp(log_dt)[:, None]
    ld_re, ld_im = lam_re * dt, lam_im * dt
    k = jnp.arange(t_n + 1, dtype=F32)[None, :, None]
    mag = jnp.exp(ld_re[:, None, :] * k)
    pw_re, pw_im = mag * jnp.cos(ld_im[:, None, :] * k), mag * jnp.sin(ld_im[:, None, :] * k)
    lb_re, lb_im = pw_re[:, 1], pw_im[:, 1]
    den = lam_re * lam_re + lam_im * lam_im
    f_re = ((lb_re - 1.0) * lam_re + lb_im * lam_im) / den
    f_im = (lb_im * lam_re - (lb_re - 1.0) * lam_im) / den
    bb_re = f_re[..., None] * b_re - f_im[..., None] * b_im
    bb_im = f_re[..., None] * b_im + f_im[..., None] * b_re
    cp_re = c_re[:, None] * pw_re[:, :t_n, None, :] - c_im[:, None] * pw_im[:, :t_n, None, :]
    cp_im = c_re[:, None] * pw_im[:, :t_n, None, :] + c_im[:, None] * pw_re[:, :t_n, None, :]
    taps = (jnp.einsum("gtcp,gpd->gtcd", cp_re, bb_re, precision=HI)
            - jnp.einsum("gtcp,gpd->gtcd", cp_im, bb_im, precision=HI))
    m5 = jnp.stack([jnp.pad(taps[:, :t_n - j], ((0, 0), (j, 0), (0, 0), (0, 0))) for j in range(t_n)],
                   axis=1)
    m_op = jnp.transpose(m5, (0, 1, 4, 2, 3)).reshape(SSM_GROUPS, t_n * ch, t_n * ch)
    m_op = m_op + jnp.eye(t_n * ch, dtype=F32)[None] * jnp.tile(d_skip.reshape(SSM_GROUPS, 1, ch), (1, t_n, 1)).reshape(
        SSM_GROUPS, 1, t_n * ch)
    rv_re, rv_im = pw_re[:, t_n - 1::-1][:, :t_n], pw_im[:, t_n - 1::-1][:, :t_n]
    bo_re = rv_re[:, :, None, :] * jnp.swapaxes(bb_re, 1, 2)[:, None] - rv_im[:, :, None, :] * jnp.swapaxes(bb_im, 1, 2)[:, None]
    bo_im = rv_re[:, :, None, :] * jnp.swapaxes(bb_im, 1, 2)[:, None] + rv_im[:, :, None, :] * jnp.swapaxes(bb_re, 1, 2)[:, None]
    b_op = jnp.concatenate([bo_re, bo_im], axis=-1).reshape(SSM_GROUPS, t_n * ch, 2 * p_n)
    q_re = c_re[:, None] * pw_re[:, 1:, None, :] - c_im[:, None] * pw_im[:, 1:, None, :]
    q_im = c_re[:, None] * pw_im[:, 1:, None, :] + c_im[:, None] * pw_re[:, 1:, None, :]
    c_op = jnp.concatenate([q_re, -q_im], axis=-1).reshape(SSM_GROUPS, t_n * ch, 2 * p_n)
    c_op = jnp.swapaxes(c_op, 1, 2)
    a1 = jnp.concatenate([pw_re[:, t_n], pw_re[:, t_n]], axis=-1)[:, None, :]
    a2 = jnp.concatenate([-pw_im[:, t_n], pw_im[:, t_n]], axis=-1)[:, None, :]
    return m_op, b_op, c_op, a1, a2


def _s5_groups(name, fn, ins, out_dims):
    g_n = ins[0].shape[0]
    blk = lambda a, b: pl.BlockSpec((None, a, b), lambda g: (g, 0, 0))

    def body(*refs):
        res = fn(*[r[...] for r in refs[:len(ins)]])
        for o, v in zip(refs[len(ins):], res):
            o[...] = v

    return pl.pallas_call(
        body, name=name, grid=(g_n,), in_specs=[blk(*a.shape[1:]) for a in ins], out_specs=[blk(*d) for d in out_dims],
        out_shape=[jax.ShapeDtypeStruct((g_n,) + tuple(d), F32) for d in out_dims],
        compiler_params=_params(("parallel",)),
    )(*ins)


def _s5_state_scan(z_t, a1, a2):
    nc, g_n, p2 = z_t.shape

    def body(z_ref, a1_ref, a2_ref, s_ref):
        a1v, a2v = a1_ref[...], a2_ref[...]

        def step(n, s):
            s_ref[n] = s
            return a1v * s + a2v * pltpu.roll(s, SSM_STATE, 1) + z_ref[n]

        lax.fori_loop(0, nc, step, jnp.zeros((g_n, p2), F32))

    return pl.pallas_call(body, name="s5_state_scan", out_shape=jax.ShapeDtypeStruct(z_t.shape, F32))(z_t, a1, a2)


def _s5_state_scan_bwd(ds_t, s_t, a1, a2):
    nc, g_n, p2 = s_t.shape

    def body(ds_ref, s_ref, a1_ref, a2_ref, dz_ref, da1_ref, da2_ref):
        a1v, a2v = a1_ref[...], a2_ref[...]

        def step(k, carry):
            g, d1, d2 = carry
            n = nc - 1 - k
            dz_ref[n] = g
            sn = s_ref[n]
            d1 = d1 + g * sn
            d2 = d2 + g * pltpu.roll(sn, SSM_STATE, 1)
            g = ds_ref[n] + a1v * g + pltpu.roll(a2v * g, SSM_STATE, 1)
            return g, d1, d2

        zero = jnp.zeros((g_n, p2), F32)
        _, d1, d2 = lax.fori_loop(0, nc, step, (zero, zero, zero))
        da1_ref[...] = d1
        da2_ref[...] = d2

    row = jax.ShapeDtypeStruct((g_n, p2), F32)
    return pl.pallas_call(body, name="s5_state_scan_bwd",
                          out_shape=[jax.ShapeDtypeStruct(s_t.shape, F32), row, row])(ds_t, s_t, a1, a2)


def _s5_scan_fwd(u_g, m_op, b_op, c_op, a1, a2):
    _, nc, w = u_g.shape
    p2 = 2 * SSM_STATE
    z_g, = _s5_groups("s5_chunk_inputs", lambda u, b: (_dotf(u, b),), [u_g, b_op], [(nc, p2)])
    s_g = jnp.swapaxes(_s5_state_scan(jnp.swapaxes(z_g, 0, 1), a1[:, 0], a2[:, 0]), 0, 1)
    y_g, = _s5_groups("s5_outputs", lambda u, s, m, c: (_dotf(u, m) + _dotf(s, c),), [u_g, s_g, m_op, c_op], [(nc, w)])
    return y_g, s_g


def _s5_scan_bwd(dy_g, u_g, s_g, m_op, b_op, c_op, a1, a2):
    _, nc, w = u_g.shape
    p2 = 2 * SSM_STATE
    ds_g, = _s5_groups("s5_dstate", lambda dy, c: (_dotf(dy, c, NT),), [dy_g, c_op], [(nc, p2)])
    dz_t, da1, da2 = _s5_state_scan_bwd(jnp.swapaxes(ds_g, 0, 1), jnp.swapaxes(s_g, 0, 1), a1[:, 0], a2[:, 0])
    dz_g = jnp.swapaxes(dz_t, 0, 1)

    def grads(dy, dz, u, s, m, b):
        return _dotf(dy, m, NT) + _dotf(dz, b, NT), _dotf(u, dy, TN), _dotf(u, dz, TN), _dotf(s, dy, TN)

    du, dm, db, dc = _s5_groups("s5_grads", grads, [dy_g, dz_g, u_g, s_g, m_op, b_op], [(nc, w), (w, w), (w, p2), (p2, w)])
    return du, dm, db, dc, da1[:, None], da2[:, None]


def _to_groups(u):
    s = u.shape[0]
    return u.reshape(s // SSM_T, SSM_T, SSM_GROUPS, SSM_CH).transpose(2, 0, 1, 3).reshape(
        SSM_GROUPS, s // SSM_T, SSM_T * SSM_CH)


def _from_groups(y_g):
    nc = y_g.shape[1]
    return y_g.reshape(SSM_GROUPS, nc, SSM_T, SSM_CH).transpose(1, 2, 0, 3).reshape(nc * SSM_T, SSM_WIDTH)


def _s5_post_math(y, glu_w, glu_b, gain):
    y2 = _gelu(y)
    o = y2 * _sigmoid(_dotb(y2, glu_w) + glu_b)
    return _rms(o, gain)


def _s5_post_fwd(y, glu_w, glu_b, gain):
    return _rowwise("s5_post_fwd", _s5_post_math, [y], [(SSM_WIDTH, F32)], bcast=[glu_w, glu_b, gain])[0]


def _s5_post_bwd(dout, y, glu_w, glu_b, gain):
    def fn(dv, yv, w, b, g):
        _, vjp = jax.vjp(_s5_post_math, yv, w, b, g)
        return vjp(dv)
    return _rowwise("s5_post_bwd", fn, [dout, y], [(SSM_WIDTH, F32)], bcast=[glu_w, glu_b, gain],
                    accs=[(SSM_WIDTH, SSM_WIDTH), (1, SSM_WIDTH), (1, SSM_WIDTH)])


def _t5_bucket(dist):
    max_exact = N_BUCKETS // 2
    d = jnp.maximum(dist, 1).astype(F32)
    large = max_exact + jnp.log(d / max_exact) / math.log(REL_MAX / max_exact) * (N_BUCKETS - max_exact)
    large = jnp.minimum(large.astype(jnp.int32), N_BUCKETS - 1)
    return jnp.where(dist < max_exact, dist, large)


def _attn_bias_tables(rel_bias):
    blk = AT_BLK
    tabs = []
    for window, dil in DILATED:
        rel = blk + jnp.arange(blk)[:, None] - jnp.arange(2 * blk)[None, :]
        valid = (rel >= 0) & (rel <= window // dil)
        bias = jnp.moveaxis(rel_bias[_t5_bucket(jnp.maximum(rel, 0) * dil)], -1, 0)
        tabs.append(jnp.where(valid[None], bias, NEG_INF))
    return jnp.stack(tabs)


def _attn_branch_fwd(b_idx, dil, q2, k2, v2, bias):
    sub, blk, dh = q2.shape[0], AT_BLK, AT_DIM
    nb = sub // blk
    scale = dh ** -0.5

    def body(q_ref, k_ref, v_ref, b_ref, o_ref, l_ref):
        def block(n, carry):
            cur = pl.multiple_of(n * blk, blk)
            prv = pl.multiple_of(jnp.maximum(n - 1, 0) * blk, blk)
            q = q_ref[pl.ds(cur, blk), :] * scale
            lc = _dotb(q, k_ref[pl.ds(cur, blk), :], NT) + b_ref[:, blk:]
            lp = _dotb(q, k_ref[pl.ds(prv, blk), :], NT) + b_ref[:, :blk]
            lp = jnp.where(n > 0, lp, NEG_INF)
            m = jnp.maximum(jnp.max(lc, axis=1, keepdims=True), jnp.max(lp, axis=1, keepdims=True))
            pc, pp = jnp.exp(lc - m), jnp.exp(lp - m)
            den = jnp.sum(pc, axis=1, keepdims=True) + jnp.sum(pp, axis=1, keepdims=True)
            inv = 1.0 / den
            o = _dotb(pc * inv, v_ref[pl.ds(cur, blk), :]) + _dotb(pp * inv, v_ref[pl.ds(prv, blk), :])
            o_ref[pl.ds(cur, blk), :] = o
            l_ref[pl.ds(cur, blk), :] = jnp.broadcast_to(m + jnp.log(den), (blk, dh))
            return carry

        lax.fori_loop(0, nb, block, 0)

    seq = pl.BlockSpec((sub, dh), lambda h, r: (0, r * AT_HEADS + h))
    return pl.pallas_call(
        body, name=f"attn_fwd_d{dil}", grid=(AT_HEADS, dil),
        in_specs=[seq, seq, seq, pl.BlockSpec((None, None, blk, 2 * blk), lambda h, r: (b_idx, h, 0, 0))],
        out_specs=[seq, seq],
        out_shape=[jax.ShapeDtypeStruct(q2.shape, F32), jax.ShapeDtypeStruct(q2.shape, F32)],
        compiler_params=_params(("parallel", "parallel")),
    )(q2, k2, v2, bias)


def _attn_branch_bwd(b_idx, dil, q2, k2, v2, do2, lse2, dlt2, bias):
    sub, blk, dh = q2.shape[0], AT_BLK, AT_DIM
    nb = sub // blk
    scale = dh ** -0.5

    def body(q_ref, k_ref, v_ref, do_ref, l_ref, d_ref, b_ref, dq_ref, dk_ref, dv_ref, db_ref):
        r = pl.program_id(1)
        dk_ref[...] = jnp.zeros_like(dk_ref)
        dv_ref[...] = jnp.zeros_like(dv_ref)

        @pl.when(r == 0)
        def _():
            db_ref[...] = jnp.zeros_like(db_ref)

        def block(n, carry):
            cur = pl.multiple_of(n * blk, blk)
            prv = pl.multiple_of(jnp.maximum(n - 1, 0) * blk, blk)
            q = q_ref[pl.ds(cur, blk), :] * scale
            do = do_ref[pl.ds(cur, blk), :]
            lse = l_ref[pl.ds(cur, blk), :][:, :1]
            dlt = d_ref[pl.ds(cur, blk), :][:, :1]
            kc, kp = k_ref[pl.ds(cur, blk), :], k_ref[pl.ds(prv, blk), :]
            vc, vp = v_ref[pl.ds(cur, blk), :], v_ref[pl.ds(prv, blk), :]
            lc = _dotb(q, kc, NT) + b_ref[:, blk:]
            lp = jnp.where(n > 0, _dotb(q, kp, NT) + b_ref[:, :blk], NEG_INF)
            pc, pp = jnp.exp(lc - lse), jnp.exp(lp - lse)
            dsc = pc * (_dotb(do, vc, NT) - dlt)
            dsp = pp * (_dotb(do, vp, NT) - dlt)
            dq_ref[pl.ds(cur, blk), :] = (_dotb(dsc, kc) + _dotb(dsp, kp)) * scale
            dk_ref[pl.ds(cur, blk), :] += _dotb(dsc, q, TN)
            dk_ref[pl.ds(prv, blk), :] += _dotb(dsp, q, TN)
            dv_ref[pl.ds(cur, blk), :] += _dotb(pc, do, TN)
            dv_ref[pl.ds(prv, blk), :] += _dotb(pp, do, TN)
            db_ref[:, blk:] += dsc
            db_ref[:, :blk] += dsp
            return carry

        lax.fori_loop(0, nb, block, 0)

    seq = pl.BlockSpec((sub, dh), lambda h, r: (0, r * AT_HEADS + h))
    tab = pl.BlockSpec((None, None, blk, 2 * blk), lambda h, r: (b_idx, h, 0, 0))
    return pl.pallas_call(
        body, name=f"attn_bwd_d{dil}", grid=(AT_HEADS, dil),
        in_specs=[seq, seq, seq, seq, seq, seq, tab],
        out_specs=[seq, seq, seq, pl.BlockSpec((None, blk, 2 * blk), lambda h, r: (h, 0, 0))],
        out_shape=[jax.ShapeDtypeStruct(q2.shape, F32)] * 3 + [jax.ShapeDtypeStruct((AT_HEADS, blk, 2 * blk), F32)],
        compiler_params=_params(("parallel", "arbitrary"), 56),
    )(q2, k2, v2, do2, lse2, dlt2, bias)


def _per_head(fn, *xs):
    return jnp.concatenate([fn(*[x[:, h * AT_DIM:(h + 1) * AT_DIM] for x in xs]) for h in range(AT_HEADS)], axis=1)


def _attn_merge_math(o1, o2, o3, l1, l2, l3, gain):
    m = jnp.maximum(jnp.maximum(l1, l2), l3)
    e1, e2, e3 = jnp.exp(l1 - m), jnp.exp(l2 - m), jnp.exp(l3 - m)
    den = e1 + e2 + e3
    o = (e1 * o1 + e2 * o2 + e3 * o3) / den
    return _rms(o, gain), o, m + jnp.log(den)


def _attn_merge_fwd(os_, ls_, gain):
    w = AT_WIDTH
    return _rowwise("attn_merge_fwd", _attn_merge_math, [*os_, *ls_], [(w, F32)] * 3, bcast=[gain])


def _attn_merge_bwd(dy, o, gain):
    def fn(dyv, ov, g):
        do, dg = _rms_bwd_math(dyv, ov, g)
        dlt = _per_head(lambda a, b: jnp.broadcast_to(jnp.sum(a * b, axis=1, keepdims=True), a.shape), do, ov)
        return do, dlt, dg
    return _rowwise("attn_merge_bwd", fn, [dy, o], [(AT_WIDTH, F32)] * 2, bcast=[gain], accs=[(1, AT_WIDTH)])


def _add3(name, a, b, c):
    return _rowwise(name, lambda x, y, z: x + y + z, [a, b, c], [(a.shape[1], F32)])[0]


def _conv_taps(x, w):
    row = lax.broadcasted_iota(jnp.int32, x.shape, 0)
    y = w[DN_CONV - 1:DN_CONV, :] * x
    for sh in range(1, DN_CONV):
        y = y + w[DN_CONV - 1 - sh:DN_CONV - sh, :] * jnp.where(row >= sh, pltpu.roll(x, sh, 0), 0.0)
    return y


def _gdn_prep_fwd(proj, conv_w):
    s = proj.shape[0]
    ncb = 3 * DN_HEADS
    c0 = COL_DQKV // 128

    def body(x_ref, w_ref, o_ref):
        o_ref[...] = _silu(_conv_taps(x_ref[...], w_ref[...]))

    return pl.pallas_call(
        body, name="gdn_prep_fwd", grid=(ncb,),
        in_specs=[pl.BlockSpec((s, 128), lambda c: (0, c0 + c)), pl.BlockSpec((DN_CONV, 128), lambda c: (0, c))],
        out_specs=pl.BlockSpec((None, s, 128), lambda c: (c, 0, 0)),
        out_shape=jax.ShapeDtypeStruct((ncb, s, 128), F32),
        compiler_params=_params(("parallel",)),
    )(proj, conv_w)


def _gdn_prep_bwd(dact, proj, conv_w):
    s = proj.shape[0]
    ncb = 3 * DN_HEADS
    c0 = COL_DQKV // 128

    def body(d_ref, x_ref, w_ref, dx_ref, dw_ref):
        x, w = x_ref[...], w_ref[...]
        pre = _conv_taps(x, w)
        sg = _sigmoid(pre)
        dpre = d_ref[...] * sg * (1.0 + pre * (1.0 - sg))
        row = lax.broadcasted_iota(jnp.int32, x.shape, 0)
        dx = w[DN_CONV - 1:DN_CONV, :] * dpre
        dw_ref[pl.ds(DN_CONV - 1, 1), :] = jnp.sum(dpre * x, axis=0, keepdims=True)
        for sh in range(1, DN_CONV):
            dx = dx + w[DN_CONV - 1 - sh:DN_CONV - sh, :] * jnp.where(row < s - sh, pltpu.roll(dpre, s - sh, 0), 0.0)
            dw_ref[pl.ds(DN_CONV - 1 - sh, 1), :] = jnp.sum(
                dpre * jnp.where(row >= sh, pltpu.roll(x, sh, 0), 0.0), axis=0, keepdims=True)
        dx_ref[...] = dx

    return pl.pallas_call(
        body, name="gdn_prep_bwd", grid=(ncb,),
        in_specs=[pl.BlockSpec((None, s, 128), lambda c: (c, 0, 0)), pl.BlockSpec((s, 128), lambda c: (0, c0 + c)),
                  pl.BlockSpec((DN_CONV, 128), lambda c: (0, c))],
        out_specs=[pl.BlockSpec((s, 128), lambda c: (0, c)), pl.BlockSpec((DN_CONV, 128), lambda c: (0, c))],
        out_shape=[jax.ShapeDtypeStruct((s, ncb * 128), F32), jax.ShapeDtypeStruct((DN_CONV, ncb * 128), F32)],
        compiler_params=_params(("parallel",)),
    )(dact, proj, conv_w)


def _gates_math(ab, alog, dtb):
    lane = lax.broadcasted_iota(jnp.int32, ab.shape, 1)
    g = -jnp.exp(alog) * _softplus(ab + dtb)
    return jnp.where(lane < DN_HEADS, g, jnp.where(lane < 2 * DN_HEADS, _sigmoid(ab), 0.0))


def _gates_fwd(proj, alog, dtb):
    return _rowwise("gdn_gates_fwd", _gates_math, [(proj, 128, COL_DAB // 128)], [(128, F32)], bcast=[alog, dtb])[0]


def _gates_bwd(dgates, proj, alog, dtb):
    def fn(dv, ab, a, d):
        _, vjp = jax.vjp(_gates_math, ab, a, d)
        return vjp(dv)
    return _rowwise("gdn_gates_bwd", fn, [dgates, (proj, 128, COL_DAB // 128)], [(128, F32)], bcast=[alog, dtb],
                    accs=[(1, 128), (1, 128)])


def _l2n(x):
    return x * lax.rsqrt(jnp.sum(x * x, axis=-1, keepdims=True) + NORM_EPS)


def _gdn_intra_math(q, k, v, gcol, grow, bcol):
    c = DN_CHUNK
    ii = lax.broadcasted_iota(jnp.int32, (1, c, c), 1)
    jj = lax.broadcasted_iota(jnp.int32, (1, c, c), 2)
    gc_col = jnp.sum(jnp.where(ii >= jj, grow, 0.0), axis=2, keepdims=True)
    gc_row = jnp.sum(jnp.where(ii <= jj, gcol, 0.0), axis=1, keepdims=True)
    gc_last = jnp.sum(gcol, axis=1, keepdims=True)
    decay = jnp.exp(jnp.where(ii >= jj, gc_col - gc_row, NEG_INF))
    qn = _l2n(q) * (DN_DIM ** -0.5)
    kn = _l2n(k)
    kb = kn * bcol
    a_mat = jnp.where(ii > jj, _dot3(kb, kn, BNT) * decay, 0.0)
    nil = -a_mat
    t_inv = jnp.where(ii == jj, 1.0, 0.0) + nil
    for _ in range(5):
        nil = _dot3(nil, nil, BNN)
        t_inv = t_inv + _dot3(t_inv, nil, BNN)
    e_col = jnp.exp(gc_col)
    u = _dot3(t_inv, v * bcol, BNN)
    w = _dot3(t_inv, kb * e_col, BNN)
    attn = _dot3(qn, kn, BNT) * decay
    return u, w, attn, qn * e_col, kn * jnp.exp(gc_last - gc_col), jnp.broadcast_to(jnp.exp(gc_last), (DN_HEADS, 1, 128))


def _gdn_specs(s):
    nc = s // DN_CHUNK
    h, c, d = DN_HEADS, DN_CHUNK, DN_DIM
    return dict(
        qkv=pl.BlockSpec((3 * h, c, d), lambda n: (0, n, 0)),
        hcd=pl.BlockSpec((h, c, d), lambda n: (0, n, 0)),
        col=pl.BlockSpec((h, c, 1), lambda n: (0, n, 0)),
        row=pl.BlockSpec((h, None, 1, c), lambda n: (0, n, 0, 0)),
        att=pl.BlockSpec((h, c, c), lambda n: (0, n, 0)),
        dec=pl.BlockSpec((h, None, 1, 128), lambda n: (0, n, 0, 0)),
        s_hcd=jax.ShapeDtypeStruct((h, s, d), F32), s_col=jax.ShapeDtypeStruct((h, s, 1), F32),
        s_row=jax.ShapeDtypeStruct((h, nc, 1, c), F32), s_att=jax.ShapeDtypeStruct((h, s, c), F32),
        s_dec=jax.ShapeDtypeStruct((h, nc, 1, 128), F32), s_qkv=jax.ShapeDtypeStruct((3 * h, s, d), F32),
    )


def _gdn_intra_fwd(act, gcol, grow, bcol):
    s = act.shape[1]
    sp = _gdn_specs(s)
    h = DN_HEADS

    def body(a_ref, gc_ref, gr_ref, bc_ref, u_ref, w_ref, at_ref, qd_ref, kt_ref, dec_ref):
        outs = _gdn_intra_math(a_ref[0:h], a_ref[h:2 * h], a_ref[2 * h:3 * h], gc_ref[...], gr_ref[...], bc_ref[...])
        for ref, val in zip((u_ref, w_ref, at_ref, qd_ref, kt_ref, dec_ref), outs):
            ref[...] = val

    return pl.pallas_call(
        body, name="gdn_intra_fwd", grid=(s // DN_CHUNK,),
        in_specs=[sp["qkv"], sp["col"], sp["row"], sp["col"]],
        out_specs=[sp["hcd"], sp["hcd"], sp["att"], sp["hcd"], sp["hcd"], sp["dec"]],
        out_shape=[sp["s_hcd"], sp["s_hcd"], sp["s_att"], sp["s_hcd"], sp["s_hcd"], sp["s_dec"]],
        compiler_params=_params(("parallel",)),
    )(act, gcol, grow, bcol)


def _gdn_intra_bwd(act, gcol, grow, bcol, du, dw, dattn, dqd, dkt, ddec):
    s = act.shape[1]
    sp = _gdn_specs(s)
    h = DN_HEADS

    def body(a_ref, gc_ref, gr_ref, bc_ref, du_ref, dw_ref, dat_ref, dqd_ref, dkt_ref, dde_ref,
             dact_ref, dgc_ref, dgr_ref, dbc_ref):
        _, vjp = jax.vjp(_gdn_intra_math, a_ref[0:h], a_ref[h:2 * h], a_ref[2 * h:3 * h],
                         gc_ref[...], gr_ref[...], bc_ref[...])
        dq, dk, dv, dgc, dgr, dbc = vjp((du_ref[...], dw_ref[...], dat_ref[...], dqd_ref[...], dkt_ref[...], dde_ref[...]))
        dact_ref[0:h] = dq
        dact_ref[h:2 * h] = dk
        dact_ref[2 * h:3 * h] = dv
        dgc_ref[...] = dgc
        dgr_ref[...] = dgr
        dbc_ref[...] = dbc

    return pl.pallas_call(
        body, name="gdn_intra_bwd", grid=(s // DN_CHUNK,),
        in_specs=[sp["qkv"], sp["col"], sp["row"], sp["col"], sp["hcd"], sp["hcd"], sp["att"], sp["hcd"], sp["hcd"], sp["dec"]],
        out_specs=[sp["qkv"], sp["col"], sp["row"], sp["col"]],
        out_shape=[sp["s_qkv"], sp["s_col"], sp["s_row"], sp["s_col"]],
        compiler_params=_params(("parallel",)),
    )(act, gcol, grow, bcol, du, dw, dattn, dqd, dkt, ddec)


def _gdn_step_math(state, u, w, attn, qd, kt, dec):
    v_new = u - _dot3(w, state, BNN)
    o = _dot3(qd, state, BNN) + _dot3(attn, v_new, BNN)
    return state * dec[:, :, :1] + _dot3(kt, v_new, BTN), o


def _gdn_scan_fwd(u, w, attn, qd, kt, dec):
    s = u.shape[1]
    nc = s // DN_CHUNK
    sp = _gdn_specs(s)
    h, d = DN_HEADS, DN_DIM

    def body(u_ref, w_ref, at_ref, qd_ref, kt_ref, dec_ref, o_ref, st_ref, state):
        @pl.when(pl.program_id(0) == 0)
        def _():
            state[...] = jnp.zeros_like(state)

        st_ref[...] = state[...]
        new, o = _gdn_step_math(state[...], u_ref[...], w_ref[...], at_ref[...], qd_ref[...], kt_ref[...], dec_ref[...])
        state[...] = new
        o_ref[...] = o

    return pl.pallas_call(
        body, name="gdn_scan_fwd", grid=(nc,),
        in_specs=[sp["hcd"], sp["hcd"], sp["att"], sp["hcd"], sp["hcd"], sp["dec"]],
        out_specs=[sp["hcd"], pl.BlockSpec((None, h, d, d), lambda n: (n, 0, 0, 0))],
        out_shape=[sp["s_hcd"], jax.ShapeDtypeStruct((nc, h, d, d), F32)],
        scratch_shapes=[pltpu.VMEM((h, d, d), F32)],
        compiler_params=_params(("arbitrary",)),
    )(u, w, attn, qd, kt, dec)


def _gdn_scan_bwd(do, states, u, w, attn, qd, kt, dec):
    s = u.shape[1]
    nc = s // DN_CHUNK
    h, c, d = DN_HEADS, DN_CHUNK, DN_DIM
    rev = lambda n: nc - 1 - n
    hcd = pl.BlockSpec((h, c, d), lambda n: (0, rev(n), 0))
    att = pl.BlockSpec((h, c, c), lambda n: (0, rev(n), 0))
    dec_s = pl.BlockSpec((h, None, 1, 128), lambda n: (0, rev(n), 0, 0))
    sp = _gdn_specs(s)

    def body(do_ref, st_ref, u_ref, w_ref, at_ref, qd_ref, kt_ref, dec_ref,
             du_ref, dw_ref, dat_ref, dqd_ref, dkt_ref, dde_ref, dstate):
        @pl.when(pl.program_id(0) == 0)
        def _():
            dstate[...] = jnp.zeros_like(dstate)

        _, vjp = jax.vjp(_gdn_step_math, st_ref[...], u_ref[...], w_ref[...], at_ref[...], qd_ref[...], kt_ref[...],
                         dec_ref[...])
        dst, du, dw, dat, dqd, dkt, dde = vjp((dstate[...], do_ref[...]))
        dstate[...] = dst
        for ref, val in zip((du_ref, dw_ref, dat_ref, dqd_ref, dkt_ref, dde_ref), (du, dw, dat, dqd, dkt, dde)):
            ref[...] = val

    return pl.pallas_call(
        body, name="gdn_scan_bwd", grid=(nc,),
        in_specs=[hcd, pl.BlockSpec((None, h, d, d), lambda n: (rev(n), 0, 0, 0)), hcd, hcd, att, hcd, hcd, dec_s],
        out_specs=[hcd, hcd, att, hcd, hcd, dec_s],
        out_shape=[sp["s_hcd"], sp["s_hcd"], sp["s_att"], sp["s_hcd"], sp["s_hcd"], sp["s_dec"]],
        scratch_shapes=[pltpu.VMEM((h, d, d), F32)],
        compiler_params=_params(("arbitrary",)),
    )(do, states, u, w, attn, qd, kt, dec)


def _gdn_out_math(o, z, gain):
    return _rms(o, gain) * _silu(z)


def _gdn_out_fwd(o_rows, z_rows, gain):
    return _rowwise("gdn_out_fwd", _gdn_out_math, [o_rows, z_rows], [(DN_DIM, F32)], bcast=[gain])[0]


def _gdn_out_bwd(dy_rows, o_rows, z_rows, gain):
    def fn(dv, ov, zv, g):
        _, vjp = jax.vjp(_gdn_out_math, ov, zv, g)
        return vjp(dv)
    return _rowwise("gdn_out_bwd", fn, [dy_rows, o_rows, z_rows], [(DN_DIM, F32)] * 2, bcast=[gain], accs=[(1, DN_DIM)])


def _heads_major(x):
    s = x.shape[0]
    return x.reshape(s, -1, DN_DIM).transpose(1, 0, 2)


def _heads_minor(x):
    return x.transpose(1, 0, 2).reshape(x.shape[1], -1)


def _pad_row(v, width=128):
    return jnp.pad(v.reshape(1, -1), ((0, 0), (0, width - v.size)))


def _ffn_fwd(tag, x, gains_in, gains_out, wg, wu, wd, widx):
    h = _rms_fwd(f"{tag}_prenorm", x, gains_in)
    a = _mm_cols(f"{tag}_gate", h, wg, widx, BF16)
    b = _mm_cols(f"{tag}_up", h, wu, widx, BF16)
    s = _swiglu_fwd(f"{tag}_swiglu", a, b)
    f = _mm_rows(f"{tag}_down", s, wd, widx, F32)
    x_new = _rms_residual(f"{tag}_postnorm", x, f, gains_out, 0.5)
    return x_new, (x, h, a, b, s, f)


def _ffn_bwd(tag, dx_new, saved, gains_in, gains_out, wg, wu, wd, widx, gbuf):
    x, h, a, b, s, f = saved
    grad = lambda n, p, q, ac, bc, wt: _mm_grad(f"{tag}_{n}_grad", p, q, ac, bc, lead=widx, lead_dims=wt.shape[1:-2],
                                               into=gbuf.get(n))
    df, dg_out = _rms_bwd(f"{tag}_postnorm_bwd", dx_new, f, gains_out, 0.5, out_dtype=BF16)
    ds = _mm_rows_t(f"{tag}_down_bwd", df, wd, widx, BF16)
    gbuf["down"] = grad("down", s, df, wd.shape[0], None, wd)
    da, db = _swiglu_bwd(f"{tag}_swiglu_bwd", ds, a, b)
    gbuf["gate"] = grad("gate", h, da, None, wg.shape[0], wg)
    gbuf["up"] = grad("up", h, db, None, wu.shape[0], wu)
    dh = _mm_cols_t(f"{tag}_gateup_bwd", [(da, wg), (db, wu)], widx, F32)
    dx, dg_in = _rms_bwd(f"{tag}_prenorm_bwd", dh, x, gains_in, 1.0, residual=dx_new)
    return dx, dg_in, dg_out


def _attn_views(t, dil):
    return t.reshape(t.shape[0] // dil, dil * AT_WIDTH)


def _mixer_fwd(l, h, w, p, bias_tabs):
    s = h.shape[0]
    nc = s // DN_CHUNK
    proj = _mm_cols("w_in_fwd", h, w["w_in"], (l,), F32)
    ops, ops_vjp = jax.vjp(_s5_operators, p["ssm_lambda_re"][l], p["ssm_lambda_im"][l], p["ssm_b_re"][l], p["ssm_b_im"][l],
                           p["ssm_c_re"][l], p["ssm_c_im"][l], p["ssm_d"][l], p["ssm_log_dt"][l])
    u_g = _to_groups(proj[:, :SSM_WIDTH])
    y_g, s_g = _s5_scan_fwd(u_g, *ops)
    y = _from_groups(y_g)
    glu_w, glu_b, gain_ssm = p["ssm_glu_w"][l], p["ssm_glu_b"][l][None], p["ssm_out_gain"][l][None]
    y_ssm = _s5_post_fwd(y, glu_w, glu_b, gain_ssm)
    conv_w, alog, dtb = p["dn_conv_w"][l], _pad_row(p["dn_a_log"][l]), _pad_row(p["dn_dt_bias"][l])
    gain_dn = p["dn_norm_gain"][l][None]
    act = _gdn_prep_fwd(proj, conv_w)
    gates_t = _gates_fwd(proj, alog, dtb)[:, :2 * DN_HEADS].T
    gcol, bcol = gates_t[:DN_HEADS, :, None], gates_t[DN_HEADS:, :, None]
    grow = gates_t[:DN_HEADS].reshape(DN_HEADS, nc, 1, DN_CHUNK)
    u, wy, attn, qd, kt, dec = _gdn_intra_fwd(act, gcol, grow, bcol)
    o_dn, states = _gdn_scan_fwd(u, wy, attn, qd, kt, dec)
    o_rows = o_dn.reshape(DN_HEADS * s, DN_DIM)
    z_rows = _heads_major(proj[:, COL_DZ:COL_DAB]).reshape(DN_HEADS * s, DN_DIM)
    y_dn = _heads_minor(_gdn_out_fwd(o_rows, z_rows, gain_dn).reshape(DN_HEADS, s, DN_DIM))
    q, k, v = proj[:, COL_AQ:COL_AK], proj[:, COL_AK:COL_AV], proj[:, COL_AV:COL_DQKV]
    outs, lses = [], []
    for bi, (_, dil) in enumerate(DILATED):
        o_b, l_b = _attn_branch_fwd(bi, dil, _attn_views(q, dil), _attn_views(k, dil), _attn_views(v, dil), bias_tabs)
        outs.append(o_b.reshape(s, AT_WIDTH))
        lses.append(l_b.reshape(s, AT_WIDTH))
    gain_at = p["attn_out_gain"][l][None]
    y_at, o_at, lse = _attn_merge_fwd(outs, lses, gain_at)
    mix = jnp.concatenate([y_ssm, y_dn, y_at], axis=1).astype(BF16)
    out = _mm_rows("w_out_fwd", mix, w["w_out"], (l,), F32)
    saved = dict(h=h, proj=proj, ops=ops, ops_vjp=ops_vjp, u_g=u_g, s_g=s_g, y=y, act=act, gcol=gcol, grow=grow, bcol=bcol,
                 u=u, wy=wy, attn=attn, qd=qd, kt=kt, dec=dec, states=states, o_rows=o_rows, z_rows=z_rows,
                 q=q, k=k, v=v, o_at=o_at, lse=lse, mix=mix)
    return out, saved


def _mixer_bwd(l, dout, sv, w, p, bias_tabs):
    s = dout.shape[0]
    proj = sv["proj"]
    g = {}
    g["w_out"] = _mm_grad("w_out_grad", sv["mix"], dout, w["w_out"].shape[0], None)
    dmix = _mm_rows_t("w_out_bwd", dout, w["w_out"], (l,), F32)
    d_ssm, d_dn, d_at = dmix[:, :SSM_WIDTH], dmix[:, SSM_WIDTH:SSM_WIDTH + DN_WIDTH], dmix[:, SSM_WIDTH + DN_WIDTH:]
    glu_w, glu_b, gain_ssm = p["ssm_glu_w"][l], p["ssm_glu_b"][l][None], p["ssm_out_gain"][l][None]
    dy, g["ssm_glu_w"], dglu_b, dgain_ssm = _s5_post_bwd(d_ssm, sv["y"], glu_w, glu_b, gain_ssm)
    g["ssm_glu_b"], g["ssm_out_gain"] = dglu_b[0], dgain_ssm[0]
    du_g, *d_ops = _s5_scan_bwd(_to_groups(dy), sv["u_g"], sv["s_g"], *sv["ops"])
    (g["ssm_lambda_re"], g["ssm_lambda_im"], g["ssm_b_re"], g["ssm_b_im"], g["ssm_c_re"], g["ssm_c_im"], g["ssm_d"],
     g["ssm_log_dt"]) = sv["ops_vjp"](tuple(d_ops))
    d_u = _from_groups(du_g)
    gain_at = p["attn_out_gain"][l][None]
    do, dlt, dgain_at = _attn_merge_bwd(d_at, sv["o_at"], gain_at)
    g["attn_out_gain"] = dgain_at[0]
    dqs, dks, dvs, dbs = [], [], [], []
    for bi, (_, dil) in enumerate(DILATED):
        vw = lambda t: _attn_views(t, dil)
        dq2, dk2, dv2, db = _attn_branch_bwd(bi, dil, vw(sv["q"]), vw(sv["k"]), vw(sv["v"]), vw(do), vw(sv["lse"]), vw(dlt),
                                            bias_tabs)
        dqs.append(dq2.reshape(s, AT_WIDTH))
        dks.append(dk2.reshape(s, AT_WIDTH))
        dvs.append(dv2.reshape(s, AT_WIDTH))
        dbs.append(db)
    dq, dk, dv = _add3("attn_dq_sum", *dqs), _add3("attn_dk_sum", *dks), _add3("attn_dv_sum", *dvs)
    g["bias_tabs"] = jnp.stack(dbs)
    conv_w, alog, dtb = p["dn_conv_w"][l], _pad_row(p["dn_a_log"][l]), _pad_row(p["dn_dt_bias"][l])
    gain_dn = p["dn_norm_gain"][l][None]
    dy_rows = _heads_major(d_dn).reshape(DN_HEADS * s, DN_DIM)
    do_rows, dz_rows, dgain_dn = _gdn_out_bwd(dy_rows, sv["o_rows"], sv["z_rows"], gain_dn)
    g["dn_norm_gain"] = dgain_dn[0]
    d_scan = _gdn_scan_bwd(do_rows.reshape(DN_HEADS, s, DN_DIM), sv["states"], sv["u"], sv["wy"], sv["attn"], sv["qd"],
                           sv["kt"], sv["dec"])
    dact, dgc, dgr, dbc = _gdn_intra_bwd(sv["act"], sv["gcol"], sv["grow"], sv["bcol"], *d_scan)
    dgates_t = jnp.concatenate([dgc[..., 0] + dgr.reshape(DN_HEADS, s), dbc[..., 0]], axis=0)
    dgates = jnp.pad(dgates_t.T, ((0, 0), (0, 128 - 2 * DN_HEADS)))
    dab, dalog, ddtb = _gates_bwd(dgates, proj, alog, dtb)
    g["dn_a_log"], g["dn_dt_bias"] = dalog[0, :DN_HEADS], ddtb[0, :DN_HEADS]
    dqkv, g["dn_conv_w"] = _gdn_prep_bwd(dact, proj, conv_w)
    dz = _heads_minor(dz_rows.reshape(DN_HEADS, s, DN_DIM))
    dproj = jnp.concatenate([d_u, dq, dk, dv, dqkv, dz, dab, jnp.zeros((s, N_IN_PAD - COL_DAB - 128), F32)], axis=1)
    g["w_in"] = _mm_grad("w_in_grad", sv["h"], dproj, None, 1)[0]
    dh = _mm_cols_t("w_in_bwd", [(dproj, w["w_in"])], (l,), F32)
    return dh, g


def _local_step(x, target, w, p):
    depth = p["norm_gains"].shape[0]
    gains = p["norm_gains"]
    bias_tabs, bias_vjp = jax.vjp(_attn_bias_tables, p["rel_bias"])
    saved = []
    for l in range(depth):
        gn = lambda i: gains[l, i][None]
        x, sv1 = _ffn_fwd("ffn1", x, gn(0), gn(1), w["ffn_w_gate"], w["ffn_w_up"], w["ffn_w_down"], (l, 0))
        h = _rms_fwd("mix_prenorm", x, gn(2))
        out, svm = _mixer_fwd(l, h, w, p, bias_tabs)
        x_mid = x
        x = _rms_residual("mix_postnorm", x, out, gn(3), 1.0)
        x, sv2 = _ffn_fwd("ffn2", x, gn(4), gn(5), w["ffn_w_gate"], w["ffn_w_up"], w["ffn_w_down"], (l, 1))
        saved.append((sv1, svm, x_mid, out, sv2))
    loss, dx = _loss_and_grad(x, target)

    small = ["ssm_lambda_re", "ssm_lambda_im", "ssm_b_re", "ssm_b_im", "ssm_c_re", "ssm_c_im", "ssm_d", "ssm_log_dt",
             "ssm_glu_w", "ssm_glu_b", "ssm_out_gain", "dn_conv_w", "dn_a_log", "dn_dt_bias", "dn_norm_gain", "attn_out_gain"]
    per_layer = {n: [None] * depth for n in small + ["norm_gains", "w_in", "w_out"]}
    d_tabs = None
    gbuf = {}
    for l in reversed(range(depth)):
        gn = lambda i: gains[l, i][None]
        sv1, svm, x_mid, out, sv2 = saved[l]
        dx, dg4, dg5 = _ffn_bwd("ffn2", dx, sv2, gn(4), gn(5), w["ffn_w_gate"], w["ffn_w_up"], w["ffn_w_down"], (l, 1), gbuf)
        dout, dg3 = _rms_bwd("mix_postnorm_bwd", dx, out, gn(3), 1.0)
        dh, gm = _mixer_bwd(l, dout, svm, w, p, bias_tabs)
        dx, dg2 = _rms_bwd("mix_prenorm_bwd", dh, x_mid, gn(2), 1.0, residual=dx)
        dx, dg0, dg1 = _ffn_bwd("ffn1", dx, sv1, gn(0), gn(1), w["ffn_w_gate"], w["ffn_w_up"], w["ffn_w_down"], (l, 0), gbuf)
        per_layer["norm_gains"][l] = jnp.concatenate([dg0, dg1, dg2, dg3, dg4, dg5], axis=0)
        d_tabs = gm["bias_tabs"] if d_tabs is None else d_tabs + gm["bias_tabs"]
        for n in small + ["w_in", "w_out"]:
            per_layer[n][l] = gm[n]
    grads = {n: jnp.stack(per_layer[n], axis=0) for n in small + ["norm_gains", "w_in"]}
    grads["w_out"] = jnp.stack(per_layer["w_out"], axis=1)
    grads["ffn_w_gate"], grads["ffn_w_up"], grads["ffn_w_down"] = gbuf["gate"], gbuf["up"], gbuf["down"]
    grads["rel_bias"] = bias_vjp(d_tabs)[0]
    return loss, dx, grads


_ANY = pl.BlockSpec(memory_space=pl.ANY)


def _place():
    return lax.axis_index("x"), lax.axis_index("y"), lax.axis_index("c")


DMA_CHUNKS = 16


class _Transfer:
    def __init__(self, make, src, dst):
        self.make, self.src, self.dst = make, src, dst

    def start(self):
        rows = self.src.shape[0]
        k = DMA_CHUNKS
        while k > 1 and rows % (16 * k):
            k //= 2
        step = rows // k
        for i in range(k):
            self.make(self.src.at[pl.ds(i * step, step)], self.dst.at[pl.ds(i * step, step)]).start()

    def whole(self):
        return self.make(self.src, self.dst)


def _rows_into_block(name, fn, ins, idx, n_blocks, out_dtype):
    r, c = ins[0].shape[-2:]
    tm = min(_row_tile(r), 256)

    def body(idx_ref, *refs):
        refs[-1][...] = fn(*[t[...] for t in refs[:-1]]).astype(out_dtype)

    in_specs = [pl.BlockSpec((tm, c), lambda i, b: (i, 0)) if a.ndim == 2 else
                pl.BlockSpec((a.shape[0], tm, c), lambda i, b: (0, i, 0)) for a in ins]
    return pl.pallas_call(
        body, name=name,
        grid_spec=pltpu.PrefetchScalarGridSpec(
            num_scalar_prefetch=1, grid=(r // tm,), in_specs=in_specs,
            out_specs=pl.BlockSpec((None, tm, c), lambda i, b: (b[0], i, 0))),
        out_shape=jax.ShapeDtypeStruct((n_blocks, r, c), out_dtype),
        compiler_params=_params(("parallel",), 48),
    )(idx, *ins)


def _all_gather_xy(name, buf):
    def body(_, out, send_sems, recv_sems):
        x, y, c = _place()
        sib = (x, y, 1 - c)
        chips = [(1 - x, y), (x, 1 - y), (1 - x, 1 - y)]
        blk = lambda cx, cy: 2 * cx + cy

        def copy(k, j, half, to):
            make = lambda s, d: pltpu.make_async_remote_copy(src_ref=s, dst_ref=d, send_sem=send_sems.at[k],
                                                             recv_sem=recv_sems.at[k], device_id=to, device_id_type=MESH)
            return _Transfer(make, out.at[j, half], out.at[j, half])

        first = [copy(k, blk(x, y), c, (*chip, c)) for k, chip in enumerate(chips)]
        for cp in first:
            cp.start()
        passed = [copy(3 + k, blk(*chip), c, sib) for k, chip in enumerate(chips)]
        for k, chip in enumerate(chips):
            copy(k, blk(*chip), c, (*chip, c)).whole().wait_recv()
            passed[k].start()
        for k, chip in enumerate(chips):
            copy(3 + k, blk(*chip), 1 - c, sib).whole().wait_recv()
        for cp in first + passed:
            cp.whole().wait_send()

    return pl.pallas_call(
        body, name=name, in_specs=[_ANY], out_specs=_ANY, input_output_aliases={0: 0},
        out_shape=jax.ShapeDtypeStruct(buf.shape, buf.dtype),
        scratch_shapes=[pltpu.SemaphoreType.DMA((6,)), pltpu.SemaphoreType.DMA((6,))],
    )(buf)


def _pair_send(name, pieces):
    def body(src, dst, send_sems, recv_sems):
        x, y, c = _place()

        def copy(j, src_ref):
            make = lambda s, d: pltpu.make_async_remote_copy(src_ref=s, dst_ref=d, send_sem=send_sems.at[j],
                                                             recv_sem=recv_sems.at[j], device_id=(x, y, 1 - c),
                                                             device_id_type=MESH)
            return _Transfer(make, src_ref, dst.at[j])

        sends = [copy(j, src.at[j, 1 - c]) for j in range(4)]
        for cp in sends:
            cp.start()
        for j in range(4):
            copy(j, src.at[j, c]).whole().wait_recv()
        for cp in sends:
            cp.whole().wait_send()

    return pl.pallas_call(
        body, name=name, in_specs=[_ANY], out_specs=_ANY,
        out_shape=jax.ShapeDtypeStruct((4,) + pieces.shape[2:], pieces.dtype),
        scratch_shapes=[pltpu.SemaphoreType.DMA((4,)), pltpu.SemaphoreType.DMA((4,))],
    )(pieces)


def _pair_sum(name, pieces, recv, core):
    _, _, r, c = pieces.shape
    tm = min(_row_tile(r), 256)

    def body(core_ref, p_ref, r_ref, o_ref):
        o_ref[...] = (p_ref[...].astype(F32) + r_ref[...].astype(F32)).astype(o_ref.dtype)

    return pl.pallas_call(
        body, name=name,
        grid_spec=pltpu.PrefetchScalarGridSpec(
            num_scalar_prefetch=1, grid=(4, r // tm),
            in_specs=[pl.BlockSpec((None, None, tm, c), lambda j, i, b: (j, b[0], i, 0)),
                      pl.BlockSpec((None, tm, c), lambda j, i, b: (j, i, 0))],
            out_specs=pl.BlockSpec((None, tm, c), lambda j, i, b: (j, i, 0))),
        out_shape=jax.ShapeDtypeStruct((4, r, c), pieces.dtype),
        compiler_params=_params(("parallel", "parallel"), 48),
    )(core, pieces, recv)


def _exchange_xy(name, part):
    def body(src, dst, send_sems, recv_sems, local_sem):
        x, y, c = _place()
        chips = [(1 - x, y), (x, 1 - y), (1 - x, 1 - y)]
        blk = lambda cx, cy: 2 * cx + cy
        me = blk(x, y)

        def copy(k, chip, dst_block):
            make = lambda s, d: pltpu.make_async_remote_copy(src_ref=s, dst_ref=d, send_sem=send_sems.at[k],
                                                             recv_sem=recv_sems.at[k], device_id=(*chip, c),
                                                             device_id_type=MESH)
            return _Transfer(make, src.at[blk(*chip)], dst.at[dst_block])

        mine = _Transfer(lambda s, d: pltpu.make_async_copy(s, d, local_sem), src.at[me], dst.at[me])
        mine.start()
        sends = [copy(k, chip, me) for k, chip in enumerate(chips)]
        for cp in sends:
            cp.start()
        for k, chip in enumerate(chips):
            copy(k, chip, blk(*chip)).whole().wait_recv()
        for cp in sends:
            cp.whole().wait_send()
        mine.whole().wait()

    return pl.pallas_call(
        body, name=name, in_specs=[_ANY], out_specs=_ANY, out_shape=jax.ShapeDtypeStruct(part.shape, part.dtype),
        scratch_shapes=[pltpu.SemaphoreType.DMA((3,)), pltpu.SemaphoreType.DMA((3,)), pltpu.SemaphoreType.DMA],
    )(part)


def _exchange8(name, src, same_to_all=False):
    blk_shape = src.shape if same_to_all else src.shape[1:]

    def body(src_ref, dst, send_sems, recv_sems, local_sem):
        x, y, c = _place()
        me = 4 * x + 2 * y + c
        part = (lambda i: src_ref) if same_to_all else (lambda i: src_ref.at[i])

        def peer(k):
            return (1 - x if k & 4 else x, 1 - y if k & 2 else y, 1 - c if k & 1 else c)

        def copy(k, dst_block):
            px, py, pc = peer(k)
            make = lambda s, d: pltpu.make_async_remote_copy(src_ref=s, dst_ref=d, send_sem=send_sems.at[k - 1],
                                                             recv_sem=recv_sems.at[k - 1], device_id=(px, py, pc),
                                                             device_id_type=MESH)
            return _Transfer(make, part(4 * px + 2 * py + pc), dst.at[dst_block])

        mine = _Transfer(lambda s, d: pltpu.make_async_copy(s, d, local_sem), part(me), dst.at[me])
        mine.start()
        sends = [copy(k, me) for k in range(1, 8)]
        for cp in sends:
            cp.start()
        for k in range(1, 8):
            px, py, pc = peer(k)
            copy(k, 4 * px + 2 * py + pc).whole().wait_recv()
        for cp in sends:
            cp.whole().wait_send()
        mine.whole().wait()

    return pl.pallas_call(
        body, name=name, in_specs=[_ANY], out_specs=_ANY,
        out_shape=jax.ShapeDtypeStruct((8,) + blk_shape, src.dtype),
        scratch_shapes=[pltpu.SemaphoreType.DMA((7,)), pltpu.SemaphoreType.DMA((7,)), pltpu.SemaphoreType.DMA],
    )(src)


def _pair_swap(name, both):
    def body(_, out, send_sem, recv_sem):
        x, y, c = _place()
        remote = lambda s, d: pltpu.make_async_remote_copy(src_ref=s, dst_ref=d, send_sem=send_sem, recv_sem=recv_sem,
                                                           device_id=(x, y, 1 - c), device_id_type=MESH)
        push = _Transfer(remote, out.at[c], out.at[c])
        push.start()
        remote(out.at[c], out.at[1 - c]).wait_recv()
        push.whole().wait_send()

    return pl.pallas_call(
        body, name=name, in_specs=[_ANY], out_specs=_ANY, input_output_aliases={0: 0},
        out_shape=jax.ShapeDtypeStruct(both.shape, both.dtype),
        scratch_shapes=[pltpu.SemaphoreType.DMA, pltpu.SemaphoreType.DMA],
    )(both)


def _sum8(name, parts):
    _, r, c = parts.shape
    tm = min(_row_tile(r), 256)

    def body(p_ref, o_ref):
        o_ref[...] = _sum_blocks(p_ref[...])

    return pl.pallas_call(
        body, name=name, grid=(r // tm,), in_specs=[pl.BlockSpec((8, tm, c), lambda i: (0, i, 0))],
        out_specs=pl.BlockSpec((tm, c), lambda i: (i, 0)), out_shape=jax.ShapeDtypeStruct((r, c), F32),
        compiler_params=_params(("parallel",), 48),
    )(parts)


def _adamw(name, g, w, m, v):
    def fn(gv, wv, mv, vv):
        m2 = ADAM_B1 * mv + (1.0 - ADAM_B1) * gv
        v2 = ADAM_B2 * vv + (1.0 - ADAM_B2) * (gv * gv)
        m_hat = m2 / (1.0 - ADAM_B1 ** ADAM_STEP)
        v_hat = v2 / (1.0 - ADAM_B2 ** ADAM_STEP)
        return -ADAM_LR * (m_hat / (jnp.sqrt(v_hat) + ADAM_EPS) + ADAM_WD * wv), m2, v2
    return _rowwise(name, fn, [g, w, m, v], [(g.shape[1], F32)] * 3)


def _pack(tensors):
    flat = jnp.concatenate([t.reshape(-1).astype(F32) for t in tensors])
    rows = -(-flat.size // (128 * 512)) * 512
    return jnp.pad(flat, (0, rows * 128 - flat.size)).reshape(rows, 128)


def _unpack(buf, shapes):
    flat, out, at = buf.reshape(-1), [], 0
    for sh in shapes:
        n = math.prod(sh)
        out.append(flat[at:at + n].reshape(sh))
        at += n
    return out


_WEIGHTS = ("norm_gains", "ffn_w_gate", "ffn_w_up", "ffn_w_down", "w_in", "w_out", "ssm_lambda_re", "ssm_lambda_im",
            "ssm_b_re", "ssm_b_im", "ssm_c_re", "ssm_c_im", "ssm_d", "ssm_log_dt", "ssm_glu_w", "ssm_glu_b",
            "ssm_out_gain", "dn_conv_w", "dn_a_log", "dn_dt_bias", "dn_norm_gain", "attn_out_gain", "rel_bias")
_MATRICES = ("ffn_w_gate", "ffn_w_up", "ffn_w_down", "w_in", "w_out")
_CUT_SMALL = {"norm_gains": 2, "ssm_glu_w": 1, "dn_conv_w": 2}
_REPLICATED = tuple(n for n in _WEIGHTS if n not in _MATRICES and n not in _CUT_SMALL)


def _sum_blocks(x):
    acc = x[0].astype(F32)
    for i in range(1, x.shape[0]):
        acc = acc + x[i].astype(F32)
    return acc


def _gather_matrix(name, shard, chip, dtype=BF16):
    c = shard.shape[-1]
    buf = _rows_into_block(f"{name}_cast", lambda v: v, [shard.reshape(-1, c)], chip, 4, dtype)
    return _all_gather_xy(f"{name}_gather", buf.reshape(4, 2, -1, c)).reshape((4,) + shard.shape)


def _reduce_matrix(name, g, shard_shape, core):
    c = g.shape[-1]
    pieces = g.reshape(4, 2, -1, c)
    part = _pair_sum(f"{name}_pair_sum", pieces, _pair_send(f"{name}_pair", pieces), core)
    both = _rows_into_block(f"{name}_sum", _sum_blocks, [_exchange_xy(f"{name}_exchange", part)], core, 2, F32)
    return _pair_swap(f"{name}_swap", both).reshape(shard_shape)


def kernel(x, norm_gains, ffn_w_gate, ffn_w_up, ffn_w_down, w_in, w_out, ssm_lambda_re, ssm_lambda_im, ssm_b_re, ssm_b_im, ssm_c_re, ssm_c_im, ssm_d, ssm_log_dt, ssm_glu_w, ssm_glu_b, ssm_out_gain, dn_conv_w, dn_a_log, dn_dt_bias, dn_norm_gain, attn_out_gain, rel_bias, loss_target, m_norm_gains, m_ffn_w_gate, m_ffn_w_up, m_ffn_w_down, m_w_in, m_w_out, m_ssm_lambda_re, m_ssm_lambda_im, m_ssm_b_re, m_ssm_b_im, m_ssm_c_re, m_ssm_c_im, m_ssm_d, m_ssm_log_dt, m_ssm_glu_w, m_ssm_glu_b, m_ssm_out_gain, m_dn_conv_w, m_dn_a_log, m_dn_dt_bias, m_dn_norm_gain, m_attn_out_gain, m_rel_bias, v_norm_gains, v_ffn_w_gate, v_ffn_w_up, v_ffn_w_down, v_w_in, v_w_out, v_ssm_lambda_re, v_ssm_lambda_im, v_ssm_b_re, v_ssm_b_im, v_ssm_c_re, v_ssm_c_im, v_ssm_d, v_ssm_log_dt, v_ssm_glu_w, v_ssm_glu_b, v_ssm_out_gain, v_dn_conv_w, v_dn_a_log, v_dn_dt_bias, v_dn_norm_gain, v_attn_out_gain, v_rel_bias):
    wts = dict(zip(_WEIGHTS, (norm_gains, ffn_w_gate, ffn_w_up, ffn_w_down, w_in, w_out, ssm_lambda_re, ssm_lambda_im, ssm_b_re, ssm_b_im, ssm_c_re, ssm_c_im, ssm_d, ssm_log_dt, ssm_glu_w, ssm_glu_b, ssm_out_gain, dn_conv_w, dn_a_log, dn_dt_bias, dn_norm_gain, attn_out_gain, rel_bias)))
    mom = dict(zip(_WEIGHTS, (m_norm_gains, m_ffn_w_gate, m_ffn_w_up, m_ffn_w_down, m_w_in, m_w_out, m_ssm_lambda_re, m_ssm_lambda_im, m_ssm_b_re, m_ssm_b_im, m_ssm_c_re, m_ssm_c_im, m_ssm_d, m_ssm_log_dt, m_ssm_glu_w, m_ssm_glu_b, m_ssm_out_gain, m_dn_conv_w, m_dn_a_log, m_dn_dt_bias, m_dn_norm_gain, m_attn_out_gain, m_rel_bias)))
    var = dict(zip(_WEIGHTS, (v_norm_gains, v_ffn_w_gate, v_ffn_w_up, v_ffn_w_down, v_w_in, v_w_out, v_ssm_lambda_re, v_ssm_lambda_im, v_ssm_b_re, v_ssm_b_im, v_ssm_c_re, v_ssm_c_im, v_ssm_d, v_ssm_log_dt, v_ssm_glu_w, v_ssm_glu_b, v_ssm_out_gain, v_dn_conv_w, v_dn_a_log, v_dn_dt_bias, v_dn_norm_gain, v_attn_out_gain, v_rel_bias)))
    depth, d_model = norm_gains.shape[0], x.shape[-1]
    chip = 2 * lax.axis_index("x") + lax.axis_index("y")
    chip_idx = chip.astype(jnp.int32).reshape(1)
    core_idx = lax.axis_index("c").astype(jnp.int32).reshape(1)

    w = {n: _gather_matrix(n, wts[n], chip_idx) for n in ("ffn_w_gate", "ffn_w_up", "ffn_w_down", "w_out")}
    w_in_all = _gather_matrix("w_in", w_in, chip_idx)
    w_in_all = jnp.transpose(w_in_all, (1, 2, 0, 3)).reshape(depth, d_model, N_IN)
    w["w_in"] = jnp.pad(w_in_all, ((0, 0), (0, 0), (0, N_IN_PAD - N_IN)))[None]
    cut_names = tuple(_CUT_SMALL)
    cut_all = _gather_matrix("small", _pack([wts[n] for n in cut_names]), chip_idx, F32)
    p = {n: wts[n] for n in _REPLICATED}
    per_chip = [_unpack(cut_all[j], [wts[n].shape for n in cut_names]) for j in range(4)]
    for i, n in enumerate(cut_names):
        p[n] = jnp.concatenate([per_chip[j][i] for j in range(4)], axis=_CUT_SMALL[n])

    loss, dx, grads = _local_step(x[0], loss_target[0], w, p)
    loss = lax.psum(loss, ("x", "y", "c"))

    total = {}
    for n in ("ffn_w_gate", "ffn_w_up", "ffn_w_down", "w_out"):
        total[n] = _reduce_matrix(n, grads[n], wts[n].shape, core_idx)
    g_in = grads["w_in"][:, :, :N_IN].reshape(depth, d_model, 4, N_IN // 4)
    total["w_in"] = _reduce_matrix("w_in", jnp.transpose(g_in, (2, 0, 1, 3)), w_in.shape, core_idx)
    small_names = _REPLICATED + cut_names
    small_sum = _sum8("small_sum", _exchange8("small_exchange", _pack([grads[n] for n in small_names]), same_to_all=True))
    for n, g in zip(small_names, _unpack(small_sum, [grads[n].shape for n in small_names])):
        if n in _CUT_SMALL:
            ax = _CUT_SMALL[n]
            g = lax.dynamic_slice_in_dim(g, chip * wts[n].shape[ax], wts[n].shape[ax], axis=ax)
        total[n] = g

    delta, new_m, new_v = {}, {}, {}
    for n in _MATRICES + cut_names:
        c = wts[n].shape[-1]
        d2, m2, v2 = _adamw(f"{n}_adamw", total[n].reshape(-1, c), wts[n].reshape(-1, c), mom[n].reshape(-1, c),
                            var[n].reshape(-1, c))
        delta[n], new_m[n], new_v[n] = (t.reshape(wts[n].shape) for t in (d2, m2, v2))
    rep_shapes = [wts[n].shape for n in _REPLICATED]
    packed = _adamw("small_adamw", _pack([total[n] for n in _REPLICATED]), _pack([wts[n] for n in _REPLICATED]),
                    _pack([mom[n] for n in _REPLICATED]), _pack([var[n] for n in _REPLICATED]))
    for dst, buf in zip((delta, new_m, new_v), packed):
        dst.update(zip(_REPLICATED, _unpack(buf, rep_shapes)))

    return (loss, dx[None], *[total[n] for n in _WEIGHTS], *[delta[n] for n in _WEIGHTS],
            *[new_m[n] for n in _WEIGHTS], *[new_v[n] for n in _WEIGHTS])
```

```python
import functools
import math

import jax
import jax.numpy as jnp
from jax import lax
from jax.experimental import pallas as pl
from jax.experimental.pallas import tpu as pltpu

F32 = jnp.float32
BF16 = jnp.bfloat16
HI = lax.Precision.HIGHEST
MESH = pl.DeviceIdType.MESH

NORM_EPS = 1e-6
NEG_INF = -1e30
SSM_GROUPS, SSM_CH, SSM_STATE, SSM_WIDTH = 32, 16, 64, 512
SSM_T = 16
DN_HEADS, DN_DIM, DN_WIDTH, DN_CONV, DN_CHUNK = 6, 128, 768, 4, 64
AT_HEADS, AT_DIM, AT_WIDTH, AT_BLK = 6, 128, 768, 128
DILATED = ((128, 1), (512, 4), (2048, 16))
N_BUCKETS, REL_MAX = 32, 2048
N_IN = SSM_WIDTH + 3 * AT_WIDTH + 4 * DN_WIDTH + 2 * DN_HEADS
N_IN_PAD = 6144
COL_AQ, COL_AK, COL_AV, COL_DQKV, COL_DZ, COL_DAB = 512, 1280, 2048, 2816, 5120, 5888
ADAM_LR, ADAM_B1, ADAM_B2, ADAM_EPS, ADAM_WD, ADAM_STEP = 0.001, 0.9, 0.999, 1e-08, 0.01, 10
V7X_VMEM_BYTES = 64 * 1024 * 1024
NN = (((1,), (0,)), ((), ()))
NT = (((1,), (1,)), ((), ()))
TN = (((0,), (0,)), ((), ()))
BNN = (((2,), (1,)), ((0,), (0,)))
BNT = (((2,), (2,)), ((0,), (0,)))
BTN = (((1,), (1,)), ((0,), (0,)))


def _params(sem, vmem_mb=None):
    kw = {}
    if vmem_mb is not None:
        kw["vmem_limit_bytes"] = min(vmem_mb * 1024 * 1024, V7X_VMEM_BYTES - 8 * 1024 * 1024)
    return pltpu.CompilerParams(dimension_semantics=sem, **kw)


def _dotf(a, b, dims=NN):
    return lax.dot_general(a, b, dims, precision=HI, preferred_element_type=F32)


def _dot3(a, b, dims=NN):
    return lax.dot_general(a, b, dims, precision=lax.Precision.HIGH, preferred_element_type=F32)


def _dotb(a, b, dims=NN):
    return lax.dot_general(a.astype(BF16), b.astype(BF16), dims, preferred_element_type=F32)


def _sigmoid(x):
    return 1.0 / (1.0 + jnp.exp(-x))


def _silu(x):
    return x * _sigmoid(x)


def _softplus(x):
    return jnp.maximum(x, 0.0) + jnp.log(1.0 + jnp.exp(-jnp.abs(x)))


def _gelu(x):
    return 0.5 * x * (1.0 + jnp.tanh(math.sqrt(2.0 / math.pi) * (x + 0.044715 * x * x * x)))


def _rms(x, gain):
    return x * lax.rsqrt(jnp.mean(x * x, axis=-1, keepdims=True) + NORM_EPS) * gain


def _row_tile(s):
    for t in (512, 256, 128, 64, 32, 16, 8):
        if s % t == 0:
            return t
    return s


def _mm(name, pairs, *, grid, a_blk, a_map, b_blk, b_map, o_shape, o_blk, o_map, dims, out_dtype=F32, vmem_mb=48,
        into=None):
    n_red = grid[-1]
    n_pairs = len(pairs)
    n_in = 2 * n_pairs + (into is not None)
    acc_shape = tuple(d for d in o_blk if d is not None)

    def body(*refs):
        ins, o_ref, scr = refs[:2 * n_pairs], refs[n_in], refs[n_in + 1:]
        part = _dotb(ins[0][...], ins[1][...], dims)
        for p in range(1, n_pairs):
            part = part + _dotb(ins[2 * p][...], ins[2 * p + 1][...], dims)
        if n_red == 1:
            o_ref[...] = part.astype(o_ref.dtype)
        else:
            acc = scr[0]
            r = pl.program_id(len(grid) - 1)

            @pl.when(r == 0)
            def _():
                acc[...] = part

            @pl.when(r > 0)
            def _():
                acc[...] += part

            @pl.when(r == n_red - 1)
            def _():
                o_ref[...] = acc[...].astype(o_ref.dtype)

    in_specs, args = [], []
    for a, b in pairs:
        in_specs += [pl.BlockSpec(a_blk, a_map), pl.BlockSpec(b_blk, b_map)]
        args += [a, b]
    if into is not None:
        in_specs.append(pl.BlockSpec(memory_space=pl.ANY))
        args.append(into)
    return pl.pallas_call(
        body, name=name, grid=grid, in_specs=in_specs, out_specs=pl.BlockSpec(o_blk, o_map),
        out_shape=jax.ShapeDtypeStruct(o_shape, out_dtype),
        input_output_aliases={} if into is None else {n_in - 1: 0},
        scratch_shapes=[pltpu.VMEM(acc_shape, F32)] if n_red > 1 else [],
        compiler_params=_params(("parallel",) * (len(grid) - 1) + ("arbitrary",), vmem_mb),
    )(*args)


def _col_tile(n):
    for t in (1536, 1408, 1024, 768, 512, 384, 256, 128):
        if n % t == 0:
            return t
    return n


def _mm_cols(name, a, w, widx, out_dtype):
    s, k = a.shape
    j_n, nj = w.shape[0], w.shape[-1]
    tm, tn = _row_tile(s), _col_tile(nj)
    nt = nj // tn
    lead = (None,) * (1 + len(widx))
    return _mm(name, [(a, w)], grid=(j_n, nt, s // tm, 1),
               a_blk=(tm, k), a_map=lambda j, c, i, r: (i, 0),
               b_blk=lead + (k, tn), b_map=lambda j, c, i, r: (j, *widx, 0, c),
               o_shape=(s, j_n * nj), o_blk=(tm, tn), o_map=lambda j, c, i, r: (i, j * nt + c),
               dims=NN, out_dtype=out_dtype)


def _mm_rows(name, a, w, widx, out_dtype):
    s = a.shape[0]
    j_n, kj, n = w.shape[0], w.shape[-2], w.shape[-1]
    tm = _row_tile(s)
    lead = (None,) * (1 + len(widx))
    return _mm(name, [(a, w)], grid=(s // tm, j_n),
               a_blk=(tm, kj), a_map=lambda i, j: (i, j),
               b_blk=lead + (kj, n), b_map=lambda i, j: (j, *widx, 0, 0),
               o_shape=(s, n), o_blk=(tm, n), o_map=lambda i, j: (i, 0), dims=NN, out_dtype=out_dtype)


def _mm_rows_t(name, a, w, widx, out_dtype):
    s, n = a.shape
    j_n, kj = w.shape[0], w.shape[-2]
    tm = _row_tile(s)
    lead = (None,) * (1 + len(widx))
    return _mm(name, [(a, w)], grid=(j_n, s // tm, 1),
               a_blk=(tm, n), a_map=lambda j, i, r: (i, 0),
               b_blk=lead + (kj, n), b_map=lambda j, i, r: (j, *widx, 0, 0),
               o_shape=(s, j_n * kj), o_blk=(tm, kj), o_map=lambda j, i, r: (i, j), dims=NT, out_dtype=out_dtype)


def _mm_cols_t(name, pairs, widx, out_dtype):
    a0, w0 = pairs[0]
    s = a0.shape[0]
    j_n, k, nj = w0.shape[0], w0.shape[-2], w0.shape[-1]
    tm, tn = _row_tile(s), _col_tile(nj)
    nt = nj // tn
    lead = (None,) * (1 + len(widx))
    return _mm(name, pairs, grid=(s // tm, j_n * nt),
               a_blk=(tm, tn), a_map=lambda i, r: (i, r),
               b_blk=lead + (k, tn), b_map=lambda i, r: (lax.div(r, nt), *widx, 0, lax.rem(r, nt)),
               o_shape=(s, k), o_blk=(tm, k), o_map=lambda i, r: (i, 0), dims=NT, out_dtype=out_dtype)


def _mm_grad(name, a, b, a_cols, b_cols, out_dtype=BF16, lead=(), lead_dims=(), into=None):
    s = a.shape[0]
    tm = _row_tile(s)
    if a_cols is not None:
        j_n, ka, nb = a_cols, a.shape[1] // a_cols, b.shape[1]
        tk, tn = ka, _col_tile(nb)
        a_map = lambda j, kb, c, i: (i, j)
        b_map = lambda j, kb, c, i: (i, c)
    else:
        j_n, ka, nb = b_cols, a.shape[1], b.shape[1] // b_cols
        tk, tn = min(ka, 1024), _col_tile(nb)
        nt_ = nb // tn
        a_map = lambda j, kb, c, i: (i, kb)
        b_map = lambda j, kb, c, i: (i, j * nt_ + c)
    return _mm(name, [(a, b)], grid=(j_n, ka // tk, nb // tn, s // tm),
               a_blk=(tm, tk), a_map=a_map, b_blk=(tm, tn), b_map=b_map,
               o_shape=(j_n, *lead_dims, ka, nb), o_blk=(None,) * (1 + len(lead)) + (tk, tn),
               o_map=lambda j, kb, c, i: (j, *lead, kb, c), dims=TN, out_dtype=out_dtype, into=into)


def _rowwise(name, fn, rows, outs, *, bcast=(), accs=(), tm=None, vmem_mb=48):
    rows = [r if isinstance(r, tuple) else (r, r.shape[1], 0) for r in rows]
    s = rows[0][0].shape[0]
    tm = tm or min(_row_tile(s), 256)
    nr, nb, no = len(rows), len(bcast), len(outs)

    def body(*refs):
        o_refs, a_refs = refs[nr + nb:nr + nb + no], refs[nr + nb + no:]
        res = fn(*[r[...] for r in refs[:nr + nb]])
        if not isinstance(res, (tuple, list)):
            res = (res,)
        for o, v in zip(o_refs, res[:no]):
            o[...] = v.astype(o.dtype)
        if a_refs:
            i = pl.program_id(0)
            for a, v in zip(a_refs, res[no:]):
                @pl.when(i == 0)
                def _(a=a, v=v):
                    a[...] = v

                @pl.when(i > 0)
                def _(a=a, v=v):
                    a[...] += v

    in_specs = [pl.BlockSpec((tm, w), lambda i, c=c: (i, c)) for _, w, c in rows]
    in_specs += [pl.BlockSpec(b.shape, lambda i, nd=b.ndim: (0,) * nd) for b in bcast]
    out_specs = [pl.BlockSpec((tm, c), lambda i: (i, 0)) for c, _ in outs]
    out_specs += [pl.BlockSpec(sh, lambda i, nd=len(sh): (0,) * nd) for sh in accs]
    out_shape = [jax.ShapeDtypeStruct((s, c), dt) for c, dt in outs] + [jax.ShapeDtypeStruct(sh, F32) for sh in accs]
    res = pl.pallas_call(
        body, name=name, grid=(s // tm,), in_specs=in_specs, out_specs=out_specs, out_shape=out_shape,
        compiler_params=_params(("arbitrary",) if accs else ("parallel",), vmem_mb),
    )(*[r[0] for r in rows], *bcast)
    return res


def _rms_fwd(name, x, gain):
    return _rowwise(name, lambda xv, g: _rms(xv, g), [x], [(x.shape[1], BF16)], bcast=[gain])[0]


def _rms_residual(name, x, f, gain, scale):
    return _rowwise(name, lambda xv, fv, g: xv + scale * _rms(fv, g), [x, f], [(x.shape[1], F32)], bcast=[gain])[0]


def _rms_bwd_math(dy, x, gain):
    r = lax.rsqrt(jnp.mean(x * x, axis=-1, keepdims=True) + NORM_EPS)
    xh = x * r
    dxh = dy * gain
    dx = r * (dxh - xh * jnp.mean(dxh * xh, axis=-1, keepdims=True))
    return dx, jnp.sum(dy * xh, axis=0, keepdims=True)


def _rms_bwd(name, dy, x, gain, scale=1.0, residual=None, out_dtype=F32):
    d = x.shape[1]
    if residual is None:
        fn = lambda dyv, xv, g: _rms_bwd_math(scale * dyv.astype(F32), xv, g)
        rows = [dy, x]
    else:
        def fn(dyv, xv, rv, g):
            dx, dg = _rms_bwd_math(scale * dyv.astype(F32), xv, g)
            return dx + rv, dg
        rows = [dy, x, residual]
    return _rowwise(name, fn, rows, [(d, out_dtype)], bcast=[gain], accs=[(1, d)])


def _loss_and_grad(y, target):
    d = y.shape[1]

    def fn(yv, tv):
        e = yv - tv
        part = 0.5 * jnp.sum(jnp.mean(e * e, axis=-1, keepdims=True), axis=0, keepdims=True)
        return e * (1.0 / d), jnp.broadcast_to(part, (1, 128))
    dy, loss = _rowwise("loss_head", fn, [y, target], [(d, F32)], accs=[(1, 128)])
    return loss[0, 0], dy


def _s5_operators(lam_re, lam_im, b_re, b_im, c_re, c_im, d_skip, log_dt):
    t_n, ch, p_n = SSM_T, SSM_CH, SSM_STATE
    dt = jnp.exp(log_dt)[:, None]
    ld_re, ld_im = lam_re * dt, lam_im * dt
    k = jnp.arange(t_n + 1, dtype=F32)[None, :, None]
    mag = jnp.exp(ld_re[:, None, :] * k)
    pw_re, pw_im = mag * jnp.cos(ld_im[:, None, :] * k), mag * jnp.sin(ld_im[:, None, :] * k)
    lb_re, lb_im = pw_re[:, 1], pw_im[:, 1]
    den = lam_re * lam_re + lam_im * lam_im
    f_re = ((lb_re - 1.0) * lam_re + lb_im * lam_im) / den
    f_im = (lb_im * lam_re - (lb_re - 1.0) * lam_im) / den
    bb_re = f_re[..., None] * b_re - f_im[..., None] * b_im
    bb_im = f_re[..., None] * b_im + f_im[..., None] * b_re
    cp_re = c_re[:, None] * pw_re[:, :t_n, None, :] - c_im[:, None] * pw_im[:, :t_n, None, :]
    cp_im = c_re[:, None] * pw_im[:, :t_n, None, :] + c_im[:, None] * pw_re[:, :t_n, None, :]
    taps = (jnp.einsum("gtcp,gpd->gtcd", cp_re, bb_re, precision=HI)
            - jnp.einsum("gtcp,gpd->gtcd", cp_im, bb_im, precision=HI))
    m5 = jnp.stack([jnp.pad(taps[:, :t_n - j], ((0, 0), (j, 0), (0, 0), (0, 0))) for j in range(t_n)],
                   axis=1)
    m_op = jnp.transpose(m5, (0, 1, 4, 2, 3)).reshape(SSM_GROUPS, t_n * ch, t_n * ch)
    m_op = m_op + jnp.eye(t_n * ch, dtype=F32)[None] * jnp.tile(d_skip.reshape(SSM_GROUPS, 1, ch), (1, t_n, 1)).reshape(
        SSM_GROUPS, 1, t_n * ch)
    rv_re, rv_im = pw_re[:, t_n - 1::-1][:, :t_n], pw_im[:, t_n - 1::-1][:, :t_n]
    bo_re = rv_re[:, :, None, :] * jnp.swapaxes(bb_re, 1, 2)[:, None] - rv_im[:, :, None, :] * jnp.swapaxes(bb_im, 1, 2)[:, None]
    bo_im = rv_re[:, :, None, :] * jnp.swapaxes(bb_im, 1, 2)[:, None] + rv_im[:, :, None, :] * jnp.swapaxes(bb_re, 1, 2)[:, None]
    b_op = jnp.concatenate([bo_re, bo_im], axis=-1).reshape(SSM_GROUPS, t_n * ch, 2 * p_n)
    q_re = c_re[:, None] * pw_re[:, 1:, None, :] - c_im[:, None] * pw_im[:, 1:, None, :]
    q_im = c_re[:, None] * pw_im[:, 1:, None, :] + c_im[:, None] * pw_re[:, 1:, None, :]
    c_op = jnp.concatenate([q_re, -q_im], axis=-1).reshape(SSM_GROUPS, t_n * ch, 2 * p_n)
    c_op = jnp.swapaxes(c_op, 1, 2)
    a1 = jnp.concatenate([pw_re[:, t_n], pw_re[:, t_n]], axis=-1)[:, None, :]
    a2 = jnp.concatenate([-pw_im[:, t_n], pw_im[:, t_n]], axis=-1)[:, None, :]
    return m_op, b_op, c_op, a1, a2


def _s5_groups(name, fn, ins, out_dims):
    g_n = ins[0].shape[0]
    blk = lambda a, b: pl.BlockSpec((None, a, b), lambda g: (g, 0, 0))

    def body(*refs):
        res = fn(*[r[...] for r in refs[:len(ins)]])
        for o, v in zip(refs[len(ins):], res):
            o[...] = v

    return pl.pallas_call(
        body, name=name, grid=(g_n,), in_specs=[blk(*a.shape[1:]) for a in ins], out_specs=[blk(*d) for d in out_dims],
        out_shape=[jax.ShapeDtypeStruct((g_n,) + tuple(d), F32) for d in out_dims],
        compiler_params=_params(("parallel",)),
    )(*ins)


def _s5_state_scan(z_t, a1, a2):
    nc, g_n, p2 = z_t.shape

    def body(z_ref, a1_ref, a2_ref, s_ref):
        a1v, a2v = a1_ref[...], a2_ref[...]

        def step(n, s):
            s_ref[n] = s
            return a1v * s + a2v * pltpu.roll(s, SSM_STATE, 1) + z_ref[n]

        lax.fori_loop(0, nc, step, jnp.zeros((g_n, p2), F32))

    return pl.pallas_call(body, name="s5_state_scan", out_shape=jax.ShapeDtypeStruct(z_t.shape, F32))(z_t, a1, a2)


def _s5_state_scan_bwd(ds_t, s_t, a1, a2):
    nc, g_n, p2 = s_t.shape

    def body(ds_ref, s_ref, a1_ref, a2_ref, dz_ref, da1_ref, da2_ref):
        a1v, a2v = a1_ref[...], a2_ref[...]

        def step(k, carry):
            g, d1, d2 = carry
            n = nc - 1 - k
            dz_ref[n] = g
            sn = s_ref[n]
            d1 = d1 + g * sn
            d2 = d2 + g * pltpu.roll(sn, SSM_STATE, 1)
            g = ds_ref[n] + a1v * g + pltpu.roll(a2v * g, SSM_STATE, 1)
            return g, d1, d2

        zero = jnp.zeros((g_n, p2), F32)
        _, d1, d2 = lax.fori_loop(0, nc, step, (zero, zero, zero))
        da1_ref[...] = d1
        da2_ref[...] = d2

    row = jax.ShapeDtypeStruct((g_n, p2), F32)
    return pl.pallas_call(body, name="s5_state_scan_bwd",
                          out_shape=[jax.ShapeDtypeStruct(s_t.shape, F32), row, row])(ds_t, s_t, a1, a2)


def _s5_scan_fwd(u_g, m_op, b_op, c_op, a1, a2):
    _, nc, w = u_g.shape
    p2 = 2 * SSM_STATE
    z_g, = _s5_groups("s5_chunk_inputs", lambda u, b: (_dotf(u, b),), [u_g, b_op], [(nc, p2)])
    s_g = jnp.swapaxes(_s5_state_scan(jnp.swapaxes(z_g, 0, 1), a1[:, 0], a2[:, 0]), 0, 1)
    y_g, = _s5_groups("s5_outputs", lambda u, s, m, c: (_dotf(u, m) + _dotf(s, c),), [u_g, s_g, m_op, c_op], [(nc, w)])
    return y_g, s_g


def _s5_scan_bwd(dy_g, u_g, s_g, m_op, b_op, c_op, a1, a2):
    _, nc, w = u_g.shape
    p2 = 2 * SSM_STATE
    ds_g, = _s5_groups("s5_dstate", lambda dy, c: (_dotf(dy, c, NT),), [dy_g, c_op], [(nc, p2)])
    dz_t, da1, da2 = _s5_state_scan_bwd(jnp.swapaxes(ds_g, 0, 1), jnp.swapaxes(s_g, 0, 1), a1[:, 0], a2[:, 0])
    dz_g = jnp.swapaxes(dz_t, 0, 1)

    def grads(dy, dz, u, s, m, b):
        return _dotf(dy, m, NT) + _dotf(dz, b, NT), _dotf(u, dy, TN), _dotf(u, dz, TN), _dotf(s, dy, TN)

    du, dm, db, dc = _s5_groups("s5_grads", grads, [dy_g, dz_g, u_g, s_g, m_op, b_op], [(nc, w), (w, w), (w, p2), (p2, w)])
    return du, dm, db, dc, da1[:, None], da2[:, None]


def _to_groups(u):
    s = u.shape[0]
    return u.reshape(s // SSM_T, SSM_T, SSM_GROUPS, SSM_CH).transpose(2, 0, 1, 3).reshape(
        SSM_GROUPS, s // SSM_T, SSM_T * SSM_CH)


def _from_groups(y_g):
    nc = y_g.shape[1]
    return y_g.reshape(SSM_GROUPS, nc, SSM_T, SSM_CH).transpose(1, 2, 0, 3).reshape(nc * SSM_T, SSM_WIDTH)


def _s5_post_math(y, glu_w, glu_b, gain):
    y2 = _gelu(y)
    o = y2 * _sigmoid(_dotb(y2, glu_w) + glu_b)
    return _rms(o, gain)


def _s5_post_fwd(y, glu_w, glu_b, gain):
    return _rowwise("s5_post_fwd", _s5_post_math, [y], [(SSM_WIDTH, F32)], bcast=[glu_w, glu_b, gain])[0]


def _s5_post_bwd(dout, y, glu_w, glu_b, gain):
    def fn(dv, yv, w, b, g):
        _, vjp = jax.vjp(_s5_post_math, yv, w, b, g)
        return vjp(dv)
    return _rowwise("s5_post_bwd", fn, [dout, y], [(SSM_WIDTH, F32)], bcast=[glu_w, glu_b, gain],
                    accs=[(SSM_WIDTH, SSM_WIDTH), (1, SSM_WIDTH), (1, SSM_WIDTH)])


def _t5_bucket(dist):
    max_exact = N_BUCKETS // 2
    d = jnp.maximum(dist, 1).astype(F32)
    large = max_exact + jnp.log(d / max_exact) / math.log(REL_MAX / max_exact) * (N_BUCKETS - max_exact)
    large = jnp.minimum(large.astype(jnp.int32), N_BUCKETS - 1)
    return jnp.where(dist < max_exact, dist, large)


def _attn_bias_tables(rel_bias):
    blk = AT_BLK
    tabs = []
    for window, dil in DILATED:
        rel = blk + jnp.arange(blk)[:, None] - jnp.arange(2 * blk)[None, :]
        valid = (rel >= 0) & (rel <= window // dil)
        bias = jnp.moveaxis(rel_bias[_t5_bucket(jnp.maximum(rel, 0) * dil)], -1, 0)
        tabs.append(jnp.where(valid[None], bias, NEG_INF))
    return jnp.stack(tabs)


def _attn_specs(s):
    col = lambda first: pl.BlockSpec((s, AT_DIM), lambda h, r: (0, first // AT_DIM + h))
    return col(COL_AQ), col(COL_AK), col(COL_AV), col(0)


def _attn_branch_fwd(b_idx, dil, proj, bias):
    s, blk, dh = proj.shape[0], AT_BLK, AT_DIM
    nb = s // dil // blk
    scale = dh ** -0.5

    def body(q_ref, k_ref, v_ref, b_ref, o_ref, l_ref):
        r = pl.program_id(1)

        def block(n, carry):
            cur = pl.ds(r + n * (blk * dil), blk, stride=dil)
            prv = pl.ds(r + jnp.maximum(n - 1, 0) * (blk * dil), blk, stride=dil)
            q = q_ref[cur, :] * scale
            lc = _dotb(q, k_ref[cur, :], NT) + b_ref[:, blk:]
            lp = _dotb(q, k_ref[prv, :], NT) + b_ref[:, :blk]
            lp = jnp.where(n > 0, lp, NEG_INF)
            m = jnp.maximum(jnp.max(lc, axis=1, keepdims=True), jnp.max(lp, axis=1, keepdims=True))
            pc, pp = jnp.exp(lc - m), jnp.exp(lp - m)
            den = jnp.sum(pc, axis=1, keepdims=True) + jnp.sum(pp, axis=1, keepdims=True)
            inv = 1.0 / den
            o_ref[cur, :] = _dotb(pc * inv, v_ref[cur, :]) + _dotb(pp * inv, v_ref[prv, :])
            l_ref[cur, :] = jnp.broadcast_to(m + jnp.log(den), (blk, dh))
            return carry

        lax.fori_loop(0, nb, block, 0)

    q_s, k_s, v_s, out_s = _attn_specs(s)
    return pl.pallas_call(
        body, name=f"attn_fwd_d{dil}", grid=(AT_HEADS, dil),
        in_specs=[q_s, k_s, v_s, pl.BlockSpec((None, None, blk, 2 * blk), lambda h, r: (b_idx, h, 0, 0))],
        out_specs=[out_s, out_s],
        out_shape=[jax.ShapeDtypeStruct((s, AT_WIDTH), F32)] * 2,
        compiler_params=_params(("parallel", "arbitrary")),
    )(proj, proj, proj, bias)


def _attn_branch_bwd(b_idx, dil, proj, do, lse, dlt, bias):
    s, blk, dh = proj.shape[0], AT_BLK, AT_DIM
    nb = s // dil // blk
    scale = dh ** -0.5

    def body(q_ref, k_ref, v_ref, do_ref, l_ref, d_ref, b_ref, dq_ref, dk_ref, dv_ref, db_ref):
        r = pl.program_id(1)

        @pl.when(r == 0)
        def _():
            dk_ref[...] = jnp.zeros_like(dk_ref)
            dv_ref[...] = jnp.zeros_like(dv_ref)
            db_ref[...] = jnp.zeros_like(db_ref)

        def block(n, carry):
            cur = pl.ds(r + n * (blk * dil), blk, stride=dil)
            prv = pl.ds(r + jnp.maximum(n - 1, 0) * (blk * dil), blk, stride=dil)
            q = q_ref[cur, :] * scale
            do_b = do_ref[cur, :]
            lse_b = l_ref[cur, :][:, :1]
            dlt_b = d_ref[cur, :][:, :1]
            kc, kp = k_ref[cur, :], k_ref[prv, :]
            vc, vp = v_ref[cur, :], v_ref[prv, :]
            lc = _dotb(q, kc, NT) + b_ref[:, blk:]
            lp = jnp.where(n > 0, _dotb(q, kp, NT) + b_ref[:, :blk], NEG_INF)
            pc, pp = jnp.exp(lc - lse_b), jnp.exp(lp - lse_b)
            dsc = pc * (_dotb(do_b, vc, NT) - dlt_b)
            dsp = pp * (_dotb(do_b, vp, NT) - dlt_b)
            dq_ref[cur, :] = (_dotb(dsc, kc) + _dotb(dsp, kp)) * scale
            dk_ref[cur, :] = dk_ref[cur, :] + _dotb(dsc, q, TN)
            dk_ref[prv, :] = dk_ref[prv, :] + _dotb(dsp, q, TN)
            dv_ref[cur, :] = dv_ref[cur, :] + _dotb(pc, do_b, TN)
            dv_ref[prv, :] = dv_ref[prv, :] + _dotb(pp, do_b, TN)
            db_ref[:, blk:] += dsc
            db_ref[:, :blk] += dsp
            return carry

        lax.fori_loop(0, nb, block, 0)

    q_s, k_s, v_s, out_s = _attn_specs(s)
    tab = pl.BlockSpec((None, None, blk, 2 * blk), lambda h, r: (b_idx, h, 0, 0))
    return pl.pallas_call(
        body, name=f"attn_bwd_d{dil}", grid=(AT_HEADS, dil),
        in_specs=[q_s, k_s, v_s, out_s, out_s, out_s, tab],
        out_specs=[out_s, out_s, out_s, pl.BlockSpec((None, blk, 2 * blk), lambda h, r: (h, 0, 0))],
        out_shape=[jax.ShapeDtypeStruct((s, AT_WIDTH), F32)] * 3 + [jax.ShapeDtypeStruct((AT_HEADS, blk, 2 * blk), F32)],
        compiler_params=_params(("parallel", "arbitrary"), 56),
    )(proj, proj, proj, do, lse, dlt, bias)


def _per_head(fn, *xs):
    return jnp.concatenate([fn(*[x[:, h * AT_DIM:(h + 1) * AT_DIM] for x in xs]) for h in range(AT_HEADS)], axis=1)


def _attn_merge_math(o1, o2, o3, l1, l2, l3, gain):
    m = jnp.maximum(jnp.maximum(l1, l2), l3)
    e1, e2, e3 = jnp.exp(l1 - m), jnp.exp(l2 - m), jnp.exp(l3 - m)
    den = e1 + e2 + e3
    o = (e1 * o1 + e2 * o2 + e3 * o3) / den
    return _rms(o, gain), o, m + jnp.log(den)


def _attn_merge_fwd(os_, ls_, gain):
    w = AT_WIDTH
    return _rowwise("attn_merge_fwd", _attn_merge_math, [*os_, *ls_], [(w, F32)] * 3, bcast=[gain])


def _attn_merge_bwd(dy, o, gain):
    def fn(dyv, ov, g):
        do, dg = _rms_bwd_math(dyv, ov, g)
        dlt = _per_head(lambda a, b: jnp.broadcast_to(jnp.sum(a * b, axis=1, keepdims=True), a.shape), do, ov)
        return do, dlt, dg
    return _rowwise("attn_merge_bwd", fn, [dy, o], [(AT_WIDTH, F32)] * 2, bcast=[gain], accs=[(1, AT_WIDTH)])


def _add3(name, a, b, c):
    return _rowwise(name, lambda x, y, z: x + y + z, [a, b, c], [(a.shape[1], F32)])[0]


def _conv_taps(x, w):
    row = lax.broadcasted_iota(jnp.int32, x.shape, 0)
    y = w[DN_CONV - 1:DN_CONV, :] * x
    for sh in range(1, DN_CONV):
        y = y + w[DN_CONV - 1 - sh:DN_CONV - sh, :] * jnp.where(row >= sh, pltpu.roll(x, sh, 0), 0.0)
    return y


def _gdn_prep_fwd(proj, conv_w):
    s = proj.shape[0]
    ncb = 3 * DN_HEADS
    c0 = COL_DQKV // 128

    def body(x_ref, w_ref, o_ref):
        o_ref[...] = _silu(_conv_taps(x_ref[...], w_ref[...]))

    return pl.pallas_call(
        body, name="gdn_prep_fwd", grid=(ncb,),
        in_specs=[pl.BlockSpec((s, 128), lambda c: (0, c0 + c)), pl.BlockSpec((DN_CONV, 128), lambda c: (0, c))],
        out_specs=pl.BlockSpec((None, s, 128), lambda c: (c, 0, 0)),
        out_shape=jax.ShapeDtypeStruct((ncb, s, 128), F32),
        compiler_params=_params(("parallel",)),
    )(proj, conv_w)


def _gdn_prep_bwd(dact, proj, conv_w):
    s = proj.shape[0]
    ncb = 3 * DN_HEADS
    c0 = COL_DQKV // 128

    def body(d_ref, x_ref, w_ref, dx_ref, dw_ref):
        x, w = x_ref[...], w_ref[...]
        pre = _conv_taps(x, w)
        sg = _sigmoid(pre)
        dpre = d_ref[...] * sg * (1.0 + pre * (1.0 - sg))
        row = lax.broadcasted_iota(jnp.int32, x.shape, 0)
        dx = w[DN_CONV - 1:DN_CONV, :] * dpre
        dw_ref[pl.ds(DN_CONV - 1, 1), :] = jnp.sum(dpre * x, axis=0, keepdims=True)
        for sh in range(1, DN_CONV):
            dx = dx + w[DN_CONV - 1 - sh:DN_CONV - sh, :] * jnp.where(row < s - sh, pltpu.roll(dpre, s - sh, 0), 0.0)
            dw_ref[pl.ds(DN_CONV - 1 - sh, 1), :] = jnp.sum(
                dpre * jnp.where(row >= sh, pltpu.roll(x, sh, 0), 0.0), axis=0, keepdims=True)
        dx_ref[...] = dx

    return pl.pallas_call(
        body, name="gdn_prep_bwd", grid=(ncb,),
        in_specs=[pl.BlockSpec((None, s, 128), lambda c: (c, 0, 0)), pl.BlockSpec((s, 128), lambda c: (0, c0 + c)),
                  pl.BlockSpec((DN_CONV, 128), lambda c: (0, c))],
        out_specs=[pl.BlockSpec((s, 128), lambda c: (0, c)), pl.BlockSpec((DN_CONV, 128), lambda c: (0, c))],
        out_shape=[jax.ShapeDtypeStruct((s, ncb * 128), F32), jax.ShapeDtypeStruct((DN_CONV, ncb * 128), F32)],
        compiler_params=_params(("parallel",)),
    )(dact, proj, conv_w)


def _gates_math(ab, alog, dtb):
    lane = lax.broadcasted_iota(jnp.int32, ab.shape, 1)
    g = -jnp.exp(alog) * _softplus(ab + dtb)
    return jnp.where(lane < DN_HEADS, g, jnp.where(lane < 2 * DN_HEADS, _sigmoid(ab), 0.0))


def _gates_fwd(proj, alog, dtb):
    return _rowwise("gdn_gates_fwd", _gates_math, [(proj, 128, COL_DAB // 128)], [(128, F32)], bcast=[alog, dtb])[0]


def _gates_bwd(dgates, proj, alog, dtb):
    def fn(dv, ab, a, d):
        _, vjp = jax.vjp(_gates_math, ab, a, d)
        return vjp(dv)
    return _rowwise("gdn_gates_bwd", fn, [dgates, (proj, 128, COL_DAB // 128)], [(128, F32)], bcast=[alog, dtb],
                    accs=[(1, 128), (1, 128)])


def _l2n(x):
    return x * lax.rsqrt(jnp.sum(x * x, axis=-1, keepdims=True) + NORM_EPS)


def _gdn_intra_math(q, k, v, gcol, grow, bcol):
    c = DN_CHUNK
    ii = lax.broadcasted_iota(jnp.int32, (1, c, c), 1)
    jj = lax.broadcasted_iota(jnp.int32, (1, c, c), 2)
    gc_col = jnp.sum(jnp.where(ii >= jj, grow, 0.0), axis=2, keepdims=True)
    gc_row = jnp.sum(jnp.where(ii <= jj, gcol, 0.0), axis=1, keepdims=True)
    gc_last = jnp.sum(gcol, axis=1, keepdims=True)
    decay = jnp.exp(jnp.where(ii >= jj, gc_col - gc_row, NEG_INF))
    qn = _l2n(q) * (DN_DIM ** -0.5)
    kn = _l2n(k)
    kb = kn * bcol
    a_mat = jnp.where(ii > jj, _dot3(kb, kn, BNT) * decay, 0.0)
    nil = -a_mat
    t_inv = jnp.where(ii == jj, 1.0, 0.0) + nil
    for _ in range(5):
        nil = _dot3(nil, nil, BNN)
        t_inv = t_inv + _dot3(t_inv, nil, BNN)
    e_col = jnp.exp(gc_col)
    u = _dot3(t_inv, v * bcol, BNN)
    w = _dot3(t_inv, kb * e_col, BNN)
    attn = _dot3(qn, kn, BNT) * decay
    return u, w, attn, qn * e_col, kn * jnp.exp(gc_last - gc_col), jnp.broadcast_to(jnp.exp(gc_last), (DN_HEADS, 1, 128))


def _gdn_specs(s):
    nc = s // DN_CHUNK
    h, c, d = DN_HEADS, DN_CHUNK, DN_DIM
    return dict(
        qkv=pl.BlockSpec((3 * h, c, d), lambda n: (0, n, 0)),
        hcd=pl.BlockSpec((h, c, d), lambda n: (0, n, 0)),
        col=pl.BlockSpec((h, c, 1), lambda n: (0, n, 0)),
        row=pl.BlockSpec((h, None, 1, c), lambda n: (0, n, 0, 0)),
        att=pl.BlockSpec((h, c, c), lambda n: (0, n, 0)),
        dec=pl.BlockSpec((h, None, 1, 128), lambda n: (0, n, 0, 0)),
        s_hcd=jax.ShapeDtypeStruct((h, s, d), F32), s_col=jax.ShapeDtypeStruct((h, s, 1), F32),
        s_row=jax.ShapeDtypeStruct((h, nc, 1, c), F32), s_att=jax.ShapeDtypeStruct((h, s, c), F32),
        s_dec=jax.ShapeDtypeStruct((h, nc, 1, 128), F32), s_qkv=jax.ShapeDtypeStruct((3 * h, s, d), F32),
    )


def _gdn_intra_fwd(act, gcol, grow, bcol):
    s = act.shape[1]
    sp = _gdn_specs(s)
    h = DN_HEADS

    def body(a_ref, gc_ref, gr_ref, bc_ref, u_ref, w_ref, at_ref, qd_ref, kt_ref, dec_ref):
        outs = _gdn_intra_math(a_ref[0:h], a_ref[h:2 * h], a_ref[2 * h:3 * h], gc_ref[...], gr_ref[...], bc_ref[...])
        for ref, val in zip((u_ref, w_ref, at_ref, qd_ref, kt_ref, dec_ref), outs):
            ref[...] = val

    return pl.pallas_call(
        body, name="gdn_intra_fwd", grid=(s // DN_CHUNK,),
        in_specs=[sp["qkv"], sp["col"], sp["row"], sp["col"]],
        out_specs=[sp["hcd"], sp["hcd"], sp["att"], sp["hcd"], sp["hcd"], sp["dec"]],
        out_shape=[sp["s_hcd"], sp["s_hcd"], sp["s_att"], sp["s_hcd"], sp["s_hcd"], sp["s_dec"]],
        compiler_params=_params(("parallel",)),
    )(act, gcol, grow, bcol)


def _gdn_intra_bwd(act, gcol, grow, bcol, du, dw, dattn, dqd, dkt, ddec):
    s = act.shape[1]
    sp = _gdn_specs(s)
    h = DN_HEADS

    def body(a_ref, gc_ref, gr_ref, bc_ref, du_ref, dw_ref, dat_ref, dqd_ref, dkt_ref, dde_ref,
             dact_ref, dgc_ref, dgr_ref, dbc_ref):
        _, vjp = jax.vjp(_gdn_intra_math, a_ref[0:h], a_ref[h:2 * h], a_ref[2 * h:3 * h],
                         gc_ref[...], gr_ref[...], bc_ref[...])
        dq, dk, dv, dgc, dgr, dbc = vjp((du_ref[...], dw_ref[...], dat_ref[...], dqd_ref[...], dkt_ref[...], dde_ref[...]))
        dact_ref[0:h] = dq
        dact_ref[h:2 * h] = dk
        dact_ref[2 * h:3 * h] = dv
        dgc_ref[...] = dgc
        dgr_ref[...] = dgr
        dbc_ref[...] = dbc

    return pl.pallas_call(
        body, name="gdn_intra_bwd", grid=(s // DN_CHUNK,),
        in_specs=[sp["qkv"], sp["col"], sp["row"], sp["col"], sp["hcd"], sp["hcd"], sp["att"], sp["hcd"], sp["hcd"], sp["dec"]],
        out_specs=[sp["qkv"], sp["col"], sp["row"], sp["col"]],
        out_shape=[sp["s_qkv"], sp["s_col"], sp["s_row"], sp["s_col"]],
        compiler_params=_params(("parallel",)),
    )(act, gcol, grow, bcol, du, dw, dattn, dqd, dkt, ddec)


def _gdn_step_math(state, u, w, attn, qd, kt, dec):
    v_new = u - _dot3(w, state, BNN)
    o = _dot3(qd, state, BNN) + _dot3(attn, v_new, BNN)
    return state * dec[:, :, :1] + _dot3(kt, v_new, BTN), o


def _gdn_scan_fwd(u, w, attn, qd, kt, dec):
    s = u.shape[1]
    nc = s // DN_CHUNK
    sp = _gdn_specs(s)
    h, d = DN_HEADS, DN_DIM

    def body(u_ref, w_ref, at_ref, qd_ref, kt_ref, dec_ref, o_ref, st_ref, state):
        @pl.when(pl.program_id(0) == 0)
        def _():
            state[...] = jnp.zeros_like(state)

        st_ref[...] = state[...]
        new, o = _gdn_step_math(state[...], u_ref[...], w_ref[...], at_ref[...], qd_ref[...], kt_ref[...], dec_ref[...])
        state[...] = new
        o_ref[...] = o

    return pl.pallas_call(
        body, name="gdn_scan_fwd", grid=(nc,),
        in_specs=[sp["hcd"], sp["hcd"], sp["att"], sp["hcd"], sp["hcd"], sp["dec"]],
        out_specs=[sp["hcd"], pl.BlockSpec((None, h, d, d), lambda n: (n, 0, 0, 0))],
        out_shape=[sp["s_hcd"], jax.ShapeDtypeStruct((nc, h, d, d), F32)],
        scratch_shapes=[pltpu.VMEM((h, d, d), F32)],
        compiler_params=_params(("arbitrary",)),
    )(u, w, attn, qd, kt, dec)


def _gdn_scan_bwd(do, states, u, w, attn, qd, kt, dec):
    s = u.shape[1]
    nc = s // DN_CHUNK
    h, c, d = DN_HEADS, DN_CHUNK, DN_DIM
    rev = lambda n: nc - 1 - n
    hcd = pl.BlockSpec((h, c, d), lambda n: (0, rev(n), 0))
    att = pl.BlockSpec((h, c, c), lambda n: (0, rev(n), 0))
    dec_s = pl.BlockSpec((h, None, 1, 128), lambda n: (0, rev(n), 0, 0))
    sp = _gdn_specs(s)

    def body(do_ref, st_ref, u_ref, w_ref, at_ref, qd_ref, kt_ref, dec_ref,
             du_ref, dw_ref, dat_ref, dqd_ref, dkt_ref, dde_ref, dstate):
        @pl.when(pl.program_id(0) == 0)
        def _():
            dstate[...] = jnp.zeros_like(dstate)

        _, vjp = jax.vjp(_gdn_step_math, st_ref[...], u_ref[...], w_ref[...], at_ref[...], qd_ref[...], kt_ref[...],
                         dec_ref[...])
        dst, du, dw, dat, dqd, dkt, dde = vjp((dstate[...], do_ref[...]))
        dstate[...] = dst
        for ref, val in zip((du_ref, dw_ref, dat_ref, dqd_ref, dkt_ref, dde_ref), (du, dw, dat, dqd, dkt, dde)):
            ref[...] = val

    return pl.pallas_call(
        body, name="gdn_scan_bwd", grid=(nc,),
        in_specs=[hcd, pl.BlockSpec((None, h, d, d), lambda n: (rev(n), 0, 0, 0)), hcd, hcd, att, hcd, hcd, dec_s],
        out_specs=[hcd, hcd, att, hcd, hcd, dec_s],
        out_shape=[sp["s_hcd"], sp["s_hcd"], sp["s_att"], sp["s_hcd"], sp["s_hcd"], sp["s_dec"]],
        scratch_shapes=[pltpu.VMEM((h, d, d), F32)],
        compiler_params=_params(("arbitrary",)),
    )(do, states, u, w, attn, qd, kt, dec)


def _gdn_out_math(o, z, gain):
    return _rms(o, gain) * _silu(z)


def _gdn_out_fwd(o_rows, z_rows, gain):
    return _rowwise("gdn_out_fwd", _gdn_out_math, [o_rows, z_rows], [(DN_DIM, F32)], bcast=[gain])[0]


def _gdn_out_bwd(dy_rows, o_rows, z_rows, gain):
    def fn(dv, ov, zv, g):
        _, vjp = jax.vjp(_gdn_out_math, ov, zv, g)
        return vjp(dv)
    return _rowwise("gdn_out_bwd", fn, [dy_rows, o_rows, z_rows], [(DN_DIM, F32)] * 2, bcast=[gain], accs=[(1, DN_DIM)])


def _heads_major(x):
    s = x.shape[0]
    return x.reshape(s, -1, DN_DIM).transpose(1, 0, 2)


def _heads_minor(x):
    return x.transpose(1, 0, 2).reshape(x.shape[1], -1)


def _pad_row(v, width=128):
    return jnp.pad(v.reshape(1, -1), ((0, 0), (0, width - v.size)))


def _swiglu_bwd_math(ds, a, b):
    sg = _sigmoid(a)
    return ds * b * sg * (1.0 + a * (1.0 - sg)), ds * a * sg


def _ffn_up(name, h, wg, wu, widx):
    s, d = h.shape
    j_n, fs = wg.shape[0], wg.shape[-1]
    tm = _row_tile(s)
    wblk = pl.BlockSpec((None,) * (1 + len(widx)) + (d, fs), lambda j, i: (j, *widx, 0, 0))
    oblk = pl.BlockSpec((tm, fs), lambda j, i: (i, j))

    def body(h_ref, g_ref, u_ref, a_ref, b_ref, s_ref):
        hv = h_ref[...]
        a, b = _dotb(hv, g_ref[...]), _dotb(hv, u_ref[...])
        a_ref[...] = a.astype(BF16)
        b_ref[...] = b.astype(BF16)
        s_ref[...] = (_silu(a) * b).astype(BF16)

    return pl.pallas_call(
        body, name=name, grid=(j_n, s // tm), in_specs=[pl.BlockSpec((tm, d), lambda j, i: (i, 0)), wblk, wblk],
        out_specs=[oblk] * 3, out_shape=[jax.ShapeDtypeStruct((s, j_n * fs), BF16)] * 3,
        compiler_params=_params(("parallel", "arbitrary"), 56),
    )(h, wg, wu)


def _ffn_down_bwd(name, df, wd, a, b, widx):
    s, d = df.shape
    j_n, fs = wd.shape[0], wd.shape[-2]
    tm = _row_tile(s)
    wblk = pl.BlockSpec((None,) * (1 + len(widx)) + (fs, d), lambda j, i: (j, *widx, 0, 0))
    oblk = pl.BlockSpec((tm, fs), lambda j, i: (i, j))

    def body(df_ref, w_ref, a_ref, b_ref, da_ref, db_ref):
        da, db = _swiglu_bwd_math(_dotb(df_ref[...], w_ref[...], NT), a_ref[...].astype(F32), b_ref[...].astype(F32))
        da_ref[...] = da.astype(BF16)
        db_ref[...] = db.astype(BF16)

    return pl.pallas_call(
        body, name=name, grid=(j_n, s // tm), in_specs=[pl.BlockSpec((tm, d), lambda j, i: (i, 0)), wblk, oblk, oblk],
        out_specs=[oblk] * 2, out_shape=[jax.ShapeDtypeStruct((s, j_n * fs), BF16)] * 2,
        compiler_params=_params(("parallel", "arbitrary"), 56),
    )(df, wd, a, b)


def _ffn_fwd(tag, x, gains_in, gains_out, wg, wu, wd, widx):
    h = _rms_fwd(f"{tag}_prenorm", x, gains_in)
    a, b, s = _ffn_up(f"{tag}_up", h, wg, wu, widx)
    f = _mm_rows(f"{tag}_down", s, wd, widx, F32)
    x_new = _rms_residual(f"{tag}_postnorm", x, f, gains_out, 0.5)
    return x_new, (x, h, a, b, s, f)


def _ffn_bwd(tag, dx_new, saved, gains_in, gains_out, wg, wu, wd, widx, gbuf):
    x, h, a, b, s, f = saved
    grad = lambda n, p, q, ac, bc, wt: _mm_grad(f"{tag}_{n}_grad", p, q, ac, bc, lead=widx, lead_dims=wt.shape[1:-2],
                                               into=gbuf.get(n))
    df, dg_out = _rms_bwd(f"{tag}_postnorm_bwd", dx_new, f, gains_out, 0.5, out_dtype=BF16)
    da, db = _ffn_down_bwd(f"{tag}_down_bwd", df, wd, a, b, widx)
    gbuf["down"] = grad("down", s, df, wd.shape[0], None, wd)
    gbuf["gate"] = grad("gate", h, da, None, wg.shape[0], wg)
    gbuf["up"] = grad("up", h, db, None, wu.shape[0], wu)
    dh = _mm_cols_t(f"{tag}_gateup_bwd", [(da, wg), (db, wu)], widx, F32)
    dx, dg_in = _rms_bwd(f"{tag}_prenorm_bwd", dh, x, gains_in, 1.0, residual=dx_new)
    return dx, dg_in, dg_out


def _mixer_fwd(l, h, w, p, bias_tabs):
    s = h.shape[0]
    nc = s // DN_CHUNK
    proj = _mm_cols("w_in_fwd", h, w["w_in"], (l,), F32)
    ops, ops_vjp = jax.vjp(_s5_operators, p["ssm_lambda_re"][l], p["ssm_lambda_im"][l], p["ssm_b_re"][l], p["ssm_b_im"][l],
                           p["ssm_c_re"][l], p["ssm_c_im"][l], p["ssm_d"][l], p["ssm_log_dt"][l])
    u_g = _to_groups(proj[:, :SSM_WIDTH])
    y_g, s_g = _s5_scan_fwd(u_g, *ops)
    y = _from_groups(y_g)
    glu_w, glu_b, gain_ssm = p["ssm_glu_w"][l], p["ssm_glu_b"][l][None], p["ssm_out_gain"][l][None]
    y_ssm = _s5_post_fwd(y, glu_w, glu_b, gain_ssm)
    conv_w, alog, dtb = p["dn_conv_w"][l], _pad_row(p["dn_a_log"][l]), _pad_row(p["dn_dt_bias"][l])
    gain_dn = p["dn_norm_gain"][l][None]
    act = _gdn_prep_fwd(proj, conv_w)
    gates_t = _gates_fwd(proj, alog, dtb)[:, :2 * DN_HEADS].T
    gcol, bcol = gates_t[:DN_HEADS, :, None], gates_t[DN_HEADS:, :, None]
    grow = gates_t[:DN_HEADS].reshape(DN_HEADS, nc, 1, DN_CHUNK)
    u, wy, attn, qd, kt, dec = _gdn_intra_fwd(act, gcol, grow, bcol)
    o_dn, states = _gdn_scan_fwd(u, wy, attn, qd, kt, dec)
    o_rows = o_dn.reshape(DN_HEADS * s, DN_DIM)
    z_rows = _heads_major(proj[:, COL_DZ:COL_DAB]).reshape(DN_HEADS * s, DN_DIM)
    y_dn = _heads_minor(_gdn_out_fwd(o_rows, z_rows, gain_dn).reshape(DN_HEADS, s, DN_DIM))
    outs, lses = zip(*[_attn_branch_fwd(bi, dil, proj, bias_tabs) for bi, (_, dil) in enumerate(DILATED)])
    gain_at = p["attn_out_gain"][l][None]
    y_at, o_at, lse = _attn_merge_fwd(outs, lses, gain_at)
    mix = jnp.concatenate([y_ssm, y_dn, y_at], axis=1).astype(BF16)
    out = _mm_rows("w_out_fwd", mix, w["w_out"], (l,), F32)
    saved = dict(h=h, proj=proj, ops=ops, ops_vjp=ops_vjp, u_g=u_g, s_g=s_g, y=y, act=act, gcol=gcol, grow=grow, bcol=bcol,
                 u=u, wy=wy, attn=attn, qd=qd, kt=kt, dec=dec, states=states, o_rows=o_rows, z_rows=z_rows,
                 o_at=o_at, lse=lse, mix=mix)
    return out, saved


def _mixer_bwd(l, dout, sv, w, p, bias_tabs):
    s = dout.shape[0]
    proj = sv["proj"]
    g = {}
    g["w_out"] = _mm_grad("w_out_grad", sv["mix"], dout, w["w_out"].shape[0], None)
    dmix = _mm_rows_t("w_out_bwd", dout, w["w_out"], (l,), F32)
    d_ssm, d_dn, d_at = dmix[:, :SSM_WIDTH], dmix[:, SSM_WIDTH:SSM_WIDTH + DN_WIDTH], dmix[:, SSM_WIDTH + DN_WIDTH:]
    glu_w, glu_b, gain_ssm = p["ssm_glu_w"][l], p["ssm_glu_b"][l][None], p["ssm_out_gain"][l][None]
    dy, g["ssm_glu_w"], dglu_b, dgain_ssm = _s5_post_bwd(d_ssm, sv["y"], glu_w, glu_b, gain_ssm)
    g["ssm_glu_b"], g["ssm_out_gain"] = dglu_b[0], dgain_ssm[0]
    du_g, *d_ops = _s5_scan_bwd(_to_groups(dy), sv["u_g"], sv["s_g"], *sv["ops"])
    (g["ssm_lambda_re"], g["ssm_lambda_im"], g["ssm_b_re"], g["ssm_b_im"], g["ssm_c_re"], g["ssm_c_im"], g["ssm_d"],
     g["ssm_log_dt"]) = sv["ops_vjp"](tuple(d_ops))
    d_u = _from_groups(du_g)
    gain_at = p["attn_out_gain"][l][None]
    do, dlt, dgain_at = _attn_merge_bwd(d_at, sv["o_at"], gain_at)
    g["attn_out_gain"] = dgain_at[0]
    dqs, dks, dvs, dbs = zip(*[_attn_branch_bwd(bi, dil, proj, do, sv["lse"], dlt, bias_tabs)
                               for bi, (_, dil) in enumerate(DILATED)])
    dq, dk, dv = _add3("attn_dq_sum", *dqs), _add3("attn_dk_sum", *dks), _add3("attn_dv_sum", *dvs)
    g["bias_tabs"] = jnp.stack(dbs)
    conv_w, alog, dtb = p["dn_conv_w"][l], _pad_row(p["dn_a_log"][l]), _pad_row(p["dn_dt_bias"][l])
    gain_dn = p["dn_norm_gain"][l][None]
    dy_rows = _heads_major(d_dn).reshape(DN_HEADS * s, DN_DIM)
    do_rows, dz_rows, dgain_dn = _gdn_out_bwd(dy_rows, sv["o_rows"], sv["z_rows"], gain_dn)
    g["dn_norm_gain"] = dgain_dn[0]
    d_scan = _gdn_scan_bwd(do_rows.reshape(DN_HEADS, s, DN_DIM), sv["states"], sv["u"], sv["wy"], sv["attn"], sv["qd"],
                           sv["kt"], sv["dec"])
    dact, dgc, dgr, dbc = _gdn_intra_bwd(sv["act"], sv["gcol"], sv["grow"], sv["bcol"], *d_scan)
    dgates_t = jnp.concatenate([dgc[..., 0] + dgr.reshape(DN_HEADS, s), dbc[..., 0]], axis=0)
    dgates = jnp.pad(dgates_t.T, ((0, 0), (0, 128 - 2 * DN_HEADS)))
    dab, dalog, ddtb = _gates_bwd(dgates, proj, alog, dtb)
    g["dn_a_log"], g["dn_dt_bias"] = dalog[0, :DN_HEADS], ddtb[0, :DN_HEADS]
    dqkv, g["dn_conv_w"] = _gdn_prep_bwd(dact, proj, conv_w)
    dz = _heads_minor(dz_rows.reshape(DN_HEADS, s, DN_DIM))
    dproj = jnp.concatenate([d_u, dq, dk, dv, dqkv, dz, dab, jnp.zeros((s, N_IN_PAD - COL_DAB - 128), F32)], axis=1)
    g["w_in"] = _mm_grad("w_in_grad", sv["h"], dproj, None, 1)[0]
    dh = _mm_cols_t("w_in_bwd", [(dproj, w["w_in"])], (l,), F32)
    return dh, g


def _local_step(x, target, w, p):
    depth = p["norm_gains"].shape[0]
    gains = p["norm_gains"]
    bias_tabs, bias_vjp = jax.vjp(_attn_bias_tables, p["rel_bias"])
    saved = []
    for l in range(depth):
        gn = lambda i: gains[l, i][None]
        x, sv1 = _ffn_fwd("ffn1", x, gn(0), gn(1), w["ffn_w_gate"], w["ffn_w_up"], w["ffn_w_down"], (l, 0))
        h = _rms_fwd("mix_prenorm", x, gn(2))
        out, svm = _mixer_fwd(l, h, w, p, bias_tabs)
        x_mid = x
        x = _rms_residual("mix_postnorm", x, out, gn(3), 1.0)
        x, sv2 = _ffn_fwd("ffn2", x, gn(4), gn(5), w["ffn_w_gate"], w["ffn_w_up"], w["ffn_w_down"], (l, 1))
        saved.append((sv1, svm, x_mid, out, sv2))
    loss, dx = _loss_and_grad(x, target)

    small = ["ssm_lambda_re", "ssm_lambda_im", "ssm_b_re", "ssm_b_im", "ssm_c_re", "ssm_c_im", "ssm_d", "ssm_log_dt",
             "ssm_glu_w", "ssm_glu_b", "ssm_out_gain", "dn_conv_w", "dn_a_log", "dn_dt_bias", "dn_norm_gain", "attn_out_gain"]
    per_layer = {n: [None] * depth for n in small + ["norm_gains", "w_in", "w_out"]}
    d_tabs = None
    gbuf = {}
    for l in reversed(range(depth)):
        gn = lambda i: gains[l, i][None]
        sv1, svm, x_mid, out, sv2 = saved[l]
        dx, dg4, dg5 = _ffn_bwd("ffn2", dx, sv2, gn(4), gn(5), w["ffn_w_gate"], w["ffn_w_up"], w["ffn_w_down"], (l, 1), gbuf)
        dout, dg3 = _rms_bwd("mix_postnorm_bwd", dx, out, gn(3), 1.0)
        dh, gm = _mixer_bwd(l, dout, svm, w, p, bias_tabs)
        dx, dg2 = _rms_bwd("mix_prenorm_bwd", dh, x_mid, gn(2), 1.0, residual=dx)
        dx, dg0, dg1 = _ffn_bwd("ffn1", dx, sv1, gn(0), gn(1), w["ffn_w_gate"], w["ffn_w_up"], w["ffn_w_down"], (l, 0), gbuf)
        per_layer["norm_gains"][l] = jnp.concatenate([dg0, dg1, dg2, dg3, dg4, dg5], axis=0)
        d_tabs = gm["bias_tabs"] if d_tabs is None else d_tabs + gm["bias_tabs"]
        for n in small + ["w_in", "w_out"]:
            per_layer[n][l] = gm[n]
    grads = {n: jnp.stack(per_layer[n], axis=0) for n in small + ["norm_gains", "w_in"]}
    grads["w_out"] = jnp.stack(per_layer["w_out"], axis=1)
    grads["ffn_w_gate"], grads["ffn_w_up"], grads["ffn_w_down"] = gbuf["gate"], gbuf["up"], gbuf["down"]
    grads["rel_bias"] = bias_vjp(d_tabs)[0]
    return loss, dx, grads


_ANY = pl.BlockSpec(memory_space=pl.ANY)


def _place():
    return lax.axis_index("x"), lax.axis_index("y"), lax.axis_index("c")


DMA_CHUNKS = 16


class _Transfer:
    def __init__(self, make, src, dst):
        self.make, self.src, self.dst = make, src, dst

    def start(self):
        rows = self.src.shape[0]
        k = DMA_CHUNKS
        while k > 1 and rows % (16 * k):
            k //= 2
        step = rows // k
        for i in range(k):
            self.make(self.src.at[pl.ds(i * step, step)], self.dst.at[pl.ds(i * step, step)]).start()

    def whole(self):
        return self.make(self.src, self.dst)


def _rows_into_block(name, fn, ins, idx, n_blocks, out_dtype):
    r, c = ins[0].shape[-2:]
    tm = min(_row_tile(r), 256)

    def body(idx_ref, *refs):
        refs[-1][...] = fn(*[t[...] for t in refs[:-1]]).astype(out_dtype)

    in_specs = [pl.BlockSpec((tm, c), lambda i, b: (i, 0)) if a.ndim == 2 else
                pl.BlockSpec((a.shape[0], tm, c), lambda i, b: (0, i, 0)) for a in ins]
    return pl.pallas_call(
        body, name=name,
        grid_spec=pltpu.PrefetchScalarGridSpec(
            num_scalar_prefetch=1, grid=(r // tm,), in_specs=in_specs,
            out_specs=pl.BlockSpec((None, tm, c), lambda i, b: (b[0], i, 0))),
        out_shape=jax.ShapeDtypeStruct((n_blocks, r, c), out_dtype),
        compiler_params=_params(("parallel",), 48),
    )(idx, *ins)


def _all_gather_xy(name, buf):
    def body(_, out, send_sems, recv_sems):
        x, y, c = _place()
        sib = (x, y, 1 - c)
        chips = [(1 - x, y), (x, 1 - y), (1 - x, 1 - y)]
        blk = lambda cx, cy: 2 * cx + cy

        def copy(k, j, half, to):
            make = lambda s, d: pltpu.make_async_remote_copy(src_ref=s, dst_ref=d, send_sem=send_sems.at[k],
                                                             recv_sem=recv_sems.at[k], device_id=to, device_id_type=MESH)
            return _Transfer(make, out.at[j, half], out.at[j, half])

        first = [copy(k, blk(x, y), c, (*chip, c)) for k, chip in enumerate(chips)]
        for cp in first:
            cp.start()
        passed = [copy(3 + k, blk(*chip), c, sib) for k, chip in enumerate(chips)]
        for k, chip in enumerate(chips):
            copy(k, blk(*chip), c, (*chip, c)).whole().wait_recv()
            passed[k].start()
        for k, chip in enumerate(chips):
            copy(3 + k, blk(*chip), 1 - c, sib).whole().wait_recv()
        for cp in first + passed:
            cp.whole().wait_send()

    return pl.pallas_call(
        body, name=name, in_specs=[_ANY], out_specs=_ANY, input_output_aliases={0: 0},
        out_shape=jax.ShapeDtypeStruct(buf.shape, buf.dtype),
        scratch_shapes=[pltpu.SemaphoreType.DMA((6,)), pltpu.SemaphoreType.DMA((6,))],
    )(buf)


def _pair_send(name, pieces):
    def body(src, dst, send_sems, recv_sems):
        x, y, c = _place()

        def copy(j, src_ref):
            make = lambda s, d: pltpu.make_async_remote_copy(src_ref=s, dst_ref=d, send_sem=send_sems.at[j],
                                                             recv_sem=recv_sems.at[j], device_id=(x, y, 1 - c),
                                                             device_id_type=MESH)
            return _Transfer(make, src_ref, dst.at[j])

        sends = [copy(j, src.at[j, 1 - c]) for j in range(4)]
        for cp in sends:
            cp.start()
        for j in range(4):
            copy(j, src.at[j, c]).whole().wait_recv()
        for cp in sends:
            cp.whole().wait_send()

    return pl.pallas_call(
        body, name=name, in_specs=[_ANY], out_specs=_ANY,
        out_shape=jax.ShapeDtypeStruct((4,) + pieces.shape[2:], pieces.dtype),
        scratch_shapes=[pltpu.SemaphoreType.DMA((4,)), pltpu.SemaphoreType.DMA((4,))],
    )(pieces)


def _pair_sum(name, pieces, recv, core):
    _, _, r, c = pieces.shape
    tm = min(_row_tile(r), 256)

    def body(core_ref, p_ref, r_ref, o_ref):
        o_ref[...] = (p_ref[...].astype(F32) + r_ref[...].astype(F32)).astype(o_ref.dtype)

    return pl.pallas_call(
        body, name=name,
        grid_spec=pltpu.PrefetchScalarGridSpec(
            num_scalar_prefetch=1, grid=(4, r // tm),
            in_specs=[pl.BlockSpec((None, None, tm, c), lambda j, i, b: (j, b[0], i, 0)),
                      pl.BlockSpec((None, tm, c), lambda j, i, b: (j, i, 0))],
            out_specs=pl.BlockSpec((None, tm, c), lambda j, i, b: (j, i, 0))),
        out_shape=jax.ShapeDtypeStruct((4, r, c), pieces.dtype),
        compiler_params=_params(("parallel", "parallel"), 48),
    )(core, pieces, recv)


def _exchange_xy(name, part):
    def body(src, dst, send_sems, recv_sems, local_sem):
        x, y, c = _place()
        chips = [(1 - x, y), (x, 1 - y), (1 - x, 1 - y)]
        blk = lambda cx, cy: 2 * cx + cy
        me = blk(x, y)

        def copy(k, chip, dst_block):
            make = lambda s, d: pltpu.make_async_remote_copy(src_ref=s, dst_ref=d, send_sem=send_sems.at[k],
                                                             recv_sem=recv_sems.at[k], device_id=(*chip, c),
                                                             device_id_type=MESH)
            return _Transfer(make, src.at[blk(*chip)], dst.at[dst_block])

        mine = _Transfer(lambda s, d: pltpu.make_async_copy(s, d, local_sem), src.at[me], dst.at[me])
        mine.start()
        sends = [copy(k, chip, me) for k, chip in enumerate(chips)]
        for cp in sends:
            cp.start()
        for k, chip in enumerate(chips):
            copy(k, chip, blk(*chip)).whole().wait_recv()
        for cp in sends:
            cp.whole().wait_send()
        mine.whole().wait()

    return pl.pallas_call(
        body, name=name, in_specs=[_ANY], out_specs=_ANY, out_shape=jax.ShapeDtypeStruct(part.shape, part.dtype),
        scratch_shapes=[pltpu.SemaphoreType.DMA((3,)), pltpu.SemaphoreType.DMA((3,)), pltpu.SemaphoreType.DMA],
    )(part)


def _exchange8(name, src, same_to_all=False):
    blk_shape = src.shape if same_to_all else src.shape[1:]

    def body(src_ref, dst, send_sems, recv_sems, local_sem):
        x, y, c = _place()
        me = 4 * x + 2 * y + c
        part = (lambda i: src_ref) if same_to_all else (lambda i: src_ref.at[i])

        def peer(k):
            return (1 - x if k & 4 else x, 1 - y if k & 2 else y, 1 - c if k & 1 else c)

        def copy(k, dst_block):
            px, py, pc = peer(k)
            make = lambda s, d: pltpu.make_async_remote_copy(src_ref=s, dst_ref=d, send_sem=send_sems.at[k - 1],
                                                             recv_sem=recv_sems.at[k - 1], device_id=(px, py, pc),
                                                             device_id_type=MESH)
            return _Transfer(make, part(4 * px + 2 * py + pc), dst.at[dst_block])

        mine = _Transfer(lambda s, d: pltpu.make_async_copy(s, d, local_sem), part(me), dst.at[me])
        mine.start()
        sends = [copy(k, me) for k in range(1, 8)]
        for cp in sends:
            cp.start()
        for k in range(1, 8):
            px, py, pc = peer(k)
            copy(k, 4 * px + 2 * py + pc).whole().wait_recv()
        for cp in sends:
            cp.whole().wait_send()
        mine.whole().wait()

    return pl.pallas_call(
        body, name=name, in_specs=[_ANY], out_specs=_ANY,
        out_shape=jax.ShapeDtypeStruct((8,) + blk_shape, src.dtype),
        scratch_shapes=[pltpu.SemaphoreType.DMA((7,)), pltpu.SemaphoreType.DMA((7,)), pltpu.SemaphoreType.DMA],
    )(src)


def _pair_swap(name, both):
    def body(_, out, send_sem, recv_sem):
        x, y, c = _place()
        remote = lambda s, d: pltpu.make_async_remote_copy(src_ref=s, dst_ref=d, send_sem=send_sem, recv_sem=recv_sem,
                                                           device_id=(x, y, 1 - c), device_id_type=MESH)
        push = _Transfer(remote, out.at[c], out.at[c])
        push.start()
        remote(out.at[c], out.at[1 - c]).wait_recv()
        push.whole().wait_send()

    return pl.pallas_call(
        body, name=name, in_specs=[_ANY], out_specs=_ANY, input_output_aliases={0: 0},
        out_shape=jax.ShapeDtypeStruct(both.shape, both.dtype),
        scratch_shapes=[pltpu.SemaphoreType.DMA, pltpu.SemaphoreType.DMA],
    )(both)


def _sum8(name, parts):
    _, r, c = parts.shape
    tm = min(_row_tile(r), 256)

    def body(p_ref, o_ref):
        o_ref[...] = _sum_blocks(p_ref[...])

    return pl.pallas_call(
        body, name=name, grid=(r // tm,), in_specs=[pl.BlockSpec((8, tm, c), lambda i: (0, i, 0))],
        out_specs=pl.BlockSpec((tm, c), lambda i: (i, 0)), out_shape=jax.ShapeDtypeStruct((r, c), F32),
        compiler_params=_params(("parallel",), 48),
    )(parts)


def _adamw(name, g, w, m, v):
    def fn(gv, wv, mv, vv):
        m2 = ADAM_B1 * mv + (1.0 - ADAM_B1) * gv
        v2 = ADAM_B2 * vv + (1.0 - ADAM_B2) * (gv * gv)
        m_hat = m2 / (1.0 - ADAM_B1 ** ADAM_STEP)
        v_hat = v2 / (1.0 - ADAM_B2 ** ADAM_STEP)
        return -ADAM_LR * (m_hat / (jnp.sqrt(v_hat) + ADAM_EPS) + ADAM_WD * wv), m2, v2
    return _rowwise(name, fn, [g, w, m, v], [(g.shape[1], F32)] * 3)


def _pack(tensors):
    flat = jnp.concatenate([t.reshape(-1).astype(F32) for t in tensors])
    rows = -(-flat.size // (128 * 512)) * 512
    return jnp.pad(flat, (0, rows * 128 - flat.size)).reshape(rows, 128)


def _unpack(buf, shapes):
    flat, out, at = buf.reshape(-1), [], 0
    for sh in shapes:
        n = math.prod(sh)
        out.append(flat[at:at + n].reshape(sh))
        at += n
    return out


_WEIGHTS = ("norm_gains", "ffn_w_gate", "ffn_w_up", "ffn_w_down", "w_in", "w_out", "ssm_lambda_re", "ssm_lambda_im",
            "ssm_b_re", "ssm_b_im", "ssm_c_re", "ssm_c_im", "ssm_d", "ssm_log_dt", "ssm_glu_w", "ssm_glu_b",
            "ssm_out_gain", "dn_conv_w", "dn_a_log", "dn_dt_bias", "dn_norm_gain", "attn_out_gain", "rel_bias")
_MATRICES = ("ffn_w_gate", "ffn_w_up", "ffn_w_down", "w_in", "w_out")
_CUT_SMALL = {"norm_gains": 2, "ssm_glu_w": 1, "dn_conv_w": 2}
_REPLICATED = tuple(n for n in _WEIGHTS if n not in _MATRICES and n not in _CUT_SMALL)


def _sum_blocks(x):
    acc = x[0].astype(F32)
    for i in range(1, x.shape[0]):
        acc = acc + x[i].astype(F32)
    return acc


def _gather_matrix(name, shard, chip, dtype=BF16):
    c = shard.shape[-1]
    buf = _rows_into_block(f"{name}_cast", lambda v: v, [shard.reshape(-1, c)], chip, 4, dtype)
    return _all_gather_xy(f"{name}_gather", buf.reshape(4, 2, -1, c)).reshape((4,) + shard.shape)


def _reduce_matrix(name, g, shard_shape, core):
    c = g.shape[-1]
    pieces = g.reshape(4, 2, -1, c)
    part = _pair_sum(f"{name}_pair_sum", pieces, _pair_send(f"{name}_pair", pieces), core)
    both = _rows_into_block(f"{name}_sum", _sum_blocks, [_exchange_xy(f"{name}_exchange", part)], core, 2, F32)
    return _pair_swap(f"{name}_swap", both).reshape(shard_shape)


def kernel(x, norm_gains, ffn_w_gate, ffn_w_up, ffn_w_down, w_in, w_out, ssm_lambda_re, ssm_lambda_im, ssm_b_re, ssm_b_im, ssm_c_re, ssm_c_im, ssm_d, ssm_log_dt, ssm_glu_w, ssm_glu_b, ssm_out_gain, dn_conv_w, dn_a_log, dn_dt_bias, dn_norm_gain, attn_out_gain, rel_bias, loss_target, m_norm_gains, m_ffn_w_gate, m_ffn_w_up, m_ffn_w_down, m_w_in, m_w_out, m_ssm_lambda_re, m_ssm_lambda_im, m_ssm_b_re, m_ssm_b_im, m_ssm_c_re, m_ssm_c_im, m_ssm_d, m_ssm_log_dt, m_ssm_glu_w, m_ssm_glu_b, m_ssm_out_gain, m_dn_conv_w, m_dn_a_log, m_dn_dt_bias, m_dn_norm_gain, m_attn_out_gain, m_rel_bias, v_norm_gains, v_ffn_w_gate, v_ffn_w_up, v_ffn_w_down, v_w_in, v_w_out, v_ssm_lambda_re, v_ssm_lambda_im, v_ssm_b_re, v_ssm_b_im, v_ssm_c_re, v_ssm_c_im, v_ssm_d, v_ssm_log_dt, v_ssm_glu_w, v_ssm_glu_b, v_ssm_out_gain, v_dn_conv_w, v_dn_a_log, v_dn_dt_bias, v_dn_norm_gain, v_attn_out_gain, v_rel_bias):
    wts = dict(zip(_WEIGHTS, (norm_gains, ffn_w_gate, ffn_w_up, ffn_w_down, w_in, w_out, ssm_lambda_re, ssm_lambda_im, ssm_b_re, ssm_b_im, ssm_c_re, ssm_c_im, ssm_d, ssm_log_dt, ssm_glu_w, ssm_glu_b, ssm_out_gain, dn_conv_w, dn_a_log, dn_dt_bias, dn_norm_gain, attn_out_gain, rel_bias)))
    mom = dict(zip(_WEIGHTS, (m_norm_gains, m_ffn_w_gate, m_ffn_w_up, m_ffn_w_down, m_w_in, m_w_out, m_ssm_lambda_re, m_ssm_lambda_im, m_ssm_b_re, m_ssm_b_im, m_ssm_c_re, m_ssm_c_im, m_ssm_d, m_ssm_log_dt, m_ssm_glu_w, m_ssm_glu_b, m_ssm_out_gain, m_dn_conv_w, m_dn_a_log, m_dn_dt_bias, m_dn_norm_gain, m_attn_out_gain, m_rel_bias)))
    var = dict(zip(_WEIGHTS, (v_norm_gains, v_ffn_w_gate, v_ffn_w_up, v_ffn_w_down, v_w_in, v_w_out, v_ssm_lambda_re, v_ssm_lambda_im, v_ssm_b_re, v_ssm_b_im, v_ssm_c_re, v_ssm_c_im, v_ssm_d, v_ssm_log_dt, v_ssm_glu_w, v_ssm_glu_b, v_ssm_out_gain, v_dn_conv_w, v_dn_a_log, v_dn_dt_bias, v_dn_norm_gain, v_attn_out_gain, v_rel_bias)))
    depth, d_model = norm_gains.shape[0], x.shape[-1]
    chip = 2 * lax.axis_index("x") + lax.axis_index("y")
    chip_idx = chip.astype(jnp.int32).reshape(1)
    core_idx = lax.axis_index("c").astype(jnp.int32).reshape(1)

    w = {n: _gather_matrix(n, wts[n], chip_idx) for n in ("ffn_w_gate", "ffn_w_up", "ffn_w_down", "w_out")}
    w_in_all = _gather_matrix("w_in", w_in, chip_idx)
    w_in_all = jnp.transpose(w_in_all, (1, 2, 0, 3)).reshape(depth, d_model, N_IN)
    w["w_in"] = jnp.pad(w_in_all, ((0, 0), (0, 0), (0, N_IN_PAD - N_IN)))[None]
    cut_names = tuple(_CUT_SMALL)
    cut_all = _gather_matrix("small", _pack([wts[n] for n in cut_names]), chip_idx, F32)
    p = {n: wts[n] for n in _REPLICATED}
    per_chip = [_unpack(cut_all[j], [wts[n].shape for n in cut_names]) for j in range(4)]
    for i, n in enumerate(cut_names):
        p[n] = jnp.concatenate([per_chip[j][i] for j in range(4)], axis=_CUT_SMALL[n])

    loss, dx, grads = _local_step(x[0], loss_target[0], w, p)
    loss = lax.psum(loss, ("x", "y", "c"))

    total = {}
    for n in ("ffn_w_gate", "ffn_w_up", "ffn_w_down", "w_out"):
        total[n] = _reduce_matrix(n, grads[n], wts[n].shape, core_idx)
    g_in = grads["w_in"][:, :, :N_IN].reshape(depth, d_model, 4, N_IN // 4)
    total["w_in"] = _reduce_matrix("w_in", jnp.transpose(g_in, (2, 0, 1, 3)), w_in.shape, core_idx)
    small_names = _REPLICATED + cut_names
    small_sum = _sum8("small_sum", _exchange8("small_exchange", _pack([grads[n] for n in small_names]), same_to_all=True))
    for n, g in zip(small_names, _unpack(small_sum, [grads[n].shape for n in small_names])):
        if n in _CUT_SMALL:
            cuts = jnp.split(g, 4, axis=_CUT_SMALL[n])
            g = functools.reduce(lambda acc, j: jnp.where(chip == j, cuts[j], acc), range(1, 4), cuts[0])
        total[n] = g

    delta, new_m, new_v = {}, {}, {}
    for n in _MATRICES + cut_names:
        c = wts[n].shape[-1]
        d2, m2, v2 = _adamw(f"{n}_adamw", total[n].reshape(-1, c), wts[n].reshape(-1, c), mom[n].reshape(-1, c),
                            var[n].reshape(-1, c))
        delta[n], new_m[n], new_v[n] = (t.reshape(wts[n].shape) for t in (d2, m2, v2))
    rep_shapes = [wts[n].shape for n in _REPLICATED]
    packed = _adamw("small_adamw", _pack([total[n] for n in _REPLICATED]), _pack([wts[n] for n in _REPLICATED]),
                    _pack([mom[n] for n in _REPLICATED]), _pack([var[n] for n in _REPLICATED]))
    for dst, buf in zip((delta, new_m, new_v), packed):
        dst.update(zip(_REPLICATED, _unpack(buf, rep_shapes)))

    return (loss, dx[None], *[total[n] for n in _WEIGHTS], *[delta[n] for n in _WEIGHTS],
            *[new_m[n] for n in _WEIGHTS], *[new_v[n] for n in _WEIGHTS])
```

```python
import functools
import math

import jax
import jax.numpy as jnp
from jax import lax
from jax.experimental import pallas as pl
from jax.experimental.pallas import tpu as pltpu

F32 = jnp.float32
BF16 = jnp.bfloat16
HI = lax.Precision.HIGHEST
MESH = pl.DeviceIdType.MESH

NORM_EPS = 1e-6
NEG_INF = -1e30
SSM_GROUPS, SSM_CH, SSM_STATE, SSM_WIDTH = 32, 16, 64, 512
SSM_T = 16
DN_HEADS, DN_DIM, DN_WIDTH, DN_CONV, DN_CHUNK = 6, 128, 768, 4, 64
AT_HEADS, AT_DIM, AT_WIDTH, AT_BLK = 6, 128, 768, 128
DILATED = ((128, 1), (512, 4), (2048, 16))
N_BUCKETS, REL_MAX = 32, 2048
N_IN = SSM_WIDTH + 3 * AT_WIDTH + 4 * DN_WIDTH + 2 * DN_HEADS
N_IN_PAD = 6144
COL_AQ, COL_AK, COL_AV, COL_DQKV, COL_DZ, COL_DAB = 512, 1280, 2048, 2816, 5120, 5888
ADAM_LR, ADAM_B1, ADAM_B2, ADAM_EPS, ADAM_WD, ADAM_STEP = 0.001, 0.9, 0.999, 1e-08, 0.01, 10
V7X_VMEM_BYTES = 64 * 1024 * 1024
NN = (((1,), (0,)), ((), ()))
NT = (((1,), (1,)), ((), ()))
TN = (((0,), (0,)), ((), ()))
BNN = (((2,), (1,)), ((0,), (0,)))
BNT = (((2,), (2,)), ((0,), (0,)))
BTN = (((1,), (1,)), ((0,), (0,)))


def _params(sem, vmem_mb=None):
    kw = {}
    if vmem_mb is not None:
        kw["vmem_limit_bytes"] = min(vmem_mb * 1024 * 1024, V7X_VMEM_BYTES - 8 * 1024 * 1024)
    return pltpu.CompilerParams(dimension_semantics=sem, **kw)


def _dotf(a, b, dims=NN):
    return lax.dot_general(a, b, dims, precision=HI, preferred_element_type=F32)


def _dot3(a, b, dims=NN):
    return lax.dot_general(a, b, dims, precision=lax.Precision.HIGH, preferred_element_type=F32)


def _dotb(a, b, dims=NN):
    return lax.dot_general(a.astype(BF16), b.astype(BF16), dims, preferred_element_type=F32)


def _sigmoid(x):
    return 1.0 / (1.0 + jnp.exp(-x))


def _silu(x):
    return x * _sigmoid(x)


def _softplus(x):
    return jnp.maximum(x, 0.0) + jnp.log(1.0 + jnp.exp(-jnp.abs(x)))


def _gelu(x):
    return 0.5 * x * (1.0 + jnp.tanh(math.sqrt(2.0 / math.pi) * (x + 0.044715 * x * x * x)))


def _rms(x, gain):
    return x * lax.rsqrt(jnp.mean(x * x, axis=-1, keepdims=True) + NORM_EPS) * gain


def _row_tile(s):
    for t in (512, 256, 128, 64, 32, 16, 8):
        if s % t == 0:
            return t
    return s


def _grid_ends(grid):
    ids = [pl.program_id(i) for i in range(len(grid))]
    first = functools.reduce(jnp.logical_and, [i == 0 for i in ids])
    last = functools.reduce(jnp.logical_and, [i == n - 1 for i, n in zip(ids, grid)])
    return first, last


def _mm(name, pairs, *, grid, a_blk, a_map, b_blk, b_map, o_shape, o_blk, o_map, dims, out_dtype=F32, vmem_mb=48,
        into=None, carry=None):
    n_red = grid[-1]
    n_pairs = len(pairs)
    n_in = 2 * n_pairs + (into is not None) + (carry is not None)
    n_out = 1 + (carry is not None)
    acc_shape = tuple(d for d in o_blk if d is not None)

    def body(*refs):
        ins, o_ref, scr = refs[:2 * n_pairs], refs[n_in], list(refs[n_in + n_out:])
        if carry is not None:
            first, last = _grid_ends(grid)
            recv_sems, send_sems = scr.pop(), scr.pop()
            start, finish = _gather_parts(refs[n_in + 1], send_sems, recv_sems)
            pl.when(first)(start)
        part = _dotb(ins[0][...], ins[1][...], dims)
        for p in range(1, n_pairs):
            part = part + _dotb(ins[2 * p][...], ins[2 * p + 1][...], dims)
        if n_red == 1:
            o_ref[...] = part.astype(o_ref.dtype)
        else:
            acc = scr[0]
            r = pl.program_id(len(grid) - 1)

            @pl.when(r == 0)
            def _():
                acc[...] = part

            @pl.when(r > 0)
            def _():
                acc[...] += part

            @pl.when(r == n_red - 1)
            def _():
                o_ref[...] = acc[...].astype(o_ref.dtype)
        if carry is not None:
            pl.when(last)(finish)

    in_specs, args = [], []
    for a, b in pairs:
        in_specs += [pl.BlockSpec(a_blk, a_map), pl.BlockSpec(b_blk, b_map)]
        args += [a, b]
    out_specs, out_shape = [pl.BlockSpec(o_blk, o_map)], [jax.ShapeDtypeStruct(o_shape, out_dtype)]
    scratch = [pltpu.VMEM(acc_shape, F32)] if n_red > 1 else []
    aliases = {}
    if into is not None:
        aliases[len(args)] = 0
        in_specs.append(pl.BlockSpec(memory_space=pl.ANY))
        args.append(into)
    if carry is not None:
        aliases[len(args)] = 1
        in_specs.append(pl.BlockSpec(memory_space=pl.ANY))
        args.append(carry)
        out_specs.append(pl.BlockSpec(memory_space=pl.ANY))
        out_shape.append(jax.ShapeDtypeStruct(carry.shape, carry.dtype))
        scratch += [pltpu.SemaphoreType.DMA((6,)), pltpu.SemaphoreType.DMA((6,))]
    sem = ("arbitrary",) * len(grid) if carry is not None else ("parallel",) * (len(grid) - 1) + ("arbitrary",)
    res = pl.pallas_call(
        body, name=name, grid=grid, in_specs=in_specs, out_specs=out_specs, out_shape=out_shape,
        input_output_aliases=aliases, scratch_shapes=scratch, compiler_params=_params(sem, vmem_mb),
    )(*args)
    return res if carry is not None else res[0]


def _col_tile(n):
    for t in (1536, 1408, 1024, 768, 512, 384, 256, 128):
        if n % t == 0:
            return t
    return n


def _mm_cols(name, a, w, widx, out_dtype, carry=None):
    s, k = a.shape
    j_n, nj = w.shape[0], w.shape[-1]
    tm, tn = _row_tile(s), _col_tile(nj)
    nt = nj // tn
    lead = (None,) * (1 + len(widx))
    return _mm(name, [(a, w)], grid=(j_n, nt, s // tm, 1),
               a_blk=(tm, k), a_map=lambda j, c, i, r: (i, 0),
               b_blk=lead + (k, tn), b_map=lambda j, c, i, r: (j, *widx, 0, c),
               o_shape=(s, j_n * nj), o_blk=(tm, tn), o_map=lambda j, c, i, r: (i, j * nt + c),
               dims=NN, out_dtype=out_dtype, carry=carry)


def _mm_rows(name, a, w, widx, out_dtype, carry=None):
    s = a.shape[0]
    j_n, kj, n = w.shape[0], w.shape[-2], w.shape[-1]
    tm = _row_tile(s)
    lead = (None,) * (1 + len(widx))
    return _mm(name, [(a, w)], grid=(s // tm, j_n),
               a_blk=(tm, kj), a_map=lambda i, j: (i, j),
               b_blk=lead + (kj, n), b_map=lambda i, j: (j, *widx, 0, 0),
               o_shape=(s, n), o_blk=(tm, n), o_map=lambda i, j: (i, 0), dims=NN, out_dtype=out_dtype, carry=carry)


def _mm_rows_t(name, a, w, widx, out_dtype):
    s, n = a.shape
    j_n, kj = w.shape[0], w.shape[-2]
    tm = _row_tile(s)
    lead = (None,) * (1 + len(widx))
    return _mm(name, [(a, w)], grid=(j_n, s // tm, 1),
               a_blk=(tm, n), a_map=lambda j, i, r: (i, 0),
               b_blk=lead + (kj, n), b_map=lambda j, i, r: (j, *widx, 0, 0),
               o_shape=(s, j_n * kj), o_blk=(tm, kj), o_map=lambda j, i, r: (i, j), dims=NT, out_dtype=out_dtype)


def _mm_cols_t(name, pairs, widx, out_dtype):
    a0, w0 = pairs[0]
    s = a0.shape[0]
    j_n, k, nj = w0.shape[0], w0.shape[-2], w0.shape[-1]
    tm, tn = _row_tile(s), _col_tile(nj)
    nt = nj // tn
    lead = (None,) * (1 + len(widx))
    return _mm(name, pairs, grid=(s // tm, j_n * nt),
               a_blk=(tm, tn), a_map=lambda i, r: (i, r),
               b_blk=lead + (k, tn), b_map=lambda i, r: (lax.div(r, nt), *widx, 0, lax.rem(r, nt)),
               o_shape=(s, k), o_blk=(tm, k), o_map=lambda i, r: (i, 0), dims=NT, out_dtype=out_dtype)


def _mm_grad(name, a, b, a_cols, b_cols, out_dtype=BF16, lead=(), lead_dims=(), into=None):
    s = a.shape[0]
    tm = _row_tile(s)
    if a_cols is not None:
        j_n, ka, nb = a_cols, a.shape[1] // a_cols, b.shape[1]
        tk, tn = ka, _col_tile(nb)
        a_map = lambda j, kb, c, i: (i, j)
        b_map = lambda j, kb, c, i: (i, c)
    else:
        j_n, ka, nb = b_cols, a.shape[1], b.shape[1] // b_cols
        tk, tn = min(ka, 1024), _col_tile(nb)
        nt_ = nb // tn
        a_map = lambda j, kb, c, i: (i, kb)
        b_map = lambda j, kb, c, i: (i, j * nt_ + c)
    return _mm(name, [(a, b)], grid=(j_n, ka // tk, nb // tn, s // tm),
               a_blk=(tm, tk), a_map=a_map, b_blk=(tm, tn), b_map=b_map,
               o_shape=(j_n, *lead_dims, ka, nb), o_blk=(None,) * (1 + len(lead)) + (tk, tn),
               o_map=lambda j, kb, c, i: (j, *lead, kb, c), dims=TN, out_dtype=out_dtype, into=into)


def _rowwise(name, fn, rows, outs, *, bcast=(), accs=(), tm=None, vmem_mb=48):
    rows = [r if isinstance(r, tuple) else (r, r.shape[1], 0) for r in rows]
    s = rows[0][0].shape[0]
    tm = tm or min(_row_tile(s), 256)
    nr, nb, no = len(rows), len(bcast), len(outs)

    def body(*refs):
        o_refs, a_refs = refs[nr + nb:nr + nb + no], refs[nr + nb + no:]
        res = fn(*[r[...] for r in refs[:nr + nb]])
        if not isinstance(res, (tuple, list)):
            res = (res,)
        for o, v in zip(o_refs, res[:no]):
            o[...] = v.astype(o.dtype)
        if a_refs:
            i = pl.program_id(0)
            for a, v in zip(a_refs, res[no:]):
                @pl.when(i == 0)
                def _(a=a, v=v):
                    a[...] = v

                @pl.when(i > 0)
                def _(a=a, v=v):
                    a[...] += v

    in_specs = [pl.BlockSpec((tm, w), lambda i, c=c: (i, c)) for _, w, c in rows]
    in_specs += [pl.BlockSpec(b.shape, lambda i, nd=b.ndim: (0,) * nd) for b in bcast]
    out_specs = [pl.BlockSpec((tm, c), lambda i: (i, 0)) for c, _ in outs]
    out_specs += [pl.BlockSpec(sh, lambda i, nd=len(sh): (0,) * nd) for sh in accs]
    out_shape = [jax.ShapeDtypeStruct((s, c), dt) for c, dt in outs] + [jax.ShapeDtypeStruct(sh, F32) for sh in accs]
    res = pl.pallas_call(
        body, name=name, grid=(s // tm,), in_specs=in_specs, out_specs=out_specs, out_shape=out_shape,
        compiler_params=_params(("arbitrary",) if accs else ("parallel",), vmem_mb),
    )(*[r[0] for r in rows], *bcast)
    return res


def _rms_fwd(name, x, gain):
    return _rowwise(name, lambda xv, g: _rms(xv, g), [x], [(x.shape[1], BF16)], bcast=[gain])[0]


def _rms_residual(name, x, f, gain, scale):
    return _rowwise(name, lambda xv, fv, g: xv + scale * _rms(fv, g), [x, f], [(x.shape[1], F32)], bcast=[gain])[0]


def _rms_bwd_math(dy, x, gain):
    r = lax.rsqrt(jnp.mean(x * x, axis=-1, keepdims=True) + NORM_EPS)
    xh = x * r
    dxh = dy * gain
    dx = r * (dxh - xh * jnp.mean(dxh * xh, axis=-1, keepdims=True))
    return dx, jnp.sum(dy * xh, axis=0, keepdims=True)


def _rms_bwd(name, dy, x, gain, scale=1.0, residual=None, out_dtype=F32):
    d = x.shape[1]
    if residual is None:
        fn = lambda dyv, xv, g: _rms_bwd_math(scale * dyv.astype(F32), xv, g)
        rows = [dy, x]
    else:
        def fn(dyv, xv, rv, g):
            dx, dg = _rms_bwd_math(scale * dyv.astype(F32), xv, g)
            return dx + rv, dg
        rows = [dy, x, residual]
    return _rowwise(name, fn, rows, [(d, out_dtype)], bcast=[gain], accs=[(1, d)])


def _loss_and_grad(y, target):
    d = y.shape[1]

    def fn(yv, tv):
        e = yv - tv
        part = 0.5 * jnp.sum(jnp.mean(e * e, axis=-1, keepdims=True), axis=0, keepdims=True)
        return e * (1.0 / d), jnp.broadcast_to(part, (1, 128))
    dy, loss = _rowwise("loss_head", fn, [y, target], [(d, F32)], accs=[(1, 128)])
    return loss[0, 0], dy


def _s5_operators(lam_re, lam_im, b_re, b_im, c_re, c_im, d_skip, log_dt):
    t_n, ch, p_n = SSM_T, SSM_CH, SSM_STATE
    dt = jnp.exp(log_dt)[:, None]
    ld_re, ld_im = lam_re * dt, lam_im * dt
    k = jnp.arange(t_n + 1, dtype=F32)[None, :, None]
    mag = jnp.exp(ld_re[:, None, :] * k)
    pw_re, pw_im = mag * jnp.cos(ld_im[:, None, :] * k), mag * jnp.sin(ld_im[:, None, :] * k)
    lb_re, lb_im = pw_re[:, 1], pw_im[:, 1]
    den = lam_re * lam_re + lam_im * lam_im
    f_re = ((lb_re - 1.0) * lam_re + lb_im * lam_im) / den
    f_im = (lb_im * lam_re - (lb_re - 1.0) * lam_im) / den
    bb_re = f_re[..., None] * b_re - f_im[..., None] * b_im
    bb_im = f_re[..., None] * b_im + f_im[..., None] * b_re
    cp_re = c_re[:, None] * pw_re[:, :t_n, None, :] - c_im[:, None] * pw_im[:, :t_n, None, :]
    cp_im = c_re[:, None] * pw_im[:, :t_n, None, :] + c_im[:, None] * pw_re[:, :t_n, None, :]
    taps = (jnp.einsum("gtcp,gpd->gtcd", cp_re, bb_re, precision=HI)
            - jnp.einsum("gtcp,gpd->gtcd", cp_im, bb_im, precision=HI))
    m5 = jnp.stack([jnp.pad(taps[:, :t_n - j], ((0, 0), (j, 0), (0, 0), (0, 0))) for j in range(t_n)],
                   axis=1)
    m_op = jnp.transpose(m5, (0, 1, 4, 2, 3)).reshape(SSM_GROUPS, t_n * ch, t_n * ch)
    m_op = m_op + jnp.eye(t_n * ch, dtype=F32)[None] * jnp.tile(d_skip.reshape(SSM_GROUPS, 1, ch), (1, t_n, 1)).reshape(
        SSM_GROUPS, 1, t_n * ch)
    rv_re, rv_im = pw_re[:, t_n - 1::-1][:, :t_n], pw_im[:, t_n - 1::-1][:, :t_n]
    bo_re = rv_re[:, :, None, :] * jnp.swapaxes(bb_re, 1, 2)[:, None] - rv_im[:, :, None, :] * jnp.swapaxes(bb_im, 1, 2)[:, None]
    bo_im = rv_re[:, :, None, :] * jnp.swapaxes(bb_im, 1, 2)[:, None] + rv_im[:, :, None, :] * jnp.swapaxes(bb_re, 1, 2)[:, None]
    b_op = jnp.concatenate([bo_re, bo_im], axis=-1).reshape(SSM_GROUPS, t_n * ch, 2 * p_n)
    q_re = c_re[:, None] * pw_re[:, 1:, None, :] - c_im[:, None] * pw_im[:, 1:, None, :]
    q_im = c_re[:, None] * pw_im[:, 1:, None, :] + c_im[:, None] * pw_re[:, 1:, None, :]
    c_op = jnp.concatenate([q_re, -q_im], axis=-1).reshape(SSM_GROUPS, t_n * ch, 2 * p_n)
    c_op = jnp.swapaxes(c_op, 1, 2)
    a1 = jnp.concatenate([pw_re[:, t_n], pw_re[:, t_n]], axis=-1)[:, None, :]
    a2 = jnp.concatenate([-pw_im[:, t_n], pw_im[:, t_n]], axis=-1)[:, None, :]
    return m_op, b_op, c_op, a1, a2


def _s5_groups(name, fn, ins, out_dims):
    g_n = ins[0].shape[0]
    blk = lambda a, b: pl.BlockSpec((None, a, b), lambda g: (g, 0, 0))

    def body(*refs):
        res = fn(*[r[...] for r in refs[:len(ins)]])
        for o, v in zip(refs[len(ins):], res):
            o[...] = v

    return pl.pallas_call(
        body, name=name, grid=(g_n,), in_specs=[blk(*a.shape[1:]) for a in ins], out_specs=[blk(*d) for d in out_dims],
        out_shape=[jax.ShapeDtypeStruct((g_n,) + tuple(d), F32) for d in out_dims],
        compiler_params=_params(("parallel",)),
    )(*ins)


def _s5_state_scan(z_t, a1, a2):
    nc, g_n, p2 = z_t.shape

    def body(z_ref, a1_ref, a2_ref, s_ref):
        a1v, a2v = a1_ref[...], a2_ref[...]

        def step(n, s):
            s_ref[n] = s
            return a1v * s + a2v * pltpu.roll(s, SSM_STATE, 1) + z_ref[n]

        lax.fori_loop(0, nc, step, jnp.zeros((g_n, p2), F32))

    return pl.pallas_call(body, name="s5_state_scan", out_shape=jax.ShapeDtypeStruct(z_t.shape, F32))(z_t, a1, a2)


def _s5_state_scan_bwd(ds_t, s_t, a1, a2):
    nc, g_n, p2 = s_t.shape

    def body(ds_ref, s_ref, a1_ref, a2_ref, dz_ref, da1_ref, da2_ref):
        a1v, a2v = a1_ref[...], a2_ref[...]

        def step(k, carry):
            g, d1, d2 = carry
            n = nc - 1 - k
            dz_ref[n] = g
            sn = s_ref[n]
            d1 = d1 + g * sn
            d2 = d2 + g * pltpu.roll(sn, SSM_STATE, 1)
            g = ds_ref[n] + a1v * g + pltpu.roll(a2v * g, SSM_STATE, 1)
            return g, d1, d2

        zero = jnp.zeros((g_n, p2), F32)
        _, d1, d2 = lax.fori_loop(0, nc, step, (zero, zero, zero))
        da1_ref[...] = d1
        da2_ref[...] = d2

    row = jax.ShapeDtypeStruct((g_n, p2), F32)
    return pl.pallas_call(body, name="s5_state_scan_bwd",
                          out_shape=[jax.ShapeDtypeStruct(s_t.shape, F32), row, row])(ds_t, s_t, a1, a2)


def _s5_scan_fwd(u_g, m_op, b_op, c_op, a1, a2):
    _, nc, w = u_g.shape
    p2 = 2 * SSM_STATE
    z_g, = _s5_groups("s5_chunk_inputs", lambda u, b: (_dotf(u, b),), [u_g, b_op], [(nc, p2)])
    s_g = jnp.swapaxes(_s5_state_scan(jnp.swapaxes(z_g, 0, 1), a1[:, 0], a2[:, 0]), 0, 1)
    y_g, = _s5_groups("s5_outputs", lambda u, s, m, c: (_dotf(u, m) + _dotf(s, c),), [u_g, s_g, m_op, c_op], [(nc, w)])
    return y_g, s_g


def _s5_scan_bwd(dy_g, u_g, s_g, m_op, b_op, c_op, a1, a2):
    _, nc, w = u_g.shape
    p2 = 2 * SSM_STATE
    ds_g, = _s5_groups("s5_dstate", lambda dy, c: (_dotf(dy, c, NT),), [dy_g, c_op], [(nc, p2)])
    dz_t, da1, da2 = _s5_state_scan_bwd(jnp.swapaxes(ds_g, 0, 1), jnp.swapaxes(s_g, 0, 1), a1[:, 0], a2[:, 0])
    dz_g = jnp.swapaxes(dz_t, 0, 1)

    def grads(dy, dz, u, s, m, b):
        return _dotf(dy, m, NT) + _dotf(dz, b, NT), _dotf(u, dy, TN), _dotf(u, dz, TN), _dotf(s, dy, TN)

    du, dm, db, dc = _s5_groups("s5_grads", grads, [dy_g, dz_g, u_g, s_g, m_op, b_op], [(nc, w), (w, w), (w, p2), (p2, w)])
    return du, dm, db, dc, da1[:, None], da2[:, None]


def _to_groups(u):
    s = u.shape[0]
    return u.reshape(s // SSM_T, SSM_T, SSM_GROUPS, SSM_CH).transpose(2, 0, 1, 3).reshape(
        SSM_GROUPS, s // SSM_T, SSM_T * SSM_CH)


def _from_groups(y_g):
    nc = y_g.shape[1]
    return y_g.reshape(SSM_GROUPS, nc, SSM_T, SSM_CH).transpose(1, 2, 0, 3).reshape(nc * SSM_T, SSM_WIDTH)


def _s5_post_math(y, glu_w, glu_b, gain):
    y2 = _gelu(y)
    o = y2 * _sigmoid(_dotb(y2, glu_w) + glu_b)
    return _rms(o, gain)


def _s5_post_fwd(y, glu_w, glu_b, gain):
    return _rowwise("s5_post_fwd", _s5_post_math, [y], [(SSM_WIDTH, F32)], bcast=[glu_w, glu_b, gain])[0]


def _s5_post_bwd(dout, y, glu_w, glu_b, gain):
    def fn(dv, yv, w, b, g):
        _, vjp = jax.vjp(_s5_post_math, yv, w, b, g)
        return vjp(dv)
    return _rowwise("s5_post_bwd", fn, [dout, y], [(SSM_WIDTH, F32)], bcast=[glu_w, glu_b, gain],
                    accs=[(SSM_WIDTH, SSM_WIDTH), (1, SSM_WIDTH), (1, SSM_WIDTH)])


def _t5_bucket(dist):
    max_exact = N_BUCKETS // 2
    d = jnp.maximum(dist, 1).astype(F32)
    large = max_exact + jnp.log(d / max_exact) / math.log(REL_MAX / max_exact) * (N_BUCKETS - max_exact)
    large = jnp.minimum(large.astype(jnp.int32), N_BUCKETS - 1)
    return jnp.where(dist < max_exact, dist, large)


def _attn_bias_tables(rel_bias):
    blk = AT_BLK
    tabs = []
    for window, dil in DILATED:
        rel = blk + jnp.arange(blk)[:, None] - jnp.arange(2 * blk)[None, :]
        valid = (rel >= 0) & (rel <= window // dil)
        bias = jnp.moveaxis(rel_bias[_t5_bucket(jnp.maximum(rel, 0) * dil)], -1, 0)
        tabs.append(jnp.where(valid[None], bias, NEG_INF))
    return jnp.stack(tabs)


def _attn_specs(s):
    col = lambda first: pl.BlockSpec((s, AT_DIM), lambda h, r: (0, first // AT_DIM + h))
    return col(COL_AQ), col(COL_AK), col(COL_AV), col(0)


def _attn_branch_fwd(b_idx, dil, proj, bias):
    s, blk, dh = proj.shape[0], AT_BLK, AT_DIM
    nb = s // dil // blk
    scale = dh ** -0.5

    def body(q_ref, k_ref, v_ref, b_ref, o_ref, l_ref):
        r = pl.program_id(1)

        def block(n, carry):
            cur = pl.ds(r + n * (blk * dil), blk, stride=dil)
            prv = pl.ds(r + jnp.maximum(n - 1, 0) * (blk * dil), blk, stride=dil)
            q = q_ref[cur, :] * scale
            lc = _dotb(q, k_ref[cur, :], NT) + b_ref[:, blk:]
            lp = _dotb(q, k_ref[prv, :], NT) + b_ref[:, :blk]
            lp = jnp.where(n > 0, lp, NEG_INF)
            m = jnp.maximum(jnp.max(lc, axis=1, keepdims=True), jnp.max(lp, axis=1, keepdims=True))
            pc, pp = jnp.exp(lc - m), jnp.exp(lp - m)
            den = jnp.sum(pc, axis=1, keepdims=True) + jnp.sum(pp, axis=1, keepdims=True)
            inv = 1.0 / den
            o_ref[cur, :] = _dotb(pc * inv, v_ref[cur, :]) + _dotb(pp * inv, v_ref[prv, :])
            l_ref[cur, :] = jnp.broadcast_to(m + jnp.log(den), (blk, dh))
            return carry

        lax.fori_loop(0, nb, block, 0)

    q_s, k_s, v_s, out_s = _attn_specs(s)
    return pl.pallas_call(
        body, name=f"attn_fwd_d{dil}", grid=(AT_HEADS, dil),
        in_specs=[q_s, k_s, v_s, pl.BlockSpec((None, None, blk, 2 * blk), lambda h, r: (b_idx, h, 0, 0))],
        out_specs=[out_s, out_s],
        out_shape=[jax.ShapeDtypeStruct((s, AT_WIDTH), F32)] * 2,
        compiler_params=_params(("parallel", "arbitrary")),
    )(proj, proj, proj, bias)


def _attn_branch_bwd(b_idx, dil, proj, do, lse, dlt, bias):
    s, blk, dh = proj.shape[0], AT_BLK, AT_DIM
    nb = s // dil // blk
    scale = dh ** -0.5

    def body(q_ref, k_ref, v_ref, do_ref, l_ref, d_ref, b_ref, dq_ref, dk_ref, dv_ref, db_ref):
        r = pl.program_id(1)

        @pl.when(r == 0)
        def _():
            dk_ref[...] = jnp.zeros_like(dk_ref)
            dv_ref[...] = jnp.zeros_like(dv_ref)
            db_ref[...] = jnp.zeros_like(db_ref)

        def block(n, carry):
            cur = pl.ds(r + n * (blk * dil), blk, stride=dil)
            prv = pl.ds(r + jnp.maximum(n - 1, 0) * (blk * dil), blk, stride=dil)
            q = q_ref[cur, :] * scale
            do_b = do_ref[cur, :]
            lse_b = l_ref[cur, :][:, :1]
            dlt_b = d_ref[cur, :][:, :1]
            kc, kp = k_ref[cur, :], k_ref[prv, :]
            vc, vp = v_ref[cur, :], v_ref[prv, :]
            lc = _dotb(q, kc, NT) + b_ref[:, blk:]
            lp = jnp.where(n > 0, _dotb(q, kp, NT) + b_ref[:, :blk], NEG_INF)
            pc, pp = jnp.exp(lc - lse_b), jnp.exp(lp - lse_b)
            dsc = pc * (_dotb(do_b, vc, NT) - dlt_b)
            dsp = pp * (_dotb(do_b, vp, NT) - dlt_b)
            dq_ref[cur, :] = (_dotb(dsc, kc) + _dotb(dsp, kp)) * scale
            dk_ref[cur, :] = dk_ref[cur, :] + _dotb(dsc, q, TN)
            dk_ref[prv, :] = dk_ref[prv, :] + _dotb(dsp, q, TN)
            dv_ref[cur, :] = dv_ref[cur, :] + _dotb(pc, do_b, TN)
            dv_ref[prv, :] = dv_ref[prv, :] + _dotb(pp, do_b, TN)
            db_ref[:, blk:] += dsc
            db_ref[:, :blk] += dsp
            return carry

        lax.fori_loop(0, nb, block, 0)

    q_s, k_s, v_s, out_s = _attn_specs(s)
    tab = pl.BlockSpec((None, None, blk, 2 * blk), lambda h, r: (b_idx, h, 0, 0))
    return pl.pallas_call(
        body, name=f"attn_bwd_d{dil}", grid=(AT_HEADS, dil),
        in_specs=[q_s, k_s, v_s, out_s, out_s, out_s, tab],
        out_specs=[out_s, out_s, out_s, pl.BlockSpec((None, blk, 2 * blk), lambda h, r: (h, 0, 0))],
        out_shape=[jax.ShapeDtypeStruct((s, AT_WIDTH), F32)] * 3 + [jax.ShapeDtypeStruct((AT_HEADS, blk, 2 * blk), F32)],
        compiler_params=_params(("parallel", "arbitrary"), 56),
    )(proj, proj, proj, do, lse, dlt, bias)


def _per_head(fn, *xs):
    return jnp.concatenate([fn(*[x[:, h * AT_DIM:(h + 1) * AT_DIM] for x in xs]) for h in range(AT_HEADS)], axis=1)


def _attn_merge_math(o1, o2, o3, l1, l2, l3, gain):
    m = jnp.maximum(jnp.maximum(l1, l2), l3)
    e1, e2, e3 = jnp.exp(l1 - m), jnp.exp(l2 - m), jnp.exp(l3 - m)
    den = e1 + e2 + e3
    o = (e1 * o1 + e2 * o2 + e3 * o3) / den
    return _rms(o, gain), o, m + jnp.log(den)


def _attn_merge_fwd(os_, ls_, gain):
    w = AT_WIDTH
    return _rowwise("attn_merge_fwd", _attn_merge_math, [*os_, *ls_], [(w, F32)] * 3, bcast=[gain])


def _attn_merge_bwd(dy, o, gain):
    def fn(dyv, ov, g):
        do, dg = _rms_bwd_math(dyv, ov, g)
        dlt = _per_head(lambda a, b: jnp.broadcast_to(jnp.sum(a * b, axis=1, keepdims=True), a.shape), do, ov)
        return do, dlt, dg
    return _rowwise("attn_merge_bwd", fn, [dy, o], [(AT_WIDTH, F32)] * 2, bcast=[gain], accs=[(1, AT_WIDTH)])


def _add3(name, a, b, c):
    return _rowwise(name, lambda x, y, z: x + y + z, [a, b, c], [(a.shape[1], F32)])[0]


def _conv_taps(x, w):
    row = lax.broadcasted_iota(jnp.int32, x.shape, 0)
    y = w[DN_CONV - 1:DN_CONV, :] * x
    for sh in range(1, DN_CONV):
        y = y + w[DN_CONV - 1 - sh:DN_CONV - sh, :] * jnp.where(row >= sh, pltpu.roll(x, sh, 0), 0.0)
    return y


def _gdn_prep_fwd(proj, conv_w):
    s = proj.shape[0]
    ncb = 3 * DN_HEADS
    c0 = COL_DQKV // 128

    def body(x_ref, w_ref, o_ref):
        o_ref[...] = _silu(_conv_taps(x_ref[...], w_ref[...]))

    return pl.pallas_call(
        body, name="gdn_prep_fwd", grid=(ncb,),
        in_specs=[pl.BlockSpec((s, 128), lambda c: (0, c0 + c)), pl.BlockSpec((DN_CONV, 128), lambda c: (0, c))],
        out_specs=pl.BlockSpec((None, s, 128), lambda c: (c, 0, 0)),
        out_shape=jax.ShapeDtypeStruct((ncb, s, 128), F32),
        compiler_params=_params(("parallel",)),
    )(proj, conv_w)


def _gdn_prep_bwd(dact, proj, conv_w):
    s = proj.shape[0]
    ncb = 3 * DN_HEADS
    c0 = COL_DQKV // 128

    def body(d_ref, x_ref, w_ref, dx_ref, dw_ref):
        x, w = x_ref[...], w_ref[...]
        pre = _conv_taps(x, w)
        sg = _sigmoid(pre)
        dpre = d_ref[...] * sg * (1.0 + pre * (1.0 - sg))
        row = lax.broadcasted_iota(jnp.int32, x.shape, 0)
        dx = w[DN_CONV - 1:DN_CONV, :] * dpre
        dw_ref[pl.ds(DN_CONV - 1, 1), :] = jnp.sum(dpre * x, axis=0, keepdims=True)
        for sh in range(1, DN_CONV):
            dx = dx + w[DN_CONV - 1 - sh:DN_CONV - sh, :] * jnp.where(row < s - sh, pltpu.roll(dpre, s - sh, 0), 0.0)
            dw_ref[pl.ds(DN_CONV - 1 - sh, 1), :] = jnp.sum(
                dpre * jnp.where(row >= sh, pltpu.roll(x, sh, 0), 0.0), axis=0, keepdims=True)
        dx_ref[...] = dx

    return pl.pallas_call(
        body, name="gdn_prep_bwd", grid=(ncb,),
        in_specs=[pl.BlockSpec((None, s, 128), lambda c: (c, 0, 0)), pl.BlockSpec((s, 128), lambda c: (0, c0 + c)),
                  pl.BlockSpec((DN_CONV, 128), lambda c: (0, c))],
        out_specs=[pl.BlockSpec((s, 128), lambda c: (0, c)), pl.BlockSpec((DN_CONV, 128), lambda c: (0, c))],
        out_shape=[jax.ShapeDtypeStruct((s, ncb * 128), F32), jax.ShapeDtypeStruct((DN_CONV, ncb * 128), F32)],
        compiler_params=_params(("parallel",)),
    )(dact, proj, conv_w)


def _gates_math(ab, alog, dtb):
    lane = lax.broadcasted_iota(jnp.int32, ab.shape, 1)
    g = -jnp.exp(alog) * _softplus(ab + dtb)
    return jnp.where(lane < DN_HEADS, g, jnp.where(lane < 2 * DN_HEADS, _sigmoid(ab), 0.0))


def _gates_fwd(proj, alog, dtb):
    return _rowwise("gdn_gates_fwd", _gates_math, [(proj, 128, COL_DAB // 128)], [(128, F32)], bcast=[alog, dtb])[0]


def _gates_bwd(dgates, proj, alog, dtb):
    def fn(dv, ab, a, d):
        _, vjp = jax.vjp(_gates_math, ab, a, d)
        return vjp(dv)
    return _rowwise("gdn_gates_bwd", fn, [dgates, (proj, 128, COL_DAB // 128)], [(128, F32)], bcast=[alog, dtb],
                    accs=[(1, 128), (1, 128)])


def _l2n(x):
    return x * lax.rsqrt(jnp.sum(x * x, axis=-1, keepdims=True) + NORM_EPS)


def _gdn_intra_math(q, k, v, gcol, grow, bcol):
    c = DN_CHUNK
    ii = lax.broadcasted_iota(jnp.int32, (1, c, c), 1)
    jj = lax.broadcasted_iota(jnp.int32, (1, c, c), 2)
    gc_col = jnp.sum(jnp.where(ii >= jj, grow, 0.0), axis=2, keepdims=True)
    gc_row = jnp.sum(jnp.where(ii <= jj, gcol, 0.0), axis=1, keepdims=True)
    gc_last = jnp.sum(gcol, axis=1, keepdims=True)
    decay = jnp.exp(jnp.where(ii >= jj, gc_col - gc_row, NEG_INF))
    qn = _l2n(q) * (DN_DIM ** -0.5)
    kn = _l2n(k)
    kb = kn * bcol
    a_mat = jnp.where(ii > jj, _dot3(kb, kn, BNT) * decay, 0.0)
    nil = -a_mat
    t_inv = jnp.where(ii == jj, 1.0, 0.0) + nil
    for _ in range(5):
        nil = _dot3(nil, nil, BNN)
        t_inv = t_inv + _dot3(t_inv, nil, BNN)
    e_col = jnp.exp(gc_col)
    u = _dot3(t_inv, v * bcol, BNN)
    w = _dot3(t_inv, kb * e_col, BNN)
    attn = _dot3(qn, kn, BNT) * decay
    return u, w, attn, qn * e_col, kn * jnp.exp(gc_last - gc_col), jnp.broadcast_to(jnp.exp(gc_last), (DN_HEADS, 1, 128))


def _gdn_specs(s):
    nc = s // DN_CHUNK
    h, c, d = DN_HEADS, DN_CHUNK, DN_DIM
    return dict(
        qkv=pl.BlockSpec((3 * h, c, d), lambda n: (0, n, 0)),
        hcd=pl.BlockSpec((h, c, d), lambda n: (0, n, 0)),
        col=pl.BlockSpec((h, c, 1), lambda n: (0, n, 0)),
        row=pl.BlockSpec((h, None, 1, c), lambda n: (0, n, 0, 0)),
        att=pl.BlockSpec((h, c, c), lambda n: (0, n, 0)),
        dec=pl.BlockSpec((h, None, 1, 128), lambda n: (0, n, 0, 0)),
        s_hcd=jax.ShapeDtypeStruct((h, s, d), F32), s_col=jax.ShapeDtypeStruct((h, s, 1), F32),
        s_row=jax.ShapeDtypeStruct((h, nc, 1, c), F32), s_att=jax.ShapeDtypeStruct((h, s, c), F32),
        s_dec=jax.ShapeDtypeStruct((h, nc, 1, 128), F32), s_qkv=jax.ShapeDtypeStruct((3 * h, s, d), F32),
    )


def _gdn_intra_fwd(act, gcol, grow, bcol):
    s = act.shape[1]
    sp = _gdn_specs(s)
    h = DN_HEADS

    def body(a_ref, gc_ref, gr_ref, bc_ref, u_ref, w_ref, at_ref, qd_ref, kt_ref, dec_ref):
        outs = _gdn_intra_math(a_ref[0:h], a_ref[h:2 * h], a_ref[2 * h:3 * h], gc_ref[...], gr_ref[...], bc_ref[...])
        for ref, val in zip((u_ref, w_ref, at_ref, qd_ref, kt_ref, dec_ref), outs):
            ref[...] = val

    return pl.pallas_call(
        body, name="gdn_intra_fwd", grid=(s // DN_CHUNK,),
        in_specs=[sp["qkv"], sp["col"], sp["row"], sp["col"]],
        out_specs=[sp["hcd"], sp["hcd"], sp["att"], sp["hcd"], sp["hcd"], sp["dec"]],
        out_shape=[sp["s_hcd"], sp["s_hcd"], sp["s_att"], sp["s_hcd"], sp["s_hcd"], sp["s_dec"]],
        compiler_params=_params(("parallel",)),
    )(act, gcol, grow, bcol)


def _gdn_intra_bwd(act, gcol, grow, bcol, du, dw, dattn, dqd, dkt, ddec):
    s = act.shape[1]
    sp = _gdn_specs(s)
    h = DN_HEADS

    def body(a_ref, gc_ref, gr_ref, bc_ref, du_ref, dw_ref, dat_ref, dqd_ref, dkt_ref, dde_ref,
             dact_ref, dgc_ref, dgr_ref, dbc_ref):
        _, vjp = jax.vjp(_gdn_intra_math, a_ref[0:h], a_ref[h:2 * h], a_ref[2 * h:3 * h],
                         gc_ref[...], gr_ref[...], bc_ref[...])
        dq, dk, dv, dgc, dgr, dbc = vjp((du_ref[...], dw_ref[...], dat_ref[...], dqd_ref[...], dkt_ref[...], dde_ref[...]))
        dact_ref[0:h] = dq
        dact_ref[h:2 * h] = dk
        dact_ref[2 * h:3 * h] = dv
        dgc_ref[...] = dgc
        dgr_ref[...] = dgr
        dbc_ref[...] = dbc

    return pl.pallas_call(
        body, name="gdn_intra_bwd", grid=(s // DN_CHUNK,),
        in_specs=[sp["qkv"], sp["col"], sp["row"], sp["col"], sp["hcd"], sp["hcd"], sp["att"], sp["hcd"], sp["hcd"], sp["dec"]],
        out_specs=[sp["qkv"], sp["col"], sp["row"], sp["col"]],
        out_shape=[sp["s_qkv"], sp["s_col"], sp["s_row"], sp["s_col"]],
        compiler_params=_params(("parallel",)),
    )(act, gcol, grow, bcol, du, dw, dattn, dqd, dkt, ddec)


def _gdn_step_math(state, u, w, attn, qd, kt, dec):
    v_new = u - _dot3(w, state, BNN)
    o = _dot3(qd, state, BNN) + _dot3(attn, v_new, BNN)
    return state * dec[:, :, :1] + _dot3(kt, v_new, BTN), o


def _gdn_scan_fwd(u, w, attn, qd, kt, dec):
    s = u.shape[1]
    nc = s // DN_CHUNK
    sp = _gdn_specs(s)
    h, d = DN_HEADS, DN_DIM

    def body(u_ref, w_ref, at_ref, qd_ref, kt_ref, dec_ref, o_ref, st_ref, state):
        @pl.when(pl.program_id(0) == 0)
        def _():
            state[...] = jnp.zeros_like(state)

        st_ref[...] = state[...]
        new, o = _gdn_step_math(state[...], u_ref[...], w_ref[...], at_ref[...], qd_ref[...], kt_ref[...], dec_ref[...])
        state[...] = new
        o_ref[...] = o

    return pl.pallas_call(
        body, name="gdn_scan_fwd", grid=(nc,),
        in_specs=[sp["hcd"], sp["hcd"], sp["att"], sp["hcd"], sp["hcd"], sp["dec"]],
        out_specs=[sp["hcd"], pl.BlockSpec((None, h, d, d), lambda n: (n, 0, 0, 0))],
        out_shape=[sp["s_hcd"], jax.ShapeDtypeStruct((nc, h, d, d), F32)],
        scratch_shapes=[pltpu.VMEM((h, d, d), F32)],
        compiler_params=_params(("arbitrary",)),
    )(u, w, attn, qd, kt, dec)


def _gdn_scan_bwd(do, states, u, w, attn, qd, kt, dec):
    s = u.shape[1]
    nc = s // DN_CHUNK
    h, c, d = DN_HEADS, DN_CHUNK, DN_DIM
    rev = lambda n: nc - 1 - n
    hcd = pl.BlockSpec((h, c, d), lambda n: (0, rev(n), 0))
    att = pl.BlockSpec((h, c, c), lambda n: (0, rev(n), 0))
    dec_s = pl.BlockSpec((h, None, 1, 128), lambda n: (0, rev(n), 0, 0))
    sp = _gdn_specs(s)

    def body(do_ref, st_ref, u_ref, w_ref, at_ref, qd_ref, kt_ref, dec_ref,
             du_ref, dw_ref, dat_ref, dqd_ref, dkt_ref, dde_ref, dstate):
        @pl.when(pl.program_id(0) == 0)
        def _():
            dstate[...] = jnp.zeros_like(dstate)

        _, vjp = jax.vjp(_gdn_step_math, st_ref[...], u_ref[...], w_ref[...], at_ref[...], qd_ref[...], kt_ref[...],
                         dec_ref[...])
        dst, du, dw, dat, dqd, dkt, dde = vjp((dstate[...], do_ref[...]))
        dstate[...] = dst
        for ref, val in zip((du_ref, dw_ref, dat_ref, dqd_ref, dkt_ref, dde_ref), (du, dw, dat, dqd, dkt, dde)):
            ref[...] = val

    return pl.pallas_call(
        body, name="gdn_scan_bwd", grid=(nc,),
        in_specs=[hcd, pl.BlockSpec((None, h, d, d), lambda n: (rev(n), 0, 0, 0)), hcd, hcd, att, hcd, hcd, dec_s],
        out_specs=[hcd, hcd, att, hcd, hcd, dec_s],
        out_shape=[sp["s_hcd"], sp["s_hcd"], sp["s_att"], sp["s_hcd"], sp["s_hcd"], sp["s_dec"]],
        scratch_shapes=[pltpu.VMEM((h, d, d), F32)],
        compiler_params=_params(("arbitrary",)),
    )(do, states, u, w, attn, qd, kt, dec)


def _gdn_out_math(o, z, gain):
    return _rms(o, gain) * _silu(z)


def _gdn_out_fwd(o_rows, z_rows, gain):
    return _rowwise("gdn_out_fwd", _gdn_out_math, [o_rows, z_rows], [(DN_DIM, F32)], bcast=[gain])[0]


def _gdn_out_bwd(dy_rows, o_rows, z_rows, gain):
    def fn(dv, ov, zv, g):
        _, vjp = jax.vjp(_gdn_out_math, ov, zv, g)
        return vjp(dv)
    return _rowwise("gdn_out_bwd", fn, [dy_rows, o_rows, z_rows], [(DN_DIM, F32)] * 2, bcast=[gain], accs=[(1, DN_DIM)])


def _heads_major(x):
    s = x.shape[0]
    return x.reshape(s, -1, DN_DIM).transpose(1, 0, 2)


def _heads_minor(x):
    return x.transpose(1, 0, 2).reshape(x.shape[1], -1)


def _pad_row(v, width=128):
    return jnp.pad(v.reshape(1, -1), ((0, 0), (0, width - v.size)))


def _swiglu_bwd_math(ds, a, b):
    sg = _sigmoid(a)
    return ds * b * sg * (1.0 + a * (1.0 - sg)), ds * a * sg


def _ffn_up(name, h, wg, wu, widx, carry=None):
    s, d = h.shape
    j_n, fs = wg.shape[0], wg.shape[-1]
    tm = _row_tile(s)
    grid = (j_n, s // tm)
    wblk = pl.BlockSpec((None,) * (1 + len(widx)) + (d, fs), lambda j, i: (j, *widx, 0, 0))
    oblk = pl.BlockSpec((tm, fs), lambda j, i: (i, j))

    def body(h_ref, g_ref, u_ref, *refs):
        if carry is not None:
            _, a_ref, b_ref, s_ref, got_ref, send_sems, recv_sems = refs
            first, last = _grid_ends(grid)
            start, finish = _gather_parts(got_ref, send_sems, recv_sems)
            pl.when(first)(start)
        else:
            a_ref, b_ref, s_ref = refs
        hv = h_ref[...]
        a, b = _dotb(hv, g_ref[...]), _dotb(hv, u_ref[...])
        a_ref[...] = a.astype(BF16)
        b_ref[...] = b.astype(BF16)
        s_ref[...] = (_silu(a) * b).astype(BF16)
        if carry is not None:
            pl.when(last)(finish)

    in_specs, args = [pl.BlockSpec((tm, d), lambda j, i: (i, 0)), wblk, wblk], [h, wg, wu]
    out_specs, out_shape, scratch, aliases = [oblk] * 3, [jax.ShapeDtypeStruct((s, j_n * fs), BF16)] * 3, [], {}
    if carry is not None:
        in_specs.append(pl.BlockSpec(memory_space=pl.ANY))
        args.append(carry)
        out_specs.append(pl.BlockSpec(memory_space=pl.ANY))
        out_shape.append(jax.ShapeDtypeStruct(carry.shape, carry.dtype))
        scratch, aliases = [pltpu.SemaphoreType.DMA((6,)), pltpu.SemaphoreType.DMA((6,))], {3: 3}
    return pl.pallas_call(
        body, name=name, grid=grid, in_specs=in_specs, out_specs=out_specs, out_shape=out_shape,
        input_output_aliases=aliases, scratch_shapes=scratch,
        compiler_params=_params(("arbitrary", "arbitrary") if carry is not None else ("parallel", "arbitrary"), 56),
    )(*args)


def _ffn_down_bwd(name, df, wd, a, b, widx):
    s, d = df.shape
    j_n, fs = wd.shape[0], wd.shape[-2]
    tm = _row_tile(s)
    wblk = pl.BlockSpec((None,) * (1 + len(widx)) + (fs, d), lambda j, i: (j, *widx, 0, 0))
    oblk = pl.BlockSpec((tm, fs), lambda j, i: (i, j))

    def body(df_ref, w_ref, a_ref, b_ref, da_ref, db_ref):
        da, db = _swiglu_bwd_math(_dotb(df_ref[...], w_ref[...], NT), a_ref[...].astype(F32), b_ref[...].astype(F32))
        da_ref[...] = da.astype(BF16)
        db_ref[...] = db.astype(BF16)

    return pl.pallas_call(
        body, name=name, grid=(j_n, s // tm), in_specs=[pl.BlockSpec((tm, d), lambda j, i: (i, 0)), wblk, oblk, oblk],
        out_specs=[oblk] * 2, out_shape=[jax.ShapeDtypeStruct((s, j_n * fs), BF16)] * 2,
        compiler_params=_params(("parallel", "arbitrary"), 56),
    )(df, wd, a, b)


def _ffn_fwd(tag, x, gains_in, gains_out, wg, wu, wd, widx, carry_up=None, carry_down=None):
    h = _rms_fwd(f"{tag}_prenorm", x, gains_in)
    a, b, s, *got_up = _ffn_up(f"{tag}_up", h, wg, wu, widx, carry_up)
    f = _mm_rows(f"{tag}_down", s, wd, widx, F32, carry_down)
    f, got_down = f if carry_down is not None else (f, None)
    x_new = _rms_residual(f"{tag}_postnorm", x, f, gains_out, 0.5)
    return x_new, (x, h, a, b, s, f), (got_up[0] if got_up else None), got_down


def _ffn_bwd(tag, dx_new, saved, gains_in, gains_out, wg, wu, wd, widx, gbuf, slab, slab_dims):
    x, h, a, b, s, f = saved
    grad = lambda n, p, q, ac, bc, wt: _mm_grad(f"{tag}_{n}_grad", p, q, ac, bc, lead=slab, lead_dims=slab_dims,
                                               into=gbuf.get(n))
    df, dg_out = _rms_bwd(f"{tag}_postnorm_bwd", dx_new, f, gains_out, 0.5, out_dtype=BF16)
    da, db = _ffn_down_bwd(f"{tag}_down_bwd", df, wd, a, b, widx)
    gbuf["down"] = grad("down", s, df, wd.shape[0], None, wd)
    gbuf["gate"] = grad("gate", h, da, None, wg.shape[0], wg)
    gbuf["up"] = grad("up", h, db, None, wu.shape[0], wu)
    dh = _mm_cols_t(f"{tag}_gateup_bwd", [(da, wg), (db, wu)], widx, F32)
    dx, dg_in = _rms_bwd(f"{tag}_prenorm_bwd", dh, x, gains_in, 1.0, residual=dx_new)
    return dx, dg_in, dg_out


def _mixer_fwd(l, h, w, p, bias_tabs, carry_in=None):
    s = h.shape[0]
    nc = s // DN_CHUNK
    proj = _mm_cols("w_in_fwd", h, w["w_in"], (), F32, carry_in)
    proj, got_in = proj if carry_in is not None else (proj, None)
    ops, ops_vjp = jax.vjp(_s5_operators, p["ssm_lambda_re"][l], p["ssm_lambda_im"][l], p["ssm_b_re"][l], p["ssm_b_im"][l],
                           p["ssm_c_re"][l], p["ssm_c_im"][l], p["ssm_d"][l], p["ssm_log_dt"][l])
    u_g = _to_groups(proj[:, :SSM_WIDTH])
    y_g, s_g = _s5_scan_fwd(u_g, *ops)
    y = _from_groups(y_g)
    glu_w, glu_b, gain_ssm = p["ssm_glu_w"][l], p["ssm_glu_b"][l][None], p["ssm_out_gain"][l][None]
    y_ssm = _s5_post_fwd(y, glu_w, glu_b, gain_ssm)
    conv_w, alog, dtb = p["dn_conv_w"][l], _pad_row(p["dn_a_log"][l]), _pad_row(p["dn_dt_bias"][l])
    gain_dn = p["dn_norm_gain"][l][None]
    act = _gdn_prep_fwd(proj, conv_w)
    gates_t = _gates_fwd(proj, alog, dtb)[:, :2 * DN_HEADS].T
    gcol, bcol = gates_t[:DN_HEADS, :, None], gates_t[DN_HEADS:, :, None]
    grow = gates_t[:DN_HEADS].reshape(DN_HEADS, nc, 1, DN_CHUNK)
    u, wy, attn, qd, kt, dec = _gdn_intra_fwd(act, gcol, grow, bcol)
    o_dn, states = _gdn_scan_fwd(u, wy, attn, qd, kt, dec)
    o_rows = o_dn.reshape(DN_HEADS * s, DN_DIM)
    z_rows = _heads_major(proj[:, COL_DZ:COL_DAB]).reshape(DN_HEADS * s, DN_DIM)
    y_dn = _heads_minor(_gdn_out_fwd(o_rows, z_rows, gain_dn).reshape(DN_HEADS, s, DN_DIM))
    outs, lses = zip(*[_attn_branch_fwd(bi, dil, proj, bias_tabs) for bi, (_, dil) in enumerate(DILATED)])
    gain_at = p["attn_out_gain"][l][None]
    y_at, o_at, lse = _attn_merge_fwd(outs, lses, gain_at)
    mix = jnp.concatenate([y_ssm, y_dn, y_at], axis=1).astype(BF16)
    out = _mm_rows("w_out_fwd", mix, w["w_out"], (), F32)
    saved = dict(h=h, proj=proj, ops=ops, ops_vjp=ops_vjp, u_g=u_g, s_g=s_g, y=y, act=act, gcol=gcol, grow=grow, bcol=bcol,
                 u=u, wy=wy, attn=attn, qd=qd, kt=kt, dec=dec, states=states, o_rows=o_rows, z_rows=z_rows,
                 o_at=o_at, lse=lse, mix=mix)
    return out, saved, got_in


def _mixer_bwd(l, dout, sv, w, p, bias_tabs):
    s = dout.shape[0]
    proj = sv["proj"]
    g = {}
    g["w_out"] = _mm_grad("w_out_grad", sv["mix"], dout, w["w_out"].shape[0], None)
    dmix = _mm_rows_t("w_out_bwd", dout, w["w_out"], (), F32)
    d_ssm, d_dn, d_at = dmix[:, :SSM_WIDTH], dmix[:, SSM_WIDTH:SSM_WIDTH + DN_WIDTH], dmix[:, SSM_WIDTH + DN_WIDTH:]
    glu_w, glu_b, gain_ssm = p["ssm_glu_w"][l], p["ssm_glu_b"][l][None], p["ssm_out_gain"][l][None]
    dy, g["ssm_glu_w"], dglu_b, dgain_ssm = _s5_post_bwd(d_ssm, sv["y"], glu_w, glu_b, gain_ssm)
    g["ssm_glu_b"], g["ssm_out_gain"] = dglu_b[0], dgain_ssm[0]
    du_g, *d_ops = _s5_scan_bwd(_to_groups(dy), sv["u_g"], sv["s_g"], *sv["ops"])
    (g["ssm_lambda_re"], g["ssm_lambda_im"], g["ssm_b_re"], g["ssm_b_im"], g["ssm_c_re"], g["ssm_c_im"], g["ssm_d"],
     g["ssm_log_dt"]) = sv["ops_vjp"](tuple(d_ops))
    d_u = _from_groups(du_g)
    gain_at = p["attn_out_gain"][l][None]
    do, dlt, dgain_at = _attn_merge_bwd(d_at, sv["o_at"], gain_at)
    g["attn_out_gain"] = dgain_at[0]
    dqs, dks, dvs, dbs = zip(*[_attn_branch_bwd(bi, dil, proj, do, sv["lse"], dlt, bias_tabs)
                               for bi, (_, dil) in enumerate(DILATED)])
    dq, dk, dv = _add3("attn_dq_sum", *dqs), _add3("attn_dk_sum", *dks), _add3("attn_dv_sum", *dvs)
    g["bias_tabs"] = jnp.stack(dbs)
    conv_w, alog, dtb = p["dn_conv_w"][l], _pad_row(p["dn_a_log"][l]), _pad_row(p["dn_dt_bias"][l])
    gain_dn = p["dn_norm_gain"][l][None]
    dy_rows = _heads_major(d_dn).reshape(DN_HEADS * s, DN_DIM)
    do_rows, dz_rows, dgain_dn = _gdn_out_bwd(dy_rows, sv["o_rows"], sv["z_rows"], gain_dn)
    g["dn_norm_gain"] = dgain_dn[0]
    d_scan = _gdn_scan_bwd(do_rows.reshape(DN_HEADS, s, DN_DIM), sv["states"], sv["u"], sv["wy"], sv["attn"], sv["qd"],
                           sv["kt"], sv["dec"])
    dact, dgc, dgr, dbc = _gdn_intra_bwd(sv["act"], sv["gcol"], sv["grow"], sv["bcol"], *d_scan)
    dgates_t = jnp.concatenate([dgc[..., 0] + dgr.reshape(DN_HEADS, s), dbc[..., 0]], axis=0)
    dgates = jnp.pad(dgates_t.T, ((0, 0), (0, 128 - 2 * DN_HEADS)))
    dab, dalog, ddtb = _gates_bwd(dgates, proj, alog, dtb)
    g["dn_a_log"], g["dn_dt_bias"] = dalog[0, :DN_HEADS], ddtb[0, :DN_HEADS]
    dqkv, g["dn_conv_w"] = _gdn_prep_bwd(dact, proj, conv_w)
    dz = _heads_minor(dz_rows.reshape(DN_HEADS, s, DN_DIM))
    dproj = jnp.concatenate([d_u, dq, dk, dv, dqkv, dz, dab, jnp.zeros((s, N_IN_PAD - COL_DAB - 128), F32)], axis=1)
    g["w_in"] = _mm_grad("w_in_grad", sv["h"], dproj, None, 1)[0]
    dh = _mm_cols_t("w_in_bwd", [(dproj, w["w_in"])], (), F32)
    return dh, g


def _layer_weights(bufs):
    d_model = bufs["w_in"].shape[2] * 2
    w_in = jnp.transpose(bufs["w_in"].reshape(4, d_model, -1), (1, 0, 2)).reshape(d_model, N_IN)
    return {"ffn_w_gate": bufs["ffn_w_gate"], "ffn_w_up": bufs["ffn_w_up"], "ffn_w_down": bufs["ffn_w_down"],
            "w_out": bufs["w_out"].reshape(4, -1, d_model), "w_in": jnp.pad(w_in, ((0, 0), (0, N_IN_PAD - N_IN)))[None]}


def _local_step(x, target, first, later, p):
    depth = p["norm_gains"].shape[0]
    gains = p["norm_gains"]
    bias_tabs, bias_vjp = jax.vjp(_attn_bias_tables, p["rel_bias"])
    saved, weights = [], []
    bufs = first
    for l in range(depth):
        gn = lambda i: gains[l, i][None]
        w = _layer_weights(bufs)
        weights.append(w)
        nxt = later[l] if l < len(later) else {}
        ffn = (w["ffn_w_gate"], w["ffn_w_up"], w["ffn_w_down"])
        got = {}
        x, sv1, got["ffn_w_gate"], got["w_out"] = _ffn_fwd("ffn1", x, gn(0), gn(1), *ffn, (0,), nxt.get("ffn_w_gate"),
                                                          nxt.get("w_out"))
        h = _rms_fwd("mix_prenorm", x, gn(2))
        out, svm, got["w_in"] = _mixer_fwd(l, h, w, p, bias_tabs, nxt.get("w_in"))
        x_mid = x
        x = _rms_residual("mix_postnorm", x, out, gn(3), 1.0)
        x, sv2, got["ffn_w_up"], got["ffn_w_down"] = _ffn_fwd("ffn2", x, gn(4), gn(5), *ffn, (1,), nxt.get("ffn_w_up"),
                                                             nxt.get("ffn_w_down"))
        saved.append((sv1, svm, x_mid, out, sv2))
        bufs = got
    loss, dx = _loss_and_grad(x, target)

    small = ["ssm_lambda_re", "ssm_lambda_im", "ssm_b_re", "ssm_b_im", "ssm_c_re", "ssm_c_im", "ssm_d", "ssm_log_dt",
             "ssm_glu_w", "ssm_glu_b", "ssm_out_gain", "dn_conv_w", "dn_a_log", "dn_dt_bias", "dn_norm_gain", "attn_out_gain"]
    per_layer = {n: [None] * depth for n in small + ["norm_gains", "w_in", "w_out"]}
    d_tabs = None
    gbuf = {}
    for l in reversed(range(depth)):
        gn = lambda i: gains[l, i][None]
        sv1, svm, x_mid, out, sv2 = saved[l]
        w = weights[l]
        ffn = (w["ffn_w_gate"], w["ffn_w_up"], w["ffn_w_down"])
        dx, dg4, dg5 = _ffn_bwd("ffn2", dx, sv2, gn(4), gn(5), *ffn, (1,), gbuf, (l, 1), (depth, 2))
        dout, dg3 = _rms_bwd("mix_postnorm_bwd", dx, out, gn(3), 1.0)
        dh, gm = _mixer_bwd(l, dout, svm, w, p, bias_tabs)
        dx, dg2 = _rms_bwd("mix_prenorm_bwd", dh, x_mid, gn(2), 1.0, residual=dx)
        dx, dg0, dg1 = _ffn_bwd("ffn1", dx, sv1, gn(0), gn(1), *ffn, (0,), gbuf, (l, 0), (depth, 2))
        per_layer["norm_gains"][l] = jnp.concatenate([dg0, dg1, dg2, dg3, dg4, dg5], axis=0)
        d_tabs = gm["bias_tabs"] if d_tabs is None else d_tabs + gm["bias_tabs"]
        for n in small + ["w_in", "w_out"]:
            per_layer[n][l] = gm[n]
    grads = {n: jnp.stack(per_layer[n], axis=0) for n in small + ["norm_gains", "w_in"]}
    grads["w_out"] = jnp.stack(per_layer["w_out"], axis=1)
    grads["ffn_w_gate"], grads["ffn_w_up"], grads["ffn_w_down"] = gbuf["gate"], gbuf["up"], gbuf["down"]
    grads["rel_bias"] = bias_vjp(d_tabs)[0]
    return loss, dx, grads


_ANY = pl.BlockSpec(memory_space=pl.ANY)


def _place():
    return lax.axis_index("x"), lax.axis_index("y"), lax.axis_index("c")


DMA_CHUNKS = 16


class _Transfer:
    def __init__(self, make, src, dst):
        self.make, self.src, self.dst = make, src, dst

    def start(self):
        rows = self.src.shape[0]
        k = DMA_CHUNKS
        while k > 1 and rows % (16 * k):
            k //= 2
        step = rows // k
        for i in range(k):
            self.make(self.src.at[pl.ds(i * step, step)], self.dst.at[pl.ds(i * step, step)]).start()

    def whole(self):
        return self.make(self.src, self.dst)


def _rows_into_block(name, fn, ins, idx, n_blocks, out_dtype, layer=None):
    r, c = ins[0].shape[-2:]
    tm = min(_row_tile(r), 256)

    def body(idx_ref, *refs):
        refs[-1][...] = fn(*[t[...] for t in refs[:-1]]).astype(out_dtype)

    def spec(a):
        if a.ndim == 2:
            return pl.BlockSpec((tm, c), lambda i, b: (i, 0))
        if layer is not None:
            return pl.BlockSpec((None, tm, c), lambda i, b: (layer, i, 0))
        return pl.BlockSpec((a.shape[0], tm, c), lambda i, b: (0, i, 0))

    in_specs = [spec(a) for a in ins]
    return pl.pallas_call(
        body, name=name,
        grid_spec=pltpu.PrefetchScalarGridSpec(
            num_scalar_prefetch=1, grid=(r // tm,), in_specs=in_specs,
            out_specs=pl.BlockSpec((None, tm, c), lambda i, b: (b[0], i, 0))),
        out_shape=jax.ShapeDtypeStruct((n_blocks, r, c), out_dtype),
        compiler_params=_params(("parallel",), 48),
    )(idx, *ins)


def _gather_parts(out, send_sems, recv_sems):
    def place():
        x, y, c = _place()
        return c, (x, y, 1 - c), [(1 - x, y), (x, 1 - y), (1 - x, 1 - y)], 2 * x + y

    blk = lambda chip: 2 * chip[0] + chip[1]

    def copy(k, j, half, to):
        make = lambda s, d: pltpu.make_async_remote_copy(src_ref=s, dst_ref=d, send_sem=send_sems.at[k],
                                                         recv_sem=recv_sems.at[k], device_id=to, device_id_type=MESH)
        return _Transfer(make, out.at[j, half], out.at[j, half])

    def start():
        c, _, chips, me = place()
        for k, chip in enumerate(chips):
            copy(k, me, c, (*chip, c)).start()

    def finish():
        c, sib, chips, me = place()
        passed = [copy(3 + k, blk(chip), c, sib) for k, chip in enumerate(chips)]
        for k, chip in enumerate(chips):
            copy(k, blk(chip), c, (*chip, c)).whole().wait_recv()
            passed[k].start()
        for k, chip in enumerate(chips):
            copy(3 + k, blk(chip), 1 - c, sib).whole().wait_recv()
        for k, chip in enumerate(chips):
            copy(k, me, c, (*chip, c)).whole().wait_send()
        for cp in passed:
            cp.whole().wait_send()

    return start, finish


def _all_gather_xy(name, buf):
    def body(_, out, send_sems, recv_sems):
        start, finish = _gather_parts(out, send_sems, recv_sems)
        start()
        finish()

    return pl.pallas_call(
        body, name=name, in_specs=[_ANY], out_specs=_ANY, input_output_aliases={0: 0},
        out_shape=jax.ShapeDtypeStruct(buf.shape, buf.dtype),
        scratch_shapes=[pltpu.SemaphoreType.DMA((6,)), pltpu.SemaphoreType.DMA((6,))],
    )(buf)


def _pair_send(name, pieces):
    def body(src, dst, send_sems, recv_sems):
        x, y, c = _place()

        def copy(j, src_ref):
            make = lambda s, d: pltpu.make_async_remote_copy(src_ref=s, dst_ref=d, send_sem=send_sems.at[j],
                                                             recv_sem=recv_sems.at[j], device_id=(x, y, 1 - c),
                                                             device_id_type=MESH)
            return _Transfer(make, src_ref, dst.at[j])

        sends = [copy(j, src.at[j, 1 - c]) for j in range(4)]
        for cp in sends:
            cp.start()
        for j in range(4):
            copy(j, src.at[j, c]).whole().wait_recv()
        for cp in sends:
            cp.whole().wait_send()

    return pl.pallas_call(
        body, name=name, in_specs=[_ANY], out_specs=_ANY,
        out_shape=jax.ShapeDtypeStruct((4,) + pieces.shape[2:], pieces.dtype),
        scratch_shapes=[pltpu.SemaphoreType.DMA((4,)), pltpu.SemaphoreType.DMA((4,))],
    )(pieces)


def _pair_sum(name, pieces, recv, core):
    _, _, r, c = pieces.shape
    tm = min(_row_tile(r), 256)

    def body(core_ref, p_ref, r_ref, o_ref):
        o_ref[...] = (p_ref[...].astype(F32) + r_ref[...].astype(F32)).astype(o_ref.dtype)

    return pl.pallas_call(
        body, name=name,
        grid_spec=pltpu.PrefetchScalarGridSpec(
            num_scalar_prefetch=1, grid=(4, r // tm),
            in_specs=[pl.BlockSpec((None, None, tm, c), lambda j, i, b: (j, b[0], i, 0)),
                      pl.BlockSpec((None, tm, c), lambda j, i, b: (j, i, 0))],
            out_specs=pl.BlockSpec((None, tm, c), lambda j, i, b: (j, i, 0))),
        out_shape=jax.ShapeDtypeStruct((4, r, c), pieces.dtype),
        compiler_params=_params(("parallel", "parallel"), 48),
    )(core, pieces, recv)


def _exchange_xy(name, part):
    def body(src, dst, send_sems, recv_sems, local_sem):
        x, y, c = _place()
        chips = [(1 - x, y), (x, 1 - y), (1 - x, 1 - y)]
        blk = lambda cx, cy: 2 * cx + cy
        me = blk(x, y)

        def copy(k, chip, dst_block):
            make = lambda s, d: pltpu.make_async_remote_copy(src_ref=s, dst_ref=d, send_sem=send_sems.at[k],
                                                             recv_sem=recv_sems.at[k], device_id=(*chip, c),
                                                             device_id_type=MESH)
            return _Transfer(make, src.at[blk(*chip)], dst.at[dst_block])

        mine = _Transfer(lambda s, d: pltpu.make_async_copy(s, d, local_sem), src.at[me], dst.at[me])
        mine.start()
        sends = [copy(k, chip, me) for k, chip in enumerate(chips)]
        for cp in sends:
            cp.start()
        for k, chip in enumerate(chips):
            copy(k, chip, blk(*chip)).whole().wait_recv()
        for cp in sends:
            cp.whole().wait_send()
        mine.whole().wait()

    return pl.pallas_call(
        body, name=name, in_specs=[_ANY], out_specs=_ANY, out_shape=jax.ShapeDtypeStruct(part.shape, part.dtype),
        scratch_shapes=[pltpu.SemaphoreType.DMA((3,)), pltpu.SemaphoreType.DMA((3,)), pltpu.SemaphoreType.DMA],
    )(part)


def _exchange8(name, src, same_to_all=False):
    blk_shape = src.shape if same_to_all else src.shape[1:]

    def body(src_ref, dst, send_sems, recv_sems, local_sem):
        x, y, c = _place()
        me = 4 * x + 2 * y + c
        part = (lambda i: src_ref) if same_to_all else (lambda i: src_ref.at[i])

        def peer(k):
            return (1 - x if k & 4 else x, 1 - y if k & 2 else y, 1 - c if k & 1 else c)

        def copy(k, dst_block):
            px, py, pc = peer(k)
            make = lambda s, d: pltpu.make_async_remote_copy(src_ref=s, dst_ref=d, send_sem=send_sems.at[k - 1],
                                                             recv_sem=recv_sems.at[k - 1], device_id=(px, py, pc),
                                                             device_id_type=MESH)
            return _Transfer(make, part(4 * px + 2 * py + pc), dst.at[dst_block])

        mine = _Transfer(lambda s, d: pltpu.make_async_copy(s, d, local_sem), part(me), dst.at[me])
        mine.start()
        sends = [copy(k, me) for k in range(1, 8)]
        for cp in sends:
            cp.start()
        for k in range(1, 8):
            px, py, pc = peer(k)
            copy(k, 4 * px + 2 * py + pc).whole().wait_recv()
        for cp in sends:
            cp.whole().wait_send()
        mine.whole().wait()

    return pl.pallas_call(
        body, name=name, in_specs=[_ANY], out_specs=_ANY,
        out_shape=jax.ShapeDtypeStruct((8,) + blk_shape, src.dtype),
        scratch_shapes=[pltpu.SemaphoreType.DMA((7,)), pltpu.SemaphoreType.DMA((7,)), pltpu.SemaphoreType.DMA],
    )(src)


def _pair_swap(name, both):
    def body(_, out, send_sem, recv_sem):
        x, y, c = _place()
        remote = lambda s, d: pltpu.make_async_remote_copy(src_ref=s, dst_ref=d, send_sem=send_sem, recv_sem=recv_sem,
                                                           device_id=(x, y, 1 - c), device_id_type=MESH)
        push = _Transfer(remote, out.at[c], out.at[c])
        push.start()
        remote(out.at[c], out.at[1 - c]).wait_recv()
        push.whole().wait_send()

    return pl.pallas_call(
        body, name=name, in_specs=[_ANY], out_specs=_ANY, input_output_aliases={0: 0},
        out_shape=jax.ShapeDtypeStruct(both.shape, both.dtype),
        scratch_shapes=[pltpu.SemaphoreType.DMA, pltpu.SemaphoreType.DMA],
    )(both)


def _sum8(name, parts):
    _, r, c = parts.shape
    tm = min(_row_tile(r), 256)

    def body(p_ref, o_ref):
        o_ref[...] = _sum_blocks(p_ref[...])

    return pl.pallas_call(
        body, name=name, grid=(r // tm,), in_specs=[pl.BlockSpec((8, tm, c), lambda i: (0, i, 0))],
        out_specs=pl.BlockSpec((tm, c), lambda i: (i, 0)), out_shape=jax.ShapeDtypeStruct((r, c), F32),
        compiler_params=_params(("parallel",), 48),
    )(parts)


def _adamw(name, g, w, m, v):
    def fn(gv, wv, mv, vv):
        m2 = ADAM_B1 * mv + (1.0 - ADAM_B1) * gv
        v2 = ADAM_B2 * vv + (1.0 - ADAM_B2) * (gv * gv)
        m_hat = m2 / (1.0 - ADAM_B1 ** ADAM_STEP)
        v_hat = v2 / (1.0 - ADAM_B2 ** ADAM_STEP)
        return -ADAM_LR * (m_hat / (jnp.sqrt(v_hat) + ADAM_EPS) + ADAM_WD * wv), m2, v2
    return _rowwise(name, fn, [g, w, m, v], [(g.shape[1], F32)] * 3)


def _pack(tensors):
    flat = jnp.concatenate([t.reshape(-1).astype(F32) for t in tensors])
    rows = -(-flat.size // (128 * 512)) * 512
    return jnp.pad(flat, (0, rows * 128 - flat.size)).reshape(rows, 128)


def _unpack(buf, shapes):
    flat, out, at = buf.reshape(-1), [], 0
    for sh in shapes:
        n = math.prod(sh)
        out.append(flat[at:at + n].reshape(sh))
        at += n
    return out


_WEIGHTS = ("norm_gains", "ffn_w_gate", "ffn_w_up", "ffn_w_down", "w_in", "w_out", "ssm_lambda_re", "ssm_lambda_im",
            "ssm_b_re", "ssm_b_im", "ssm_c_re", "ssm_c_im", "ssm_d", "ssm_log_dt", "ssm_glu_w", "ssm_glu_b",
            "ssm_out_gain", "dn_conv_w", "dn_a_log", "dn_dt_bias", "dn_norm_gain", "attn_out_gain", "rel_bias")
_MATRICES = ("ffn_w_gate", "ffn_w_up", "ffn_w_down", "w_in", "w_out")
_CUT_SMALL = {"norm_gains": 2, "ssm_glu_w": 1, "dn_conv_w": 2}
_REPLICATED = tuple(n for n in _WEIGHTS if n not in _MATRICES and n not in _CUT_SMALL)


def _sum_blocks(x):
    acc = x[0].astype(F32)
    for i in range(1, x.shape[0]):
        acc = acc + x[i].astype(F32)
    return acc


def _gather_matrix(name, shard, chip, dtype=BF16):
    c = shard.shape[-1]
    buf = _rows_into_block(f"{name}_cast", lambda v: v, [shard.reshape(-1, c)], chip, 4, dtype)
    return _all_gather_xy(f"{name}_gather", buf.reshape(4, 2, -1, c)).reshape((4,) + shard.shape)


def _reduce_matrix(name, g, shard_shape, core):
    c = g.shape[-1]
    pieces = g.reshape(4, 2, -1, c)
    part = _pair_sum(f"{name}_pair_sum", pieces, _pair_send(f"{name}_pair", pieces), core)
    both = _rows_into_block(f"{name}_sum", _sum_blocks, [_exchange_xy(f"{name}_exchange", part)], core, 2, F32)
    return _pair_swap(f"{name}_swap", both).reshape(shard_shape)


def kernel(x, norm_gains, ffn_w_gate, ffn_w_up, ffn_w_down, w_in, w_out, ssm_lambda_re, ssm_lambda_im, ssm_b_re, ssm_b_im, ssm_c_re, ssm_c_im, ssm_d, ssm_log_dt, ssm_glu_w, ssm_glu_b, ssm_out_gain, dn_conv_w, dn_a_log, dn_dt_bias, dn_norm_gain, attn_out_gain, rel_bias, loss_target, m_norm_gains, m_ffn_w_gate, m_ffn_w_up, m_ffn_w_down, m_w_in, m_w_out, m_ssm_lambda_re, m_ssm_lambda_im, m_ssm_b_re, m_ssm_b_im, m_ssm_c_re, m_ssm_c_im, m_ssm_d, m_ssm_log_dt, m_ssm_glu_w, m_ssm_glu_b, m_ssm_out_gain, m_dn_conv_w, m_dn_a_log, m_dn_dt_bias, m_dn_norm_gain, m_attn_out_gain, m_rel_bias, v_norm_gains, v_ffn_w_gate, v_ffn_w_up, v_ffn_w_down, v_w_in, v_w_out, v_ssm_lambda_re, v_ssm_lambda_im, v_ssm_b_re, v_ssm_b_im, v_ssm_c_re, v_ssm_c_im, v_ssm_d, v_ssm_log_dt, v_ssm_glu_w, v_ssm_glu_b, v_ssm_out_gain, v_dn_conv_w, v_dn_a_log, v_dn_dt_bias, v_dn_norm_gain, v_attn_out_gain, v_rel_bias):
    wts = dict(zip(_WEIGHTS, (norm_gains, ffn_w_gate, ffn_w_up, ffn_w_down, w_in, w_out, ssm_lambda_re, ssm_lambda_im, ssm_b_re, ssm_b_im, ssm_c_re, ssm_c_im, ssm_d, ssm_log_dt, ssm_glu_w, ssm_glu_b, ssm_out_gain, dn_conv_w, dn_a_log, dn_dt_bias, dn_norm_gain, attn_out_gain, rel_bias)))
    mom = dict(zip(_WEIGHTS, (m_norm_gains, m_ffn_w_gate, m_ffn_w_up, m_ffn_w_down, m_w_in, m_w_out, m_ssm_lambda_re, m_ssm_lambda_im, m_ssm_b_re, m_ssm_b_im, m_ssm_c_re, m_ssm_c_im, m_ssm_d, m_ssm_log_dt, m_ssm_glu_w, m_ssm_glu_b, m_ssm_out_gain, m_dn_conv_w, m_dn_a_log, m_dn_dt_bias, m_dn_norm_gain, m_attn_out_gain, m_rel_bias)))
    var = dict(zip(_WEIGHTS, (v_norm_gains, v_ffn_w_gate, v_ffn_w_up, v_ffn_w_down, v_w_in, v_w_out, v_ssm_lambda_re, v_ssm_lambda_im, v_ssm_b_re, v_ssm_b_im, v_ssm_c_re, v_ssm_c_im, v_ssm_d, v_ssm_log_dt, v_ssm_glu_w, v_ssm_glu_b, v_ssm_out_gain, v_dn_conv_w, v_dn_a_log, v_dn_dt_bias, v_dn_norm_gain, v_attn_out_gain, v_rel_bias)))
    depth, d_model = norm_gains.shape[0], x.shape[-1]
    chip = 2 * lax.axis_index("x") + lax.axis_index("y")
    chip_idx = chip.astype(jnp.int32).reshape(1)
    core_idx = lax.axis_index("c").astype(jnp.int32).reshape(1)

    def layer_buffer(n, l):
        c = wts[n].shape[-1]
        rows = wts[n].reshape(depth, -1, c)
        return _rows_into_block(f"{n}_cast", lambda v: v, [rows], chip_idx, 4, BF16, layer=l).reshape(4, 2, -1, c)

    layers = [{n: layer_buffer(n, l) for n in _MATRICES} for l in range(depth)]
    first = {n: _all_gather_xy(f"{n}_gather", layers[0][n]) for n in _MATRICES}
    cut_names = tuple(_CUT_SMALL)
    cut_all = _gather_matrix("small", _pack([wts[n] for n in cut_names]), chip_idx, F32)
    p = {n: wts[n] for n in _REPLICATED}
    per_chip = [_unpack(cut_all[j], [wts[n].shape for n in cut_names]) for j in range(4)]
    for i, n in enumerate(cut_names):
        p[n] = jnp.concatenate([per_chip[j][i] for j in range(4)], axis=_CUT_SMALL[n])

    loss, dx, grads = _local_step(x[0], loss_target[0], first, layers[1:], p)
    loss = lax.psum(loss, ("x", "y", "c"))

    total = {}
    for n in ("ffn_w_gate", "ffn_w_up", "ffn_w_down", "w_out"):
        total[n] = _reduce_matrix(n, grads[n], wts[n].shape, core_idx)
    g_in = grads["w_in"][:, :, :N_IN].reshape(depth, d_model, 4, N_IN // 4)
    total["w_in"] = _reduce_matrix("w_in", jnp.transpose(g_in, (2, 0, 1, 3)), w_in.shape, core_idx)
    small_names = _REPLICATED + cut_names
    small_sum = _sum8("small_sum", _exchange8("small_exchange", _pack([grads[n] for n in small_names]), same_to_all=True))
    for n, g in zip(small_names, _unpack(small_sum, [grads[n].shape for n in small_names])):
        if n in _CUT_SMALL:
            cuts = jnp.split(g, 4, axis=_CUT_SMALL[n])
            g = functools.reduce(lambda acc, j: jnp.where(chip == j, cuts[j], acc), range(1, 4), cuts[0])
        total[n] = g

    delta, new_m, new_v = {}, {}, {}
    for n in _MATRICES + cut_names:
        c = wts[n].shape[-1]
        d2, m2, v2 = _adamw(f"{n}_adamw", total[n].reshape(-1, c), wts[n].reshape(-1, c), mom[n].reshape(-1, c),
                            var[n].reshape(-1, c))
        delta[n], new_m[n], new_v[n] = (t.reshape(wts[n].shape) for t in (d2, m2, v2))
    rep_shapes = [wts[n].shape for n in _REPLICATED]
    packed = _adamw("small_adamw", _pack([total[n] for n in _REPLICATED]), _pack([wts[n] for n in _REPLICATED]),
                    _pack([mom[n] for n in _REPLICATED]), _pack([var[n] for n in _REPLICATED]))
    for dst, buf in zip((delta, new_m, new_v), packed):
        dst.update(zip(_REPLICATED, _unpack(buf, rep_shapes)))

    return (loss, dx[None], *[total[n] for n in _WEIGHTS], *[delta[n] for n in _WEIGHTS],
            *[new_m[n] for n in _WEIGHTS], *[new_v[n] for n in _WEIGHTS])
```

```python
import functools
import math

import jax
import jax.numpy as jnp
from jax import lax
from jax.experimental import pallas as pl
from jax.experimental.pallas import tpu as pltpu

F32 = jnp.float32
BF16 = jnp.bfloat16
HI = lax.Precision.HIGHEST
MESH = pl.DeviceIdType.MESH

NORM_EPS = 1e-6
NEG_INF = -1e30
SSM_GROUPS, SSM_CH, SSM_STATE, SSM_WIDTH = 32, 16, 64, 512
SSM_T = 16
DN_HEADS, DN_DIM, DN_WIDTH, DN_CONV, DN_CHUNK = 6, 128, 768, 4, 64
AT_HEADS, AT_DIM, AT_WIDTH, AT_BLK = 6, 128, 768, 128
DILATED = ((128, 1), (512, 4), (2048, 16))
N_BUCKETS, REL_MAX = 32, 2048
N_IN = SSM_WIDTH + 3 * AT_WIDTH + 4 * DN_WIDTH + 2 * DN_HEADS
N_IN_PAD = 6144
COL_AQ, COL_AK, COL_AV, COL_DQKV, COL_DZ, COL_DAB = 512, 1280, 2048, 2816, 5120, 5888
ADAM_LR, ADAM_B1, ADAM_B2, ADAM_EPS, ADAM_WD, ADAM_STEP = 0.001, 0.9, 0.999, 1e-08, 0.01, 10
V7X_VMEM_BYTES = 64 * 1024 * 1024
NN = (((1,), (0,)), ((), ()))
NT = (((1,), (1,)), ((), ()))
TN = (((0,), (0,)), ((), ()))
BNN = (((2,), (1,)), ((0,), (0,)))
BNT = (((2,), (2,)), ((0,), (0,)))
BTN = (((1,), (1,)), ((0,), (0,)))


def _params(sem, vmem_mb=None):
    kw = {}
    if vmem_mb is not None:
        kw["vmem_limit_bytes"] = min(vmem_mb * 1024 * 1024, V7X_VMEM_BYTES - 8 * 1024 * 1024)
    return pltpu.CompilerParams(dimension_semantics=sem, **kw)


def _dotf(a, b, dims=NN):
    return lax.dot_general(a, b, dims, precision=HI, preferred_element_type=F32)


def _dot3(a, b, dims=NN):
    return lax.dot_general(a, b, dims, precision=lax.Precision.HIGH, preferred_element_type=F32)


def _dotb(a, b, dims=NN):
    return lax.dot_general(a.astype(BF16), b.astype(BF16), dims, preferred_element_type=F32)


def _sigmoid(x):
    return 1.0 / (1.0 + jnp.exp(-x))


def _silu(x):
    return x * _sigmoid(x)


def _softplus(x):
    return jnp.maximum(x, 0.0) + jnp.log(1.0 + jnp.exp(-jnp.abs(x)))


def _gelu(x):
    return 0.5 * x * (1.0 + jnp.tanh(math.sqrt(2.0 / math.pi) * (x + 0.044715 * x * x * x)))


def _rms(x, gain):
    return x * lax.rsqrt(jnp.mean(x * x, axis=-1, keepdims=True) + NORM_EPS) * gain


def _row_tile(s):
    for t in (512, 256, 128, 64, 32, 16, 8):
        if s % t == 0:
            return t
    return s


def _grid_ends(grid):
    ids = [pl.program_id(i) for i in range(len(grid))]
    first = functools.reduce(jnp.logical_and, [i == 0 for i in ids])
    last = functools.reduce(jnp.logical_and, [i == n - 1 for i, n in zip(ids, grid)])
    return first, last


def _mm(name, pairs, *, grid, a_blk, a_map, b_blk, b_map, o_shape, o_blk, o_map, dims, out_dtype=F32, vmem_mb=48,
        into=None, carry=None, exchange=None):
    n_red = grid[-1]
    n_pairs = len(pairs)
    comm = carry if carry is not None else exchange
    n_in = 2 * n_pairs + (into is not None) + (comm is not None)
    n_out = 1 + (comm is not None)
    acc_shape = tuple(d for d in o_blk if d is not None)

    def body(*refs):
        ins, o_ref, scr = refs[:2 * n_pairs], refs[n_in], list(refs[n_in + n_out:])
        if comm is not None:
            first, last = _grid_ends(grid)
            recv_sems, send_sems = scr.pop(), scr.pop()
            if carry is not None:
                start, finish = _gather_parts(refs[n_in + 1], send_sems, recv_sems)
            else:
                start, finish = _exchange_parts(refs[n_in - 1], refs[n_in + 1], send_sems, recv_sems)
            pl.when(first)(start)
        part = _dotb(ins[0][...], ins[1][...], dims)
        for p in range(1, n_pairs):
            part = part + _dotb(ins[2 * p][...], ins[2 * p + 1][...], dims)
        if n_red == 1:
            o_ref[...] = part.astype(o_ref.dtype)
        else:
            acc = scr[0]
            r = pl.program_id(len(grid) - 1)

            @pl.when(r == 0)
            def _():
                acc[...] = part

            @pl.when(r > 0)
            def _():
                acc[...] += part

            @pl.when(r == n_red - 1)
            def _():
                o_ref[...] = acc[...].astype(o_ref.dtype)
        if comm is not None:
            pl.when(last)(finish)

    in_specs, args = [], []
    for a, b in pairs:
        in_specs += [pl.BlockSpec(a_blk, a_map), pl.BlockSpec(b_blk, b_map)]
        args += [a, b]
    out_specs, out_shape = [pl.BlockSpec(o_blk, o_map)], [jax.ShapeDtypeStruct(o_shape, out_dtype)]
    scratch = [pltpu.VMEM(acc_shape, F32)] if n_red > 1 else []
    aliases = {}
    if into is not None:
        aliases[len(args)] = 0
        in_specs.append(pl.BlockSpec(memory_space=pl.ANY))
        args.append(into)
    if comm is not None:
        if carry is not None:
            aliases[len(args)] = 1
        in_specs.append(pl.BlockSpec(memory_space=pl.ANY))
        args.append(comm)
        out_specs.append(pl.BlockSpec(memory_space=pl.ANY))
        out_shape.append(jax.ShapeDtypeStruct(comm.shape, comm.dtype))
        scratch += [pltpu.SemaphoreType.DMA((6,)), pltpu.SemaphoreType.DMA((6,))]
    sem = ("arbitrary",) * len(grid) if comm is not None else ("parallel",) * (len(grid) - 1) + ("arbitrary",)
    res = pl.pallas_call(
        body, name=name, grid=grid, in_specs=in_specs, out_specs=out_specs, out_shape=out_shape,
        input_output_aliases=aliases, scratch_shapes=scratch, compiler_params=_params(sem, vmem_mb),
    )(*args)
    return res if comm is not None else res[0]


def _col_tile(n):
    for t in (1536, 1408, 1024, 768, 512, 384, 256, 128):
        if n % t == 0:
            return t
    return n


def _mm_cols(name, a, w, widx, out_dtype, carry=None):
    s, k = a.shape
    j_n, nj = w.shape[0], w.shape[-1]
    tm, tn = _row_tile(s), _col_tile(nj)
    nt = nj // tn
    lead = (None,) * (1 + len(widx))
    return _mm(name, [(a, w)], grid=(j_n, nt, s // tm, 1),
               a_blk=(tm, k), a_map=lambda j, c, i, r: (i, 0),
               b_blk=lead + (k, tn), b_map=lambda j, c, i, r: (j, *widx, 0, c),
               o_shape=(s, j_n * nj), o_blk=(tm, tn), o_map=lambda j, c, i, r: (i, j * nt + c),
               dims=NN, out_dtype=out_dtype, carry=carry)


def _mm_rows(name, a, w, widx, out_dtype, carry=None):
    s = a.shape[0]
    j_n, kj, n = w.shape[0], w.shape[-2], w.shape[-1]
    tm = _row_tile(s)
    lead = (None,) * (1 + len(widx))
    return _mm(name, [(a, w)], grid=(s // tm, j_n),
               a_blk=(tm, kj), a_map=lambda i, j: (i, j),
               b_blk=lead + (kj, n), b_map=lambda i, j: (j, *widx, 0, 0),
               o_shape=(s, n), o_blk=(tm, n), o_map=lambda i, j: (i, 0), dims=NN, out_dtype=out_dtype, carry=carry)


def _mm_rows_t(name, a, w, widx, out_dtype):
    s, n = a.shape
    j_n, kj = w.shape[0], w.shape[-2]
    tm = _row_tile(s)
    lead = (None,) * (1 + len(widx))
    return _mm(name, [(a, w)], grid=(j_n, s // tm, 1),
               a_blk=(tm, n), a_map=lambda j, i, r: (i, 0),
               b_blk=lead + (kj, n), b_map=lambda j, i, r: (j, *widx, 0, 0),
               o_shape=(s, j_n * kj), o_blk=(tm, kj), o_map=lambda j, i, r: (i, j), dims=NT, out_dtype=out_dtype)


def _mm_cols_t(name, pairs, widx, out_dtype, exchange=None):
    a0, w0 = pairs[0]
    s = a0.shape[0]
    j_n, k, nj = w0.shape[0], w0.shape[-2], w0.shape[-1]
    tm, tn = _row_tile(s), _col_tile(nj)
    nt = nj // tn
    lead = (None,) * (1 + len(widx))
    return _mm(name, pairs, grid=(s // tm, j_n * nt),
               a_blk=(tm, tn), a_map=lambda i, r: (i, r),
               b_blk=lead + (k, tn), b_map=lambda i, r: (lax.div(r, nt), *widx, 0, lax.rem(r, nt)),
               o_shape=(s, k), o_blk=(tm, k), o_map=lambda i, r: (i, 0), dims=NT, out_dtype=out_dtype, exchange=exchange)


def _mm_grad(name, a, b, a_cols, b_cols, out_dtype=BF16, lead=(), lead_dims=(), into=None, exchange=None):
    s = a.shape[0]
    tm = _row_tile(s)
    if a_cols is not None:
        j_n, ka, nb = a_cols, a.shape[1] // a_cols, b.shape[1]
        tk, tn = ka, _col_tile(nb)
        a_map = lambda j, kb, c, i: (i, j)
        b_map = lambda j, kb, c, i: (i, c)
    else:
        j_n, ka, nb = b_cols, a.shape[1], b.shape[1] // b_cols
        tk, tn = min(ka, 1024), _col_tile(nb)
        nt_ = nb // tn
        a_map = lambda j, kb, c, i: (i, kb)
        b_map = lambda j, kb, c, i: (i, j * nt_ + c)
    return _mm(name, [(a, b)], grid=(j_n, ka // tk, nb // tn, s // tm),
               a_blk=(tm, tk), a_map=a_map, b_blk=(tm, tn), b_map=b_map,
               o_shape=(j_n, *lead_dims, ka, nb), o_blk=(None,) * (1 + len(lead)) + (tk, tn),
               o_map=lambda j, kb, c, i: (j, *lead, kb, c), dims=TN, out_dtype=out_dtype, into=into, exchange=exchange)


def _rowwise(name, fn, rows, outs, *, bcast=(), accs=(), tm=None, vmem_mb=48):
    rows = [r if isinstance(r, tuple) else (r, r.shape[1], 0) for r in rows]
    s = rows[0][0].shape[0]
    tm = tm or min(_row_tile(s), 256)
    nr, nb, no = len(rows), len(bcast), len(outs)

    def body(*refs):
        o_refs, a_refs = refs[nr + nb:nr + nb + no], refs[nr + nb + no:]
        res = fn(*[r[...] for r in refs[:nr + nb]])
        if not isinstance(res, (tuple, list)):
            res = (res,)
        for o, v in zip(o_refs, res[:no]):
            o[...] = v.astype(o.dtype)
        if a_refs:
            i = pl.program_id(0)
            for a, v in zip(a_refs, res[no:]):
                @pl.when(i == 0)
                def _(a=a, v=v):
                    a[...] = v

                @pl.when(i > 0)
                def _(a=a, v=v):
                    a[...] += v

    in_specs = [pl.BlockSpec((tm, w), lambda i, c=c: (i, c)) for _, w, c in rows]
    in_specs += [pl.BlockSpec(b.shape, lambda i, nd=b.ndim: (0,) * nd) for b in bcast]
    out_specs = [pl.BlockSpec((tm, c), lambda i: (i, 0)) for c, _ in outs]
    out_specs += [pl.BlockSpec(sh, lambda i, nd=len(sh): (0,) * nd) for sh in accs]
    out_shape = [jax.ShapeDtypeStruct((s, c), dt) for c, dt in outs] + [jax.ShapeDtypeStruct(sh, F32) for sh in accs]
    res = pl.pallas_call(
        body, name=name, grid=(s // tm,), in_specs=in_specs, out_specs=out_specs, out_shape=out_shape,
        compiler_params=_params(("arbitrary",) if accs else ("parallel",), vmem_mb),
    )(*[r[0] for r in rows], *bcast)
    return res


def _rms_fwd(name, x, gain):
    return _rowwise(name, lambda xv, g: _rms(xv, g), [x], [(x.shape[1], BF16)], bcast=[gain])[0]


def _rms_residual(name, x, f, gain, scale):
    return _rowwise(name, lambda xv, fv, g: xv + scale * _rms(fv, g), [x, f], [(x.shape[1], F32)], bcast=[gain])[0]


def _rms_bwd_math(dy, x, gain):
    r = lax.rsqrt(jnp.mean(x * x, axis=-1, keepdims=True) + NORM_EPS)
    xh = x * r
    dxh = dy * gain
    dx = r * (dxh - xh * jnp.mean(dxh * xh, axis=-1, keepdims=True))
    return dx, jnp.sum(dy * xh, axis=0, keepdims=True)


def _rms_bwd(name, dy, x, gain, scale=1.0, residual=None, out_dtype=F32):
    d = x.shape[1]
    if residual is None:
        fn = lambda dyv, xv, g: _rms_bwd_math(scale * dyv.astype(F32), xv, g)
        rows = [dy, x]
    else:
        def fn(dyv, xv, rv, g):
            dx, dg = _rms_bwd_math(scale * dyv.astype(F32), xv, g)
            return dx + rv, dg
        rows = [dy, x, residual]
    return _rowwise(name, fn, rows, [(d, out_dtype)], bcast=[gain], accs=[(1, d)])


def _loss_and_grad(y, target):
    d = y.shape[1]

    def fn(yv, tv):
        e = yv - tv
        part = 0.5 * jnp.sum(jnp.mean(e * e, axis=-1, keepdims=True), axis=0, keepdims=True)
        return e * (1.0 / d), jnp.broadcast_to(part, (1, 128))
    dy, loss = _rowwise("loss_head", fn, [y, target], [(d, F32)], accs=[(1, 128)])
    return loss[0, 0], dy


def _s5_operators(lam_re, lam_im, b_re, b_im, c_re, c_im, d_skip, log_dt):
    t_n, ch, p_n = SSM_T, SSM_CH, SSM_STATE
    dt = jnp.exp(log_dt)[:, None]
    ld_re, ld_im = lam_re * dt, lam_im * dt
    k = jnp.arange(t_n + 1, dtype=F32)[None, :, None]
    mag = jnp.exp(ld_re[:, None, :] * k)
    pw_re, pw_im = mag * jnp.cos(ld_im[:, None, :] * k), mag * jnp.sin(ld_im[:, None, :] * k)
    lb_re, lb_im = pw_re[:, 1], pw_im[:, 1]
    den = lam_re * lam_re + lam_im * lam_im
    f_re = ((lb_re - 1.0) * lam_re + lb_im * lam_im) / den
    f_im = (lb_im * lam_re - (lb_re - 1.0) * lam_im) / den
    bb_re = f_re[..., None] * b_re - f_im[..., None] * b_im
    bb_im = f_re[..., None] * b_im + f_im[..., None] * b_re
    cp_re = c_re[:, None] * pw_re[:, :t_n, None, :] - c_im[:, None] * pw_im[:, :t_n, None, :]
    cp_im = c_re[:, None] * pw_im[:, :t_n, None, :] + c_im[:, None] * pw_re[:, :t_n, None, :]
    taps = (jnp.einsum("gtcp,gpd->gtcd", cp_re, bb_re, precision=HI)
            - jnp.einsum("gtcp,gpd->gtcd", cp_im, bb_im, precision=HI))
    m5 = jnp.stack([jnp.pad(taps[:, :t_n - j], ((0, 0), (j, 0), (0, 0), (0, 0))) for j in range(t_n)],
                   axis=1)
    m_op = jnp.transpose(m5, (0, 1, 4, 2, 3)).reshape(SSM_GROUPS, t_n * ch, t_n * ch)
    m_op = m_op + jnp.eye(t_n * ch, dtype=F32)[None] * jnp.tile(d_skip.reshape(SSM_GROUPS, 1, ch), (1, t_n, 1)).reshape(
        SSM_GROUPS, 1, t_n * ch)
    rv_re, rv_im = pw_re[:, t_n - 1::-1][:, :t_n], pw_im[:, t_n - 1::-1][:, :t_n]
    bo_re = rv_re[:, :, None, :] * jnp.swapaxes(bb_re, 1, 2)[:, None] - rv_im[:, :, None, :] * jnp.swapaxes(bb_im, 1, 2)[:, None]
    bo_im = rv_re[:, :, None, :] * jnp.swapaxes(bb_im, 1, 2)[:, None] + rv_im[:, :, None, :] * jnp.swapaxes(bb_re, 1, 2)[:, None]
    b_op = jnp.concatenate([bo_re, bo_im], axis=-1).reshape(SSM_GROUPS, t_n * ch, 2 * p_n)
    q_re = c_re[:, None] * pw_re[:, 1:, None, :] - c_im[:, None] * pw_im[:, 1:, None, :]
    q_im = c_re[:, None] * pw_im[:, 1:, None, :] + c_im[:, None] * pw_re[:, 1:, None, :]
    c_op = jnp.concatenate([q_re, -q_im], axis=-1).reshape(SSM_GROUPS, t_n * ch, 2 * p_n)
    c_op = jnp.swapaxes(c_op, 1, 2)
    a1 = jnp.concatenate([pw_re[:, t_n], pw_re[:, t_n]], axis=-1)[:, None, :]
    a2 = jnp.concatenate([-pw_im[:, t_n], pw_im[:, t_n]], axis=-1)[:, None, :]
    return m_op, b_op, c_op, a1, a2


def _s5_groups(name, fn, ins, out_dims):
    g_n = ins[0].shape[0]
    blk = lambda a, b: pl.BlockSpec((None, a, b), lambda g: (g, 0, 0))

    def body(*refs):
        res = fn(*[r[...] for r in refs[:len(ins)]])
        for o, v in zip(refs[len(ins):], res):
            o[...] = v

    return pl.pallas_call(
        body, name=name, grid=(g_n,), in_specs=[blk(*a.shape[1:]) for a in ins], out_specs=[blk(*d) for d in out_dims],
        out_shape=[jax.ShapeDtypeStruct((g_n,) + tuple(d), F32) for d in out_dims],
        compiler_params=_params(("parallel",)),
    )(*ins)


def _s5_state_scan(z_t, a1, a2):
    nc, g_n, p2 = z_t.shape

    def body(z_ref, a1_ref, a2_ref, s_ref):
        a1v, a2v = a1_ref[...], a2_ref[...]

        def step(n, s):
            s_ref[n] = s
            return a1v * s + a2v * pltpu.roll(s, SSM_STATE, 1) + z_ref[n]

        lax.fori_loop(0, nc, step, jnp.zeros((g_n, p2), F32))

    return pl.pallas_call(body, name="s5_state_scan", out_shape=jax.ShapeDtypeStruct(z_t.shape, F32))(z_t, a1, a2)


def _s5_state_scan_bwd(ds_t, s_t, a1, a2):
    nc, g_n, p2 = s_t.shape

    def body(ds_ref, s_ref, a1_ref, a2_ref, dz_ref, da1_ref, da2_ref):
        a1v, a2v = a1_ref[...], a2_ref[...]

        def step(k, carry):
            g, d1, d2 = carry
            n = nc - 1 - k
            dz_ref[n] = g
            sn = s_ref[n]
            d1 = d1 + g * sn
            d2 = d2 + g * pltpu.roll(sn, SSM_STATE, 1)
            g = ds_ref[n] + a1v * g + pltpu.roll(a2v * g, SSM_STATE, 1)
            return g, d1, d2

        zero = jnp.zeros((g_n, p2), F32)
        _, d1, d2 = lax.fori_loop(0, nc, step, (zero, zero, zero))
        da1_ref[...] = d1
        da2_ref[...] = d2

    row = jax.ShapeDtypeStruct((g_n, p2), F32)
    return pl.pallas_call(body, name="s5_state_scan_bwd",
                          out_shape=[jax.ShapeDtypeStruct(s_t.shape, F32), row, row])(ds_t, s_t, a1, a2)


def _s5_scan_fwd(u_g, m_op, b_op, c_op, a1, a2):
    _, nc, w = u_g.shape
    p2 = 2 * SSM_STATE
    z_g, = _s5_groups("s5_chunk_inputs", lambda u, b: (_dotf(u, b),), [u_g, b_op], [(nc, p2)])
    s_g = jnp.swapaxes(_s5_state_scan(jnp.swapaxes(z_g, 0, 1), a1[:, 0], a2[:, 0]), 0, 1)
    y_g, = _s5_groups("s5_outputs", lambda u, s, m, c: (_dotf(u, m) + _dotf(s, c),), [u_g, s_g, m_op, c_op], [(nc, w)])
    return y_g, s_g


def _s5_scan_bwd(dy_g, u_g, s_g, m_op, b_op, c_op, a1, a2):
    _, nc, w = u_g.shape
    p2 = 2 * SSM_STATE
    ds_g, = _s5_groups("s5_dstate", lambda dy, c: (_dotf(dy, c, NT),), [dy_g, c_op], [(nc, p2)])
    dz_t, da1, da2 = _s5_state_scan_bwd(jnp.swapaxes(ds_g, 0, 1), jnp.swapaxes(s_g, 0, 1), a1[:, 0], a2[:, 0])
    dz_g = jnp.swapaxes(dz_t, 0, 1)

    def grads(dy, dz, u, s, m, b):
        return _dotf(dy, m, NT) + _dotf(dz, b, NT), _dotf(u, dy, TN), _dotf(u, dz, TN), _dotf(s, dy, TN)

    du, dm, db, dc = _s5_groups("s5_grads", grads, [dy_g, dz_g, u_g, s_g, m_op, b_op], [(nc, w), (w, w), (w, p2), (p2, w)])
    return du, dm, db, dc, da1[:, None], da2[:, None]


def _to_groups(u):
    s = u.shape[0]
    return u.reshape(s // SSM_T, SSM_T, SSM_GROUPS, SSM_CH).transpose(2, 0, 1, 3).reshape(
        SSM_GROUPS, s // SSM_T, SSM_T * SSM_CH)


def _from_groups(y_g):
    nc = y_g.shape[1]
    return y_g.reshape(SSM_GROUPS, nc, SSM_T, SSM_CH).transpose(1, 2, 0, 3).reshape(nc * SSM_T, SSM_WIDTH)


def _s5_post_math(y, glu_w, glu_b, gain):
    y2 = _gelu(y)
    o = y2 * _sigmoid(_dotb(y2, glu_w) + glu_b)
    return _rms(o, gain)


def _s5_post_fwd(y, glu_w, glu_b, gain):
    return _rowwise("s5_post_fwd", _s5_post_math, [y], [(SSM_WIDTH, F32)], bcast=[glu_w, glu_b, gain])[0]


def _s5_post_bwd(dout, y, glu_w, glu_b, gain):
    def fn(dv, yv, w, b, g):
        _, vjp = jax.vjp(_s5_post_math, yv, w, b, g)
        return vjp(dv)
    return _rowwise("s5_post_bwd", fn, [dout, y], [(SSM_WIDTH, F32)], bcast=[glu_w, glu_b, gain],
                    accs=[(SSM_WIDTH, SSM_WIDTH), (1, SSM_WIDTH), (1, SSM_WIDTH)])


def _t5_bucket(dist):
    max_exact = N_BUCKETS // 2
    d = jnp.maximum(dist, 1).astype(F32)
    large = max_exact + jnp.log(d / max_exact) / math.log(REL_MAX / max_exact) * (N_BUCKETS - max_exact)
    large = jnp.minimum(large.astype(jnp.int32), N_BUCKETS - 1)
    return jnp.where(dist < max_exact, dist, large)


def _attn_bias_tables(rel_bias):
    blk = AT_BLK
    tabs = []
    for window, dil in DILATED:
        rel = blk + jnp.arange(blk)[:, None] - jnp.arange(2 * blk)[None, :]
        valid = (rel >= 0) & (rel <= window // dil)
        bias = jnp.moveaxis(rel_bias[_t5_bucket(jnp.maximum(rel, 0) * dil)], -1, 0)
        tabs.append(jnp.where(valid[None], bias, NEG_INF))
    return jnp.stack(tabs)


def _attn_specs(s):
    col = lambda first: pl.BlockSpec((s, AT_DIM), lambda h, r: (0, first // AT_DIM + h))
    return col(COL_AQ), col(COL_AK), col(COL_AV), col(0)


def _attn_branch_fwd(b_idx, dil, proj, bias):
    s, blk, dh = proj.shape[0], AT_BLK, AT_DIM
    nb = s // dil // blk
    scale = dh ** -0.5

    def body(q_ref, k_ref, v_ref, b_ref, o_ref, l_ref):
        r = pl.program_id(1)

        def block(n, carry):
            cur = pl.ds(r + n * (blk * dil), blk, stride=dil)
            prv = pl.ds(r + jnp.maximum(n - 1, 0) * (blk * dil), blk, stride=dil)
            q = q_ref[cur, :] * scale
            lc = _dotb(q, k_ref[cur, :], NT) + b_ref[:, blk:]
            lp = _dotb(q, k_ref[prv, :], NT) + b_ref[:, :blk]
            lp = jnp.where(n > 0, lp, NEG_INF)
            m = jnp.maximum(jnp.max(lc, axis=1, keepdims=True), jnp.max(lp, axis=1, keepdims=True))
            pc, pp = jnp.exp(lc - m), jnp.exp(lp - m)
            den = jnp.sum(pc, axis=1, keepdims=True) + jnp.sum(pp, axis=1, keepdims=True)
            inv = 1.0 / den
            o_ref[cur, :] = _dotb(pc * inv, v_ref[cur, :]) + _dotb(pp * inv, v_ref[prv, :])
            l_ref[cur, :] = jnp.broadcast_to(m + jnp.log(den), (blk, dh))
            return carry

        lax.fori_loop(0, nb, block, 0)

    q_s, k_s, v_s, out_s = _attn_specs(s)
    return pl.pallas_call(
        body, name=f"attn_fwd_d{dil}", grid=(AT_HEADS, dil),
        in_specs=[q_s, k_s, v_s, pl.BlockSpec((None, None, blk, 2 * blk), lambda h, r: (b_idx, h, 0, 0))],
        out_specs=[out_s, out_s],
        out_shape=[jax.ShapeDtypeStruct((s, AT_WIDTH), F32)] * 2,
        compiler_params=_params(("parallel", "arbitrary")),
    )(proj, proj, proj, bias)


def _attn_branch_bwd(b_idx, dil, proj, do, lse, dlt, bias):
    s, blk, dh = proj.shape[0], AT_BLK, AT_DIM
    nb = s // dil // blk
    scale = dh ** -0.5

    def body(q_ref, k_ref, v_ref, do_ref, l_ref, d_ref, b_ref, dq_ref, dk_ref, dv_ref, db_ref):
        r = pl.program_id(1)

        @pl.when(r == 0)
        def _():
            dk_ref[...] = jnp.zeros_like(dk_ref)
            dv_ref[...] = jnp.zeros_like(dv_ref)
            db_ref[...] = jnp.zeros_like(db_ref)

        def block(n, carry):
            cur = pl.ds(r + n * (blk * dil), blk, stride=dil)
            prv = pl.ds(r + jnp.maximum(n - 1, 0) * (blk * dil), blk, stride=dil)
            q = q_ref[cur, :] * scale
            do_b = do_ref[cur, :]
            lse_b = l_ref[cur, :][:, :1]
            dlt_b = d_ref[cur, :][:, :1]
            kc, kp = k_ref[cur, :], k_ref[prv, :]
            vc, vp = v_ref[cur, :], v_ref[prv, :]
            lc = _dotb(q, kc, NT) + b_ref[:, blk:]
            lp = jnp.where(n > 0, _dotb(q, kp, NT) + b_ref[:, :blk], NEG_INF)
            pc, pp = jnp.exp(lc - lse_b), jnp.exp(lp - lse_b)
            dsc = pc * (_dotb(do_b, vc, NT) - dlt_b)
            dsp = pp * (_dotb(do_b, vp, NT) - dlt_b)
            dq_ref[cur, :] = (_dotb(dsc, kc) + _dotb(dsp, kp)) * scale
            dk_ref[cur, :] = dk_ref[cur, :] + _dotb(dsc, q, TN)
            dk_ref[prv, :] = dk_ref[prv, :] + _dotb(dsp, q, TN)
            dv_ref[cur, :] = dv_ref[cur, :] + _dotb(pc, do_b, TN)
            dv_ref[prv, :] = dv_ref[prv, :] + _dotb(pp, do_b, TN)
            db_ref[:, blk:] += dsc
            db_ref[:, :blk] += dsp
            return carry

        lax.fori_loop(0, nb, block, 0)

    q_s, k_s, v_s, out_s = _attn_specs(s)
    tab = pl.BlockSpec((None, None, blk, 2 * blk), lambda h, r: (b_idx, h, 0, 0))
    return pl.pallas_call(
        body, name=f"attn_bwd_d{dil}", grid=(AT_HEADS, dil),
        in_specs=[q_s, k_s, v_s, out_s, out_s, out_s, tab],
        out_specs=[out_s, out_s, out_s, pl.BlockSpec((None, blk, 2 * blk), lambda h, r: (h, 0, 0))],
        out_shape=[jax.ShapeDtypeStruct((s, AT_WIDTH), F32)] * 3 + [jax.ShapeDtypeStruct((AT_HEADS, blk, 2 * blk), F32)],
        compiler_params=_params(("parallel", "arbitrary"), 56),
    )(proj, proj, proj, do, lse, dlt, bias)


def _per_head(fn, *xs):
    return jnp.concatenate([fn(*[x[:, h * AT_DIM:(h + 1) * AT_DIM] for x in xs]) for h in range(AT_HEADS)], axis=1)


def _attn_merge_math(o1, o2, o3, l1, l2, l3, gain):
    m = jnp.maximum(jnp.maximum(l1, l2), l3)
    e1, e2, e3 = jnp.exp(l1 - m), jnp.exp(l2 - m), jnp.exp(l3 - m)
    den = e1 + e2 + e3
    o = (e1 * o1 + e2 * o2 + e3 * o3) / den
    return _rms(o, gain), o, m + jnp.log(den)


def _attn_merge_fwd(os_, ls_, gain):
    w = AT_WIDTH
    return _rowwise("attn_merge_fwd", _attn_merge_math, [*os_, *ls_], [(w, F32)] * 3, bcast=[gain])


def _attn_merge_bwd(dy, o, gain):
    def fn(dyv, ov, g):
        do, dg = _rms_bwd_math(dyv, ov, g)
        dlt = _per_head(lambda a, b: jnp.broadcast_to(jnp.sum(a * b, axis=1, keepdims=True), a.shape), do, ov)
        return do, dlt, dg
    return _rowwise("attn_merge_bwd", fn, [dy, o], [(AT_WIDTH, F32)] * 2, bcast=[gain], accs=[(1, AT_WIDTH)])


def _add3(name, a, b, c):
    return _rowwise(name, lambda x, y, z: x + y + z, [a, b, c], [(a.shape[1], F32)])[0]


def _conv_taps(x, w):
    row = lax.broadcasted_iota(jnp.int32, x.shape, 0)
    y = w[DN_CONV - 1:DN_CONV, :] * x
    for sh in range(1, DN_CONV):
        y = y + w[DN_CONV - 1 - sh:DN_CONV - sh, :] * jnp.where(row >= sh, pltpu.roll(x, sh, 0), 0.0)
    return y


def _gdn_prep_fwd(proj, conv_w):
    s = proj.shape[0]
    ncb = 3 * DN_HEADS
    c0 = COL_DQKV // 128

    def body(x_ref, w_ref, o_ref):
        o_ref[...] = _silu(_conv_taps(x_ref[...], w_ref[...]))

    return pl.pallas_call(
        body, name="gdn_prep_fwd", grid=(ncb,),
        in_specs=[pl.BlockSpec((s, 128), lambda c: (0, c0 + c)), pl.BlockSpec((DN_CONV, 128), lambda c: (0, c))],
        out_specs=pl.BlockSpec((None, s, 128), lambda c: (c, 0, 0)),
        out_shape=jax.ShapeDtypeStruct((ncb, s, 128), F32),
        compiler_params=_params(("parallel",)),
    )(proj, conv_w)


def _gdn_prep_bwd(dact, proj, conv_w):
    s = proj.shape[0]
    ncb = 3 * DN_HEADS
    c0 = COL_DQKV // 128

    def body(d_ref, x_ref, w_ref, dx_ref, dw_ref):
        x, w = x_ref[...], w_ref[...]
        pre = _conv_taps(x, w)
        sg = _sigmoid(pre)
        dpre = d_ref[...] * sg * (1.0 + pre * (1.0 - sg))
        row = lax.broadcasted_iota(jnp.int32, x.shape, 0)
        dx = w[DN_CONV - 1:DN_CONV, :] * dpre
        dw_ref[pl.ds(DN_CONV - 1, 1), :] = jnp.sum(dpre * x, axis=0, keepdims=True)
        for sh in range(1, DN_CONV):
            dx = dx + w[DN_CONV - 1 - sh:DN_CONV - sh, :] * jnp.where(row < s - sh, pltpu.roll(dpre, s - sh, 0), 0.0)
            dw_ref[pl.ds(DN_CONV - 1 - sh, 1), :] = jnp.sum(
                dpre * jnp.where(row >= sh, pltpu.roll(x, sh, 0), 0.0), axis=0, keepdims=True)
        dx_ref[...] = dx

    return pl.pallas_call(
        body, name="gdn_prep_bwd", grid=(ncb,),
        in_specs=[pl.BlockSpec((None, s, 128), lambda c: (c, 0, 0)), pl.BlockSpec((s, 128), lambda c: (0, c0 + c)),
                  pl.BlockSpec((DN_CONV, 128), lambda c: (0, c))],
        out_specs=[pl.BlockSpec((s, 128), lambda c: (0, c)), pl.BlockSpec((DN_CONV, 128), lambda c: (0, c))],
        out_shape=[jax.ShapeDtypeStruct((s, ncb * 128), F32), jax.ShapeDtypeStruct((DN_CONV, ncb * 128), F32)],
        compiler_params=_params(("parallel",)),
    )(dact, proj, conv_w)


def _gates_math(ab, alog, dtb):
    lane = lax.broadcasted_iota(jnp.int32, ab.shape, 1)
    g = -jnp.exp(alog) * _softplus(ab + dtb)
    return jnp.where(lane < DN_HEADS, g, jnp.where(lane < 2 * DN_HEADS, _sigmoid(ab), 0.0))


def _gates_fwd(proj, alog, dtb):
    return _rowwise("gdn_gates_fwd", _gates_math, [(proj, 128, COL_DAB // 128)], [(128, F32)], bcast=[alog, dtb])[0]


def _gates_bwd(dgates, proj, alog, dtb):
    def fn(dv, ab, a, d):
        _, vjp = jax.vjp(_gates_math, ab, a, d)
        return vjp(dv)
    return _rowwise("gdn_gates_bwd", fn, [dgates, (proj, 128, COL_DAB // 128)], [(128, F32)], bcast=[alog, dtb],
                    accs=[(1, 128), (1, 128)])


def _l2n(x):
    return x * lax.rsqrt(jnp.sum(x * x, axis=-1, keepdims=True) + NORM_EPS)


def _gdn_intra_math(q, k, v, gcol, grow, bcol):
    c = DN_CHUNK
    ii = lax.broadcasted_iota(jnp.int32, (1, c, c), 1)
    jj = lax.broadcasted_iota(jnp.int32, (1, c, c), 2)
    gc_col = jnp.sum(jnp.where(ii >= jj, grow, 0.0), axis=2, keepdims=True)
    gc_row = jnp.sum(jnp.where(ii <= jj, gcol, 0.0), axis=1, keepdims=True)
    gc_last = jnp.sum(gcol, axis=1, keepdims=True)
    decay = jnp.exp(jnp.where(ii >= jj, gc_col - gc_row, NEG_INF))
    qn = _l2n(q) * (DN_DIM ** -0.5)
    kn = _l2n(k)
    kb = kn * bcol
    a_mat = jnp.where(ii > jj, _dot3(kb, kn, BNT) * decay, 0.0)
    nil = -a_mat
    t_inv = jnp.where(ii == jj, 1.0, 0.0) + nil
    for _ in range(5):
        nil = _dot3(nil, nil, BNN)
        t_inv = t_inv + _dot3(t_inv, nil, BNN)
    e_col = jnp.exp(gc_col)
    u = _dot3(t_inv, v * bcol, BNN)
    w = _dot3(t_inv, kb * e_col, BNN)
    attn = _dot3(qn, kn, BNT) * decay
    return u, w, attn, qn * e_col, kn * jnp.exp(gc_last - gc_col), jnp.broadcast_to(jnp.exp(gc_last), (DN_HEADS, 1, 128))


def _gdn_specs(s):
    nc = s // DN_CHUNK
    h, c, d = DN_HEADS, DN_CHUNK, DN_DIM
    return dict(
        qkv=pl.BlockSpec((3 * h, c, d), lambda n: (0, n, 0)),
        hcd=pl.BlockSpec((h, c, d), lambda n: (0, n, 0)),
        col=pl.BlockSpec((h, c, 1), lambda n: (0, n, 0)),
        row=pl.BlockSpec((h, None, 1, c), lambda n: (0, n, 0, 0)),
        att=pl.BlockSpec((h, c, c), lambda n: (0, n, 0)),
        dec=pl.BlockSpec((h, None, 1, 128), lambda n: (0, n, 0, 0)),
        s_hcd=jax.ShapeDtypeStruct((h, s, d), F32), s_col=jax.ShapeDtypeStruct((h, s, 1), F32),
        s_row=jax.ShapeDtypeStruct((h, nc, 1, c), F32), s_att=jax.ShapeDtypeStruct((h, s, c), F32),
        s_dec=jax.ShapeDtypeStruct((h, nc, 1, 128), F32), s_qkv=jax.ShapeDtypeStruct((3 * h, s, d), F32),
    )


def _gdn_intra_fwd(act, gcol, grow, bcol):
    s = act.shape[1]
    sp = _gdn_specs(s)
    h = DN_HEADS

    def body(a_ref, gc_ref, gr_ref, bc_ref, u_ref, w_ref, at_ref, qd_ref, kt_ref, dec_ref):
        outs = _gdn_intra_math(a_ref[0:h], a_ref[h:2 * h], a_ref[2 * h:3 * h], gc_ref[...], gr_ref[...], bc_ref[...])
        for ref, val in zip((u_ref, w_ref, at_ref, qd_ref, kt_ref, dec_ref), outs):
            ref[...] = val

    return pl.pallas_call(
        body, name="gdn_intra_fwd", grid=(s // DN_CHUNK,),
        in_specs=[sp["qkv"], sp["col"], sp["row"], sp["col"]],
        out_specs=[sp["hcd"], sp["hcd"], sp["att"], sp["hcd"], sp["hcd"], sp["dec"]],
        out_shape=[sp["s_hcd"], sp["s_hcd"], sp["s_att"], sp["s_hcd"], sp["s_hcd"], sp["s_dec"]],
        compiler_params=_params(("parallel",)),
    )(act, gcol, grow, bcol)


def _gdn_intra_bwd(act, gcol, grow, bcol, du, dw, dattn, dqd, dkt, ddec):
    s = act.shape[1]
    sp = _gdn_specs(s)
    h = DN_HEADS

    def body(a_ref, gc_ref, gr_ref, bc_ref, du_ref, dw_ref, dat_ref, dqd_ref, dkt_ref, dde_ref,
             dact_ref, dgc_ref, dgr_ref, dbc_ref):
        _, vjp = jax.vjp(_gdn_intra_math, a_ref[0:h], a_ref[h:2 * h], a_ref[2 * h:3 * h],
                         gc_ref[...], gr_ref[...], bc_ref[...])
        dq, dk, dv, dgc, dgr, dbc = vjp((du_ref[...], dw_ref[...], dat_ref[...], dqd_ref[...], dkt_ref[...], dde_ref[...]))
        dact_ref[0:h] = dq
        dact_ref[h:2 * h] = dk
        dact_ref[2 * h:3 * h] = dv
        dgc_ref[...] = dgc
        dgr_ref[...] = dgr
        dbc_ref[...] = dbc

    return pl.pallas_call(
        body, name="gdn_intra_bwd", grid=(s // DN_CHUNK,),
        in_specs=[sp["qkv"], sp["col"], sp["row"], sp["col"], sp["hcd"], sp["hcd"], sp["att"], sp["hcd"], sp["hcd"], sp["dec"]],
        out_specs=[sp["qkv"], sp["col"], sp["row"], sp["col"]],
        out_shape=[sp["s_qkv"], sp["s_col"], sp["s_row"], sp["s_col"]],
        compiler_params=_params(("parallel",)),
    )(act, gcol, grow, bcol, du, dw, dattn, dqd, dkt, ddec)


def _gdn_step_math(state, u, w, attn, qd, kt, dec):
    v_new = u - _dot3(w, state, BNN)
    o = _dot3(qd, state, BNN) + _dot3(attn, v_new, BNN)
    return state * dec[:, :, :1] + _dot3(kt, v_new, BTN), o


def _gdn_scan_fwd(u, w, attn, qd, kt, dec):
    s = u.shape[1]
    nc = s // DN_CHUNK
    sp = _gdn_specs(s)
    h, d = DN_HEADS, DN_DIM

    def body(u_ref, w_ref, at_ref, qd_ref, kt_ref, dec_ref, o_ref, st_ref, state):
        @pl.when(pl.program_id(0) == 0)
        def _():
            state[...] = jnp.zeros_like(state)

        st_ref[...] = state[...]
        new, o = _gdn_step_math(state[...], u_ref[...], w_ref[...], at_ref[...], qd_ref[...], kt_ref[...], dec_ref[...])
        state[...] = new
        o_ref[...] = o

    return pl.pallas_call(
        body, name="gdn_scan_fwd", grid=(nc,),
        in_specs=[sp["hcd"], sp["hcd"], sp["att"], sp["hcd"], sp["hcd"], sp["dec"]],
        out_specs=[sp["hcd"], pl.BlockSpec((None, h, d, d), lambda n: (n, 0, 0, 0))],
        out_shape=[sp["s_hcd"], jax.ShapeDtypeStruct((nc, h, d, d), F32)],
        scratch_shapes=[pltpu.VMEM((h, d, d), F32)],
        compiler_params=_params(("arbitrary",)),
    )(u, w, attn, qd, kt, dec)


def _gdn_scan_bwd(do, states, u, w, attn, qd, kt, dec):
    s = u.shape[1]
    nc = s // DN_CHUNK
    h, c, d = DN_HEADS, DN_CHUNK, DN_DIM
    rev = lambda n: nc - 1 - n
    hcd = pl.BlockSpec((h, c, d), lambda n: (0, rev(n), 0))
    att = pl.BlockSpec((h, c, c), lambda n: (0, rev(n), 0))
    dec_s = pl.BlockSpec((h, None, 1, 128), lambda n: (0, rev(n), 0, 0))
    sp = _gdn_specs(s)

    def body(do_ref, st_ref, u_ref, w_ref, at_ref, qd_ref, kt_ref, dec_ref,
             du_ref, dw_ref, dat_ref, dqd_ref, dkt_ref, dde_ref, dstate):
        @pl.when(pl.program_id(0) == 0)
        def _():
            dstate[...] = jnp.zeros_like(dstate)

        _, vjp = jax.vjp(_gdn_step_math, st_ref[...], u_ref[...], w_ref[...], at_ref[...], qd_ref[...], kt_ref[...],
                         dec_ref[...])
        dst, du, dw, dat, dqd, dkt, dde = vjp((dstate[...], do_ref[...]))
        dstate[...] = dst
        for ref, val in zip((du_ref, dw_ref, dat_ref, dqd_ref, dkt_ref, dde_ref), (du, dw, dat, dqd, dkt, dde)):
            ref[...] = val

    return pl.pallas_call(
        body, name="gdn_scan_bwd", grid=(nc,),
        in_specs=[hcd, pl.BlockSpec((None, h, d, d), lambda n: (rev(n), 0, 0, 0)), hcd, hcd, att, hcd, hcd, dec_s],
        out_specs=[hcd, hcd, att, hcd, hcd, dec_s],
        out_shape=[sp["s_hcd"], sp["s_hcd"], sp["s_att"], sp["s_hcd"], sp["s_hcd"], sp["s_dec"]],
        scratch_shapes=[pltpu.VMEM((h, d, d), F32)],
        compiler_params=_params(("arbitrary",)),
    )(do, states, u, w, attn, qd, kt, dec)


def _gdn_out_math(o, z, gain):
    return _rms(o, gain) * _silu(z)


def _gdn_out_fwd(o_rows, z_rows, gain):
    return _rowwise("gdn_out_fwd", _gdn_out_math, [o_rows, z_rows], [(DN_DIM, F32)], bcast=[gain])[0]


def _gdn_out_bwd(dy_rows, o_rows, z_rows, gain):
    def fn(dv, ov, zv, g):
        _, vjp = jax.vjp(_gdn_out_math, ov, zv, g)
        return vjp(dv)
    return _rowwise("gdn_out_bwd", fn, [dy_rows, o_rows, z_rows], [(DN_DIM, F32)] * 2, bcast=[gain], accs=[(1, DN_DIM)])


def _heads_major(x):
    s = x.shape[0]
    return x.reshape(s, -1, DN_DIM).transpose(1, 0, 2)


def _heads_minor(x):
    return x.transpose(1, 0, 2).reshape(x.shape[1], -1)


def _pad_row(v, width=128):
    return jnp.pad(v.reshape(1, -1), ((0, 0), (0, width - v.size)))


def _swiglu_bwd_math(ds, a, b):
    sg = _sigmoid(a)
    return ds * b * sg * (1.0 + a * (1.0 - sg)), ds * a * sg


def _ffn_up(name, h, wg, wu, widx, carry=None):
    s, d = h.shape
    j_n, fs = wg.shape[0], wg.shape[-1]
    tm = _row_tile(s)
    grid = (j_n, s // tm)
    wblk = pl.BlockSpec((None,) * (1 + len(widx)) + (d, fs), lambda j, i: (j, *widx, 0, 0))
    oblk = pl.BlockSpec((tm, fs), lambda j, i: (i, j))

    def body(h_ref, g_ref, u_ref, *refs):
        if carry is not None:
            _, a_ref, b_ref, s_ref, got_ref, send_sems, recv_sems = refs
            first, last = _grid_ends(grid)
            start, finish = _gather_parts(got_ref, send_sems, recv_sems)
            pl.when(first)(start)
        else:
            a_ref, b_ref, s_ref = refs
        hv = h_ref[...]
        a, b = _dotb(hv, g_ref[...]), _dotb(hv, u_ref[...])
        a_ref[...] = a.astype(BF16)
        b_ref[...] = b.astype(BF16)
        s_ref[...] = (_silu(a) * b).astype(BF16)
        if carry is not None:
            pl.when(last)(finish)

    in_specs, args = [pl.BlockSpec((tm, d), lambda j, i: (i, 0)), wblk, wblk], [h, wg, wu]
    out_specs, out_shape, scratch, aliases = [oblk] * 3, [jax.ShapeDtypeStruct((s, j_n * fs), BF16)] * 3, [], {}
    if carry is not None:
        in_specs.append(pl.BlockSpec(memory_space=pl.ANY))
        args.append(carry)
        out_specs.append(pl.BlockSpec(memory_space=pl.ANY))
        out_shape.append(jax.ShapeDtypeStruct(carry.shape, carry.dtype))
        scratch, aliases = [pltpu.SemaphoreType.DMA((6,)), pltpu.SemaphoreType.DMA((6,))], {3: 3}
    return pl.pallas_call(
        body, name=name, grid=grid, in_specs=in_specs, out_specs=out_specs, out_shape=out_shape,
        input_output_aliases=aliases, scratch_shapes=scratch,
        compiler_params=_params(("arbitrary", "arbitrary") if carry is not None else ("parallel", "arbitrary"), 56),
    )(*args)


def _ffn_down_bwd(name, df, wd, a, b, widx):
    s, d = df.shape
    j_n, fs = wd.shape[0], wd.shape[-2]
    tm = _row_tile(s)
    wblk = pl.BlockSpec((None,) * (1 + len(widx)) + (fs, d), lambda j, i: (j, *widx, 0, 0))
    oblk = pl.BlockSpec((tm, fs), lambda j, i: (i, j))

    def body(df_ref, w_ref, a_ref, b_ref, da_ref, db_ref):
        da, db = _swiglu_bwd_math(_dotb(df_ref[...], w_ref[...], NT), a_ref[...].astype(F32), b_ref[...].astype(F32))
        da_ref[...] = da.astype(BF16)
        db_ref[...] = db.astype(BF16)

    return pl.pallas_call(
        body, name=name, grid=(j_n, s // tm), in_specs=[pl.BlockSpec((tm, d), lambda j, i: (i, 0)), wblk, oblk, oblk],
        out_specs=[oblk] * 2, out_shape=[jax.ShapeDtypeStruct((s, j_n * fs), BF16)] * 2,
        compiler_params=_params(("parallel", "arbitrary"), 56),
    )(df, wd, a, b)


def _ffn_fwd(tag, x, gains_in, gains_out, wg, wu, wd, widx, carry_up=None, carry_down=None):
    h = _rms_fwd(f"{tag}_prenorm", x, gains_in)
    a, b, s, *got_up = _ffn_up(f"{tag}_up", h, wg, wu, widx, carry_up)
    f = _mm_rows(f"{tag}_down", s, wd, widx, F32, carry_down)
    f, got_down = f if carry_down is not None else (f, None)
    x_new = _rms_residual(f"{tag}_postnorm", x, f, gains_out, 0.5)
    return x_new, (x, h, a, b, s, f), (got_up[0] if got_up else None), got_down


def _ffn_bwd(tag, dx_new, saved, gains_in, gains_out, wg, wu, wd, widx, gbuf, slab, slab_dims, exchange=None):
    x, h, a, b, s, f = saved
    grad = lambda n, p, q, ac, bc, wt: _mm_grad(f"{tag}_{n}_grad", p, q, ac, bc, lead=slab, lead_dims=slab_dims,
                                               into=gbuf.get(n))
    df, dg_out = _rms_bwd(f"{tag}_postnorm_bwd", dx_new, f, gains_out, 0.5, out_dtype=BF16)
    da, db = _ffn_down_bwd(f"{tag}_down_bwd", df, wd, a, b, widx)
    gbuf["down"] = grad("down", s, df, wd.shape[0], None, wd)
    gbuf["gate"] = grad("gate", h, da, None, wg.shape[0], wg)
    gbuf["up"] = grad("up", h, db, None, wu.shape[0], wu)
    dh = _mm_cols_t(f"{tag}_gateup_bwd", [(da, wg), (db, wu)], widx, F32, exchange)
    dh, received = dh if exchange is not None else (dh, None)
    dx, dg_in = _rms_bwd(f"{tag}_prenorm_bwd", dh, x, gains_in, 1.0, residual=dx_new)
    return dx, dg_in, dg_out, received


def _mixer_fwd(l, h, w, p, bias_tabs, carry_in=None):
    s = h.shape[0]
    nc = s // DN_CHUNK
    proj = _mm_cols("w_in_fwd", h, w["w_in"], (), F32, carry_in)
    proj, got_in = proj if carry_in is not None else (proj, None)
    ops, ops_vjp = jax.vjp(_s5_operators, p["ssm_lambda_re"][l], p["ssm_lambda_im"][l], p["ssm_b_re"][l], p["ssm_b_im"][l],
                           p["ssm_c_re"][l], p["ssm_c_im"][l], p["ssm_d"][l], p["ssm_log_dt"][l])
    u_g = _to_groups(proj[:, :SSM_WIDTH])
    y_g, s_g = _s5_scan_fwd(u_g, *ops)
    y = _from_groups(y_g)
    glu_w, glu_b, gain_ssm = p["ssm_glu_w"][l], p["ssm_glu_b"][l][None], p["ssm_out_gain"][l][None]
    y_ssm = _s5_post_fwd(y, glu_w, glu_b, gain_ssm)
    conv_w, alog, dtb = p["dn_conv_w"][l], _pad_row(p["dn_a_log"][l]), _pad_row(p["dn_dt_bias"][l])
    gain_dn = p["dn_norm_gain"][l][None]
    act = _gdn_prep_fwd(proj, conv_w)
    gates_t = _gates_fwd(proj, alog, dtb)[:, :2 * DN_HEADS].T
    gcol, bcol = gates_t[:DN_HEADS, :, None], gates_t[DN_HEADS:, :, None]
    grow = gates_t[:DN_HEADS].reshape(DN_HEADS, nc, 1, DN_CHUNK)
    u, wy, attn, qd, kt, dec = _gdn_intra_fwd(act, gcol, grow, bcol)
    o_dn, states = _gdn_scan_fwd(u, wy, attn, qd, kt, dec)
    o_rows = o_dn.reshape(DN_HEADS * s, DN_DIM)
    z_rows = _heads_major(proj[:, COL_DZ:COL_DAB]).reshape(DN_HEADS * s, DN_DIM)
    y_dn = _heads_minor(_gdn_out_fwd(o_rows, z_rows, gain_dn).reshape(DN_HEADS, s, DN_DIM))
    outs, lses = zip(*[_attn_branch_fwd(bi, dil, proj, bias_tabs) for bi, (_, dil) in enumerate(DILATED)])
    gain_at = p["attn_out_gain"][l][None]
    y_at, o_at, lse = _attn_merge_fwd(outs, lses, gain_at)
    mix = jnp.concatenate([y_ssm, y_dn, y_at], axis=1).astype(BF16)
    out = _mm_rows("w_out_fwd", mix, w["w_out"], (), F32)
    saved = dict(h=h, proj=proj, ops=ops, ops_vjp=ops_vjp, u_g=u_g, s_g=s_g, y=y, act=act, gcol=gcol, grow=grow, bcol=bcol,
                 u=u, wy=wy, attn=attn, qd=qd, kt=kt, dec=dec, states=states, o_rows=o_rows, z_rows=z_rows,
                 o_at=o_at, lse=lse, mix=mix)
    return out, saved, got_in


def _mixer_bwd(l, dout, sv, w, p, bias_tabs, exchange=None):
    s = dout.shape[0]
    proj = sv["proj"]
    g = {}
    g["w_out"] = _mm_grad("w_out_grad", sv["mix"], dout, w["w_out"].shape[0], None)
    dmix = _mm_rows_t("w_out_bwd", dout, w["w_out"], (), F32)
    d_ssm, d_dn, d_at = dmix[:, :SSM_WIDTH], dmix[:, SSM_WIDTH:SSM_WIDTH + DN_WIDTH], dmix[:, SSM_WIDTH + DN_WIDTH:]
    glu_w, glu_b, gain_ssm = p["ssm_glu_w"][l], p["ssm_glu_b"][l][None], p["ssm_out_gain"][l][None]
    dy, g["ssm_glu_w"], dglu_b, dgain_ssm = _s5_post_bwd(d_ssm, sv["y"], glu_w, glu_b, gain_ssm)
    g["ssm_glu_b"], g["ssm_out_gain"] = dglu_b[0], dgain_ssm[0]
    du_g, *d_ops = _s5_scan_bwd(_to_groups(dy), sv["u_g"], sv["s_g"], *sv["ops"])
    (g["ssm_lambda_re"], g["ssm_lambda_im"], g["ssm_b_re"], g["ssm_b_im"], g["ssm_c_re"], g["ssm_c_im"], g["ssm_d"],
     g["ssm_log_dt"]) = sv["ops_vjp"](tuple(d_ops))
    d_u = _from_groups(du_g)
    gain_at = p["attn_out_gain"][l][None]
    do, dlt, dgain_at = _attn_merge_bwd(d_at, sv["o_at"], gain_at)
    g["attn_out_gain"] = dgain_at[0]
    dqs, dks, dvs, dbs = zip(*[_attn_branch_bwd(bi, dil, proj, do, sv["lse"], dlt, bias_tabs)
                               for bi, (_, dil) in enumerate(DILATED)])
    dq, dk, dv = _add3("attn_dq_sum", *dqs), _add3("attn_dk_sum", *dks), _add3("attn_dv_sum", *dvs)
    g["bias_tabs"] = jnp.stack(dbs)
    conv_w, alog, dtb = p["dn_conv_w"][l], _pad_row(p["dn_a_log"][l]), _pad_row(p["dn_dt_bias"][l])
    gain_dn = p["dn_norm_gain"][l][None]
    dy_rows = _heads_major(d_dn).reshape(DN_HEADS * s, DN_DIM)
    do_rows, dz_rows, dgain_dn = _gdn_out_bwd(dy_rows, sv["o_rows"], sv["z_rows"], gain_dn)
    g["dn_norm_gain"] = dgain_dn[0]
    d_scan = _gdn_scan_bwd(do_rows.reshape(DN_HEADS, s, DN_DIM), sv["states"], sv["u"], sv["wy"], sv["attn"], sv["qd"],
                           sv["kt"], sv["dec"])
    dact, dgc, dgr, dbc = _gdn_intra_bwd(sv["act"], sv["gcol"], sv["grow"], sv["bcol"], *d_scan)
    dgates_t = jnp.concatenate([dgc[..., 0] + dgr.reshape(DN_HEADS, s), dbc[..., 0]], axis=0)
    dgates = jnp.pad(dgates_t.T, ((0, 0), (0, 128 - 2 * DN_HEADS)))
    dab, dalog, ddtb = _gates_bwd(dgates, proj, alog, dtb)
    g["dn_a_log"], g["dn_dt_bias"] = dalog[0, :DN_HEADS], ddtb[0, :DN_HEADS]
    dqkv, g["dn_conv_w"] = _gdn_prep_bwd(dact, proj, conv_w)
    dz = _heads_minor(dz_rows.reshape(DN_HEADS, s, DN_DIM))
    dproj = jnp.concatenate([d_u, dq, dk, dv, dqkv, dz, dab, jnp.zeros((s, N_IN_PAD - COL_DAB - 128), F32)], axis=1)
    g_in = _mm_grad("w_in_grad", sv["h"], dproj, None, 1, exchange=exchange)
    g_in, received = g_in if exchange is not None else (g_in, None)
    g["w_in"] = g_in[0]
    dh = _mm_cols_t("w_in_bwd", [(dproj, w["w_in"])], (), F32)
    return dh, g, received


def _layer_weights(bufs):
    d_model = bufs["w_in"].shape[2] * 2
    w_in = jnp.transpose(bufs["w_in"].reshape(4, d_model, -1), (1, 0, 2)).reshape(d_model, N_IN)
    return {"ffn_w_gate": bufs["ffn_w_gate"], "ffn_w_up": bufs["ffn_w_up"], "ffn_w_down": bufs["ffn_w_down"],
            "w_out": bufs["w_out"].reshape(4, -1, d_model), "w_in": jnp.pad(w_in, ((0, 0), (0, N_IN_PAD - N_IN)))[None]}


def _local_step(x, target, first, later, p, pair_sums):
    depth = p["norm_gains"].shape[0]
    gains = p["norm_gains"]
    bias_tabs, bias_vjp = jax.vjp(_attn_bias_tables, p["rel_bias"])
    saved, weights = [], []
    bufs = first
    for l in range(depth):
        gn = lambda i: gains[l, i][None]
        w = _layer_weights(bufs)
        weights.append(w)
        nxt = later[l] if l < len(later) else {}
        ffn = (w["ffn_w_gate"], w["ffn_w_up"], w["ffn_w_down"])
        got = {}
        x, sv1, got["ffn_w_gate"], got["w_out"] = _ffn_fwd("ffn1", x, gn(0), gn(1), *ffn, (0,), nxt.get("ffn_w_gate"),
                                                          nxt.get("w_out"))
        h = _rms_fwd("mix_prenorm", x, gn(2))
        out, svm, got["w_in"] = _mixer_fwd(l, h, w, p, bias_tabs, nxt.get("w_in"))
        x_mid = x
        x = _rms_residual("mix_postnorm", x, out, gn(3), 1.0)
        x, sv2, got["ffn_w_up"], got["ffn_w_down"] = _ffn_fwd("ffn2", x, gn(4), gn(5), *ffn, (1,), nxt.get("ffn_w_up"),
                                                             nxt.get("ffn_w_down"))
        saved.append((sv1, svm, x_mid, out, sv2))
        bufs = got
    loss, dx = _loss_and_grad(x, target)

    small = ["ssm_lambda_re", "ssm_lambda_im", "ssm_b_re", "ssm_b_im", "ssm_c_re", "ssm_c_im", "ssm_d", "ssm_log_dt",
             "ssm_glu_w", "ssm_glu_b", "ssm_out_gain", "dn_conv_w", "dn_a_log", "dn_dt_bias", "dn_norm_gain", "attn_out_gain"]
    per_layer = {n: [None] * depth for n in small + ["norm_gains", "w_in", "w_out"]}
    d_tabs = None
    ffn_kinds = ("gate", "up", "down")
    received = {n: [None] * depth for n in ffn_kinds}
    sums = {}
    for l in reversed(range(depth)):
        gn = lambda i: gains[l, i][None]
        sv1, svm, x_mid, out, sv2 = saved[l]
        w = weights[l]
        ffn = (w["ffn_w_gate"], w["ffn_w_up"], w["ffn_w_down"])
        gbuf, got = {}, {}
        dx, dg4, dg5, got["gate"] = _ffn_bwd("ffn2", dx, sv2, gn(4), gn(5), *ffn, (1,), gbuf, (1,), (2,), sums.get("gate"))
        dout, dg3 = _rms_bwd("mix_postnorm_bwd", dx, out, gn(3), 1.0)
        dh, gm, got["down"] = _mixer_bwd(l, dout, svm, w, p, bias_tabs, sums.get("down"))
        dx, dg2 = _rms_bwd("mix_prenorm_bwd", dh, x_mid, gn(2), 1.0, residual=dx)
        dx, dg0, dg1, got["up"] = _ffn_bwd("ffn1", dx, sv1, gn(0), gn(1), *ffn, (0,), gbuf, (0,), (2,), sums.get("up"))
        if sums:
            for n in ffn_kinds:
                received[n][l + 1] = got[n]
        sums = {n: pair_sums(n, gbuf[n]) for n in ffn_kinds}
        per_layer["norm_gains"][l] = jnp.concatenate([dg0, dg1, dg2, dg3, dg4, dg5], axis=0)
        d_tabs = gm["bias_tabs"] if d_tabs is None else d_tabs + gm["bias_tabs"]
        for n in small + ["w_in", "w_out"]:
            per_layer[n][l] = gm[n]
    for n in ffn_kinds:
        received[n][0] = _exchange_xy(f"ffn_w_{n}_exchange", sums[n])
    grads = {n: jnp.stack(per_layer[n], axis=0) for n in small + ["norm_gains", "w_in"]}
    grads["w_out"] = jnp.stack(per_layer["w_out"], axis=1)
    grads["ffn_w_gate"], grads["ffn_w_up"], grads["ffn_w_down"] = (received[n] for n in ffn_kinds)
    grads["rel_bias"] = bias_vjp(d_tabs)[0]
    return loss, dx, grads


_ANY = pl.BlockSpec(memory_space=pl.ANY)


def _place():
    return lax.axis_index("x"), lax.axis_index("y"), lax.axis_index("c")


DMA_CHUNKS = 16


class _Transfer:
    def __init__(self, make, src, dst):
        self.make, self.src, self.dst = make, src, dst

    def start(self):
        rows = self.src.shape[0]
        k = DMA_CHUNKS
        while k > 1 and rows % (16 * k):
            k //= 2
        step = rows // k
        for i in range(k):
            self.make(self.src.at[pl.ds(i * step, step)], self.dst.at[pl.ds(i * step, step)]).start()

    def whole(self):
        return self.make(self.src, self.dst)


def _rows_into_block(name, fn, ins, idx, n_blocks, out_dtype, layer=None):
    r, c = ins[0].shape[-2:]
    tm = min(_row_tile(r), 256)

    def body(idx_ref, *refs):
        refs[-1][...] = fn(*[t[...] for t in refs[:-1]]).astype(out_dtype)

    def spec(a):
        if a.ndim == 2:
            return pl.BlockSpec((tm, c), lambda i, b: (i, 0))
        if layer is not None:
            return pl.BlockSpec((None, tm, c), lambda i, b: (layer, i, 0))
        return pl.BlockSpec((a.shape[0], tm, c), lambda i, b: (0, i, 0))

    in_specs = [spec(a) for a in ins]
    return pl.pallas_call(
        body, name=name,
        grid_spec=pltpu.PrefetchScalarGridSpec(
            num_scalar_prefetch=1, grid=(r // tm,), in_specs=in_specs,
            out_specs=pl.BlockSpec((None, tm, c), lambda i, b: (b[0], i, 0))),
        out_shape=jax.ShapeDtypeStruct((n_blocks, r, c), out_dtype),
        compiler_params=_params(("parallel",), 48),
    )(idx, *ins)


def _gather_parts(out, send_sems, recv_sems):
    def place():
        x, y, c = _place()
        return c, (x, y, 1 - c), [(1 - x, y), (x, 1 - y), (1 - x, 1 - y)], 2 * x + y

    blk = lambda chip: 2 * chip[0] + chip[1]

    def copy(k, j, half, to):
        make = lambda s, d: pltpu.make_async_remote_copy(src_ref=s, dst_ref=d, send_sem=send_sems.at[k],
                                                         recv_sem=recv_sems.at[k], device_id=to, device_id_type=MESH)
        return _Transfer(make, out.at[j, half], out.at[j, half])

    def start():
        c, _, chips, me = place()
        for k, chip in enumerate(chips):
            copy(k, me, c, (*chip, c)).start()

    def finish():
        c, sib, chips, me = place()
        passed = [copy(3 + k, blk(chip), c, sib) for k, chip in enumerate(chips)]
        for k, chip in enumerate(chips):
            copy(k, blk(chip), c, (*chip, c)).whole().wait_recv()
            passed[k].start()
        for k, chip in enumerate(chips):
            copy(3 + k, blk(chip), 1 - c, sib).whole().wait_recv()
        for k, chip in enumerate(chips):
            copy(k, me, c, (*chip, c)).whole().wait_send()
        for cp in passed:
            cp.whole().wait_send()

    return start, finish


def _all_gather_xy(name, buf):
    def body(_, out, send_sems, recv_sems):
        start, finish = _gather_parts(out, send_sems, recv_sems)
        start()
        finish()

    return pl.pallas_call(
        body, name=name, in_specs=[_ANY], out_specs=_ANY, input_output_aliases={0: 0},
        out_shape=jax.ShapeDtypeStruct(buf.shape, buf.dtype),
        scratch_shapes=[pltpu.SemaphoreType.DMA((6,)), pltpu.SemaphoreType.DMA((6,))],
    )(buf)


def _pair_send(name, pieces):
    def body(src, dst, send_sems, recv_sems):
        x, y, c = _place()

        def copy(j, src_ref):
            make = lambda s, d: pltpu.make_async_remote_copy(src_ref=s, dst_ref=d, send_sem=send_sems.at[j],
                                                             recv_sem=recv_sems.at[j], device_id=(x, y, 1 - c),
                                                             device_id_type=MESH)
            return _Transfer(make, src_ref, dst.at[j])

        sends = [copy(j, src.at[j, 1 - c]) for j in range(4)]
        for cp in sends:
            cp.start()
        for j in range(4):
            copy(j, src.at[j, c]).whole().wait_recv()
        for cp in sends:
            cp.whole().wait_send()

    return pl.pallas_call(
        body, name=name, in_specs=[_ANY], out_specs=_ANY,
        out_shape=jax.ShapeDtypeStruct((4,) + pieces.shape[2:], pieces.dtype),
        scratch_shapes=[pltpu.SemaphoreType.DMA((4,)), pltpu.SemaphoreType.DMA((4,))],
    )(pieces)


def _pair_sum(name, pieces, recv, core):
    _, _, r, c = pieces.shape
    tm = min(_row_tile(r), 256)

    def body(core_ref, p_ref, r_ref, o_ref):
        o_ref[...] = (p_ref[...].astype(F32) + r_ref[...].astype(F32)).astype(o_ref.dtype)

    return pl.pallas_call(
        body, name=name,
        grid_spec=pltpu.PrefetchScalarGridSpec(
            num_scalar_prefetch=1, grid=(4, r // tm),
            in_specs=[pl.BlockSpec((None, None, tm, c), lambda j, i, b: (j, b[0], i, 0)),
                      pl.BlockSpec((None, tm, c), lambda j, i, b: (j, i, 0))],
            out_specs=pl.BlockSpec((None, tm, c), lambda j, i, b: (j, i, 0))),
        out_shape=jax.ShapeDtypeStruct((4, r, c), pieces.dtype),
        compiler_params=_params(("parallel", "parallel"), 48),
    )(core, pieces, recv)


def _exchange_parts(src, dst, send_sems, recv_sems):
    def place():
        x, y, c = _place()
        return c, [(1 - x, y), (x, 1 - y), (1 - x, 1 - y)], 2 * x + y

    blk = lambda chip: 2 * chip[0] + chip[1]

    def copy(k, chip, c, dst_block):
        make = lambda s, d: pltpu.make_async_remote_copy(src_ref=s, dst_ref=d, send_sem=send_sems.at[k],
                                                         recv_sem=recv_sems.at[k], device_id=(*chip, c),
                                                         device_id_type=MESH)
        return _Transfer(make, src.at[blk(chip)], dst.at[dst_block])

    def own(me):
        return _Transfer(lambda s, d: pltpu.make_async_copy(s, d, send_sems.at[3]), src.at[me], dst.at[me])

    def start():
        c, chips, me = place()
        own(me).start()
        for k, chip in enumerate(chips):
            copy(k, chip, c, me).start()

    def finish():
        c, chips, me = place()
        for k, chip in enumerate(chips):
            copy(k, chip, c, blk(chip)).whole().wait_recv()
        for k, chip in enumerate(chips):
            copy(k, chip, c, me).whole().wait_send()
        own(me).whole().wait()

    return start, finish


def _exchange_xy(name, part):
    def body(src, dst, send_sems, recv_sems):
        start, finish = _exchange_parts(src, dst, send_sems, recv_sems)
        start()
        finish()

    return pl.pallas_call(
        body, name=name, in_specs=[_ANY], out_specs=_ANY, out_shape=jax.ShapeDtypeStruct(part.shape, part.dtype),
        scratch_shapes=[pltpu.SemaphoreType.DMA((4,)), pltpu.SemaphoreType.DMA((4,))],
    )(part)


def _exchange8(name, src, same_to_all=False):
    blk_shape = src.shape if same_to_all else src.shape[1:]

    def body(src_ref, dst, send_sems, recv_sems, local_sem):
        x, y, c = _place()
        me = 4 * x + 2 * y + c
        part = (lambda i: src_ref) if same_to_all else (lambda i: src_ref.at[i])

        def peer(k):
            return (1 - x if k & 4 else x, 1 - y if k & 2 else y, 1 - c if k & 1 else c)

        def copy(k, dst_block):
            px, py, pc = peer(k)
            make = lambda s, d: pltpu.make_async_remote_copy(src_ref=s, dst_ref=d, send_sem=send_sems.at[k - 1],
                                                             recv_sem=recv_sems.at[k - 1], device_id=(px, py, pc),
                                                             device_id_type=MESH)
            return _Transfer(make, part(4 * px + 2 * py + pc), dst.at[dst_block])

        mine = _Transfer(lambda s, d: pltpu.make_async_copy(s, d, local_sem), part(me), dst.at[me])
        mine.start()
        sends = [copy(k, me) for k in range(1, 8)]
        for cp in sends:
            cp.start()
        for k in range(1, 8):
            px, py, pc = peer(k)
            copy(k, 4 * px + 2 * py + pc).whole().wait_recv()
        for cp in sends:
            cp.whole().wait_send()
        mine.whole().wait()

    return pl.pallas_call(
        body, name=name, in_specs=[_ANY], out_specs=_ANY,
        out_shape=jax.ShapeDtypeStruct((8,) + blk_shape, src.dtype),
        scratch_shapes=[pltpu.SemaphoreType.DMA((7,)), pltpu.SemaphoreType.DMA((7,)), pltpu.SemaphoreType.DMA],
    )(src)


def _pair_swap(name, both):
    def body(_, out, send_sem, recv_sem):
        x, y, c = _place()
        remote = lambda s, d: pltpu.make_async_remote_copy(src_ref=s, dst_ref=d, send_sem=send_sem, recv_sem=recv_sem,
                                                           device_id=(x, y, 1 - c), device_id_type=MESH)
        push = _Transfer(remote, out.at[c], out.at[c])
        push.start()
        remote(out.at[c], out.at[1 - c]).wait_recv()
        push.whole().wait_send()

    return pl.pallas_call(
        body, name=name, in_specs=[_ANY], out_specs=_ANY, input_output_aliases={0: 0},
        out_shape=jax.ShapeDtypeStruct(both.shape, both.dtype),
        scratch_shapes=[pltpu.SemaphoreType.DMA, pltpu.SemaphoreType.DMA],
    )(both)


def _sum8(name, parts):
    _, r, c = parts.shape
    tm = min(_row_tile(r), 256)

    def body(p_ref, o_ref):
        o_ref[...] = _sum_blocks(p_ref[...])

    return pl.pallas_call(
        body, name=name, grid=(r // tm,), in_specs=[pl.BlockSpec((8, tm, c), lambda i: (0, i, 0))],
        out_specs=pl.BlockSpec((tm, c), lambda i: (i, 0)), out_shape=jax.ShapeDtypeStruct((r, c), F32),
        compiler_params=_params(("parallel",), 48),
    )(parts)


def _adamw(name, g, w, m, v):
    def fn(gv, wv, mv, vv):
        m2 = ADAM_B1 * mv + (1.0 - ADAM_B1) * gv
        v2 = ADAM_B2 * vv + (1.0 - ADAM_B2) * (gv * gv)
        m_hat = m2 / (1.0 - ADAM_B1 ** ADAM_STEP)
        v_hat = v2 / (1.0 - ADAM_B2 ** ADAM_STEP)
        return -ADAM_LR * (m_hat / (jnp.sqrt(v_hat) + ADAM_EPS) + ADAM_WD * wv), m2, v2
    return _rowwise(name, fn, [g, w, m, v], [(g.shape[1], F32)] * 3)


def _pack(tensors):
    flat = jnp.concatenate([t.reshape(-1).astype(F32) for t in tensors])
    rows = -(-flat.size // (128 * 512)) * 512
    return jnp.pad(flat, (0, rows * 128 - flat.size)).reshape(rows, 128)


def _unpack(buf, shapes):
    flat, out, at = buf.reshape(-1), [], 0
    for sh in shapes:
        n = math.prod(sh)
        out.append(flat[at:at + n].reshape(sh))
        at += n
    return out


_WEIGHTS = ("norm_gains", "ffn_w_gate", "ffn_w_up", "ffn_w_down", "w_in", "w_out", "ssm_lambda_re", "ssm_lambda_im",
            "ssm_b_re", "ssm_b_im", "ssm_c_re", "ssm_c_im", "ssm_d", "ssm_log_dt", "ssm_glu_w", "ssm_glu_b",
            "ssm_out_gain", "dn_conv_w", "dn_a_log", "dn_dt_bias", "dn_norm_gain", "attn_out_gain", "rel_bias")
_MATRICES = ("ffn_w_gate", "ffn_w_up", "ffn_w_down", "w_in", "w_out")
_CUT_SMALL = {"norm_gains": 2, "ssm_glu_w": 1, "dn_conv_w": 2}
_REPLICATED = tuple(n for n in _WEIGHTS if n not in _MATRICES and n not in _CUT_SMALL)


def _sum_blocks(x):
    acc = x[0].astype(F32)
    for i in range(1, x.shape[0]):
        acc = acc + x[i].astype(F32)
    return acc


def _gather_matrix(name, shard, chip, dtype=BF16):
    c = shard.shape[-1]
    buf = _rows_into_block(f"{name}_cast", lambda v: v, [shard.reshape(-1, c)], chip, 4, dtype)
    return _all_gather_xy(f"{name}_gather", buf.reshape(4, 2, -1, c)).reshape((4,) + shard.shape)


def _reduce_matrix(name, g, shard_shape, core):
    c = g.shape[-1]
    pieces = g.reshape(4, 2, -1, c)
    part = _pair_sum(f"{name}_pair_sum", pieces, _pair_send(f"{name}_pair", pieces), core)
    both = _rows_into_block(f"{name}_sum", _sum_blocks, [_exchange_xy(f"{name}_exchange", part)], core, 2, F32)
    return _pair_swap(f"{name}_swap", both).reshape(shard_shape)


def kernel(x, norm_gains, ffn_w_gate, ffn_w_up, ffn_w_down, w_in, w_out, ssm_lambda_re, ssm_lambda_im, ssm_b_re, ssm_b_im, ssm_c_re, ssm_c_im, ssm_d, ssm_log_dt, ssm_glu_w, ssm_glu_b, ssm_out_gain, dn_conv_w, dn_a_log, dn_dt_bias, dn_norm_gain, attn_out_gain, rel_bias, loss_target, m_norm_gains, m_ffn_w_gate, m_ffn_w_up, m_ffn_w_down, m_w_in, m_w_out, m_ssm_lambda_re, m_ssm_lambda_im, m_ssm_b_re, m_ssm_b_im, m_ssm_c_re, m_ssm_c_im, m_ssm_d, m_ssm_log_dt, m_ssm_glu_w, m_ssm_glu_b, m_ssm_out_gain, m_dn_conv_w, m_dn_a_log, m_dn_dt_bias, m_dn_norm_gain, m_attn_out_gain, m_rel_bias, v_norm_gains, v_ffn_w_gate, v_ffn_w_up, v_ffn_w_down, v_w_in, v_w_out, v_ssm_lambda_re, v_ssm_lambda_im, v_ssm_b_re, v_ssm_b_im, v_ssm_c_re, v_ssm_c_im, v_ssm_d, v_ssm_log_dt, v_ssm_glu_w, v_ssm_glu_b, v_ssm_out_gain, v_dn_conv_w, v_dn_a_log, v_dn_dt_bias, v_dn_norm_gain, v_attn_out_gain, v_rel_bias):
    wts = dict(zip(_WEIGHTS, (norm_gains, ffn_w_gate, ffn_w_up, ffn_w_down, w_in, w_out, ssm_lambda_re, ssm_lambda_im, ssm_b_re, ssm_b_im, ssm_c_re, ssm_c_im, ssm_d, ssm_log_dt, ssm_glu_w, ssm_glu_b, ssm_out_gain, dn_conv_w, dn_a_log, dn_dt_bias, dn_norm_gain, attn_out_gain, rel_bias)))
    mom = dict(zip(_WEIGHTS, (m_norm_gains, m_ffn_w_gate, m_ffn_w_up, m_ffn_w_down, m_w_in, m_w_out, m_ssm_lambda_re, m_ssm_lambda_im, m_ssm_b_re, m_ssm_b_im, m_ssm_c_re, m_ssm_c_im, m_ssm_d, m_ssm_log_dt, m_ssm_glu_w, m_ssm_glu_b, m_ssm_out_gain, m_dn_conv_w, m_dn_a_log, m_dn_dt_bias, m_dn_norm_gain, m_attn_out_gain, m_rel_bias)))
    var = dict(zip(_WEIGHTS, (v_norm_gains, v_ffn_w_gate, v_ffn_w_up, v_ffn_w_down, v_w_in, v_w_out, v_ssm_lambda_re, v_ssm_lambda_im, v_ssm_b_re, v_ssm_b_im, v_ssm_c_re, v_ssm_c_im, v_ssm_d, v_ssm_log_dt, v_ssm_glu_w, v_ssm_glu_b, v_ssm_out_gain, v_dn_conv_w, v_dn_a_log, v_dn_dt_bias, v_dn_norm_gain, v_attn_out_gain, v_rel_bias)))
    depth, d_model = norm_gains.shape[0], x.shape[-1]
    chip = 2 * lax.axis_index("x") + lax.axis_index("y")
    chip_idx = chip.astype(jnp.int32).reshape(1)
    core_idx = lax.axis_index("c").astype(jnp.int32).reshape(1)

    def layer_buffer(n, l):
        c = wts[n].shape[-1]
        rows = wts[n].reshape(depth, -1, c)
        return _rows_into_block(f"{n}_cast", lambda v: v, [rows], chip_idx, 4, BF16, layer=l).reshape(4, 2, -1, c)

    layers = [{n: layer_buffer(n, l) for n in _MATRICES} for l in range(depth)]
    first = {n: _all_gather_xy(f"{n}_gather", layers[0][n]) for n in _MATRICES}
    cut_names = tuple(_CUT_SMALL)
    cut_all = _gather_matrix("small", _pack([wts[n] for n in cut_names]), chip_idx, F32)
    p = {n: wts[n] for n in _REPLICATED}
    per_chip = [_unpack(cut_all[j], [wts[n].shape for n in cut_names]) for j in range(4)]
    for i, n in enumerate(cut_names):
        p[n] = jnp.concatenate([per_chip[j][i] for j in range(4)], axis=_CUT_SMALL[n])

    def pair_sums(kind, g):
        return _pair_sum(f"ffn_w_{kind}_pair_sum", g, _pair_send(f"ffn_w_{kind}_pair", g), core_idx)

    loss, dx, grads = _local_step(x[0], loss_target[0], first, layers[1:], p, pair_sums)
    loss = lax.psum(loss, ("x", "y", "c"))

    total = {}
    for n in ("ffn_w_gate", "ffn_w_up", "ffn_w_down"):
        halves = [_pair_swap(f"{n}_swap", _rows_into_block(f"{n}_sum", _sum_blocks, [got], core_idx, 2, F32))
                  for got in grads[n]]
        total[n] = jnp.stack(halves).reshape(wts[n].shape)
    total["w_out"] = _reduce_matrix("w_out", grads["w_out"], wts["w_out"].shape, core_idx)
    g_in = grads["w_in"][:, :, :N_IN].reshape(depth, d_model, 4, N_IN // 4)
    total["w_in"] = _reduce_matrix("w_in", jnp.transpose(g_in, (2, 0, 1, 3)), w_in.shape, core_idx)
    small_names = _REPLICATED + cut_names
    small_sum = _sum8("small_sum", _exchange8("small_exchange", _pack([grads[n] for n in small_names]), same_to_all=True))
    for n, g in zip(small_names, _unpack(small_sum, [grads[n].shape for n in small_names])):
        if n in _CUT_SMALL:
            cuts = jnp.split(g, 4, axis=_CUT_SMALL[n])
            g = functools.reduce(lambda acc, j: jnp.where(chip == j, cuts[j], acc), range(1, 4), cuts[0])
        total[n] = g

    delta, new_m, new_v = {}, {}, {}
    for n in _MATRICES + cut_names:
        c = wts[n].shape[-1]
        d2, m2, v2 = _adamw(f"{n}_adamw", total[n].reshape(-1, c), wts[n].reshape(-1, c), mom[n].reshape(-1, c),
                            var[n].reshape(-1, c))
        delta[n], new_m[n], new_v[n] = (t.reshape(wts[n].shape) for t in (d2, m2, v2))
    rep_shapes = [wts[n].shape for n in _REPLICATED]
    packed = _adamw("small_adamw", _pack([total[n] for n in _REPLICATED]), _pack([wts[n] for n in _REPLICATED]),
                    _pack([mom[n] for n in _REPLICATED]), _pack([var[n] for n in _REPLICATED]))
    for dst, buf in zip((delta, new_m, new_v), packed):
        dst.update(zip(_REPLICATED, _unpack(buf, rep_shapes)))

    return (loss, dx[None], *[total[n] for n in _WEIGHTS], *[delta[n] for n in _WEIGHTS],
            *[new_m[n] for n in _WEIGHTS], *[new_v[n] for n in _WEIGHTS])
```
